```python
import math
import jax
import jax.numpy as jnp
from jax import lax
import numpy as np

D_MODEL = 1024
BATCH = 16
SEQ = 2048
DEPTH = 2

CTX_LEN = 256
GRID_W = 64
N_MIXERS = 4
GROUP_W = D_MODEL // N_MIXERS
MIX_W = N_MIXERS * GROUP_W
EPS = 1e-6
F32 = jnp.float32

HG_HEADS = 4
HG_DK = GROUP_W // HG_HEADS
HG_DV = GROUP_W // HG_HEADS
HG_CHUNK = 16
S5_CH = 16
S5_GROUPS = GROUP_W // S5_CH
S5_STATE = 64
DT_MIN = 1e-3
DT_MAX = 1e-1
DA_HEADS = 4
DA_DQK = GROUP_W // (2 * DA_HEADS)
DA_DV = GROUP_W // DA_HEADS
Q_BLOCK = 128
ROPE_BASE = 10000.0
ML_HEADS = 4
ML_DK = GROUP_W // ML_HEADS
ML_DV = GROUP_W // ML_HEADS
ML_CHUNK = 64
N_EXPERTS = 16
EC_CAPACITY = 2
D_EXPERT = 2 * D_MODEL

HG_OFF = 0
S5_OFF = HG_OFF + 5 * GROUP_W
DA_OFF = S5_OFF + GROUP_W
ML_OFF = DA_OFF + 3 * GROUP_W
ML_GATE_OFF = ML_OFF + 4 * GROUP_W
IN_COLS = ML_GATE_OFF + 4 * ML_HEADS

kernel_name = 'hybrid_diffusion_parallel_groups_ecmoe'


def rms_norm(x, w):
    xf = x.astype(F32)
    y = xf * lax.rsqrt(jnp.mean(xf * xf, axis=-1, keepdims=True) + EPS)
    return (y * w.astype(F32)).astype(x.dtype)


def head_rms_norm(x, w, n_heads):
    shp = x.shape
    xh = x.reshape(shp[:-1] + (n_heads, -1))
    return rms_norm(xh, w.reshape(n_heads, -1)).reshape(shp)


def modulated(h, norm_w, shift, scale):
    return rms_norm(h, norm_w) * (1.0 + scale) + shift


def flip_seq(a, reverse):
    return a[:, ::-1] if reverse else a


def rope_2d(n, dim):
    axis_dim = dim // 2
    inv = ROPE_BASE ** (-jnp.arange(0, axis_dim, 2, dtype=F32) / axis_dim)
    t = jnp.arange(n, dtype=jnp.int32)
    row = (t // GRID_W).astype(F32)
    col = (t % GRID_W).astype(F32)
    ang = jnp.concatenate([row[:, None] * inv, col[:, None] * inv], axis=-1)
    return jnp.cos(ang), jnp.sin(ang)


def apply_rope(x, cos, sin):
    xr = x.astype(F32).reshape(x.shape[:-1] + (-1, 2))
    x1, x2 = xr[..., 0], xr[..., 1]
    shp = (1, cos.shape[0]) + (1,) * (x.ndim - 3) + (cos.shape[1],)
    c, s = cos.reshape(shp), sin.reshape(shp)
    y = jnp.stack([x1 * c - x2 * s, x1 * s + x2 * c], axis=-1)
    return y.reshape(x.shape).astype(x.dtype)


def gla_chunked(q, k, v, log_f, s0, with_out=True):
    bsz, t_len, nh, _ = q.shape
    nc = t_len // HG_CHUNK
    def chunks(a):
        return a.astype(F32).reshape(bsz, nc, HG_CHUNK, nh, a.shape[-1])
    q, k, v, log_f = chunks(q), chunks(k), chunks(v), chunks(log_f)
    b = jnp.cumsum(log_f, axis=2)
    b_end = b[:, :, -1]
    ds = jnp.einsum('bcshd,bcshv->bchdv', k * jnp.exp(b_end[:, :, None] - b), v)
    def step(s, inp):
        g_c, ds_c = inp
        return jnp.exp(g_c)[..., None] * s + ds_c, s
    s_fin, s_start = lax.scan(step, s0, (jnp.moveaxis(b_end, 1, 0), jnp.moveaxis(ds, 1, 0)))
    if not with_out:
        return None, s_fin
    s_start = jnp.moveaxis(s_start, 0, 1)
    tri = jnp.tril(jnp.ones((HG_CHUNK, HG_CHUNK), bool))[None, None, :, :, None, None]
    decay = jnp.exp(jnp.where(tri, b[:, :, :, None] - b[:, :, None], -jnp.inf))
    scores = jnp.sum(q[:, :, :, None] * decay * k[:, :, None], axis=-1)
    o = (jnp.einsum('bctsh,bcshv->bcthv', scores, v)
         + jnp.einsum('bcthd,bchdv->bcthv', q * jnp.exp(b), s_start))
    return o.reshape(bsz, t_len, nh, -1), s_fin


def hgrn2_mixer(zc, zl, lb, norm_w, with_ctx_out):
    def heads(a):
        return a.reshape(a.shape[:2] + (HG_HEADS, -1))
    def split(z):
        q, i, ff, fb, g = jnp.split(z, 5, axis=-1)
        return heads(q) * HG_DK ** -0.5, heads(i), (heads(ff), heads(fb)), g
    qc, ic, fc, gc = split(zc)
    ql, il, fl, gl = split(zl)
    oc, ol = 0.0, 0.0
    for d in range(2):
        lbd = lb[d].reshape(HG_HEADS, HG_DK)
        f_c = lbd + (1.0 - lbd) * jax.nn.sigmoid(fc[d].astype(F32))
        f_l = lbd + (1.0 - lbd) * jax.nn.sigmoid(fl[d].astype(F32))
        s0 = jnp.zeros((zc.shape[0], HG_HEADS, HG_DK, HG_DV), F32)
        o_c, s_c = gla_chunked(flip_seq(qc, d), flip_seq(1.0 - f_c, d), flip_seq(ic, d),
                               flip_seq(jnp.log(f_c), d), s0, with_ctx_out)
        o_l, _ = gla_chunked(flip_seq(ql, d), flip_seq(1.0 - f_l, d), flip_seq(il, d),
                             flip_seq(jnp.log(f_l), d), s_c)
        ol = ol + flip_seq(o_l, d)
        if with_ctx_out:
            oc = oc + flip_seq(o_c, d)
    def readout(o, g):
        o = o.reshape(o.shape[:2] + (-1,)).astype(g.dtype)
        return head_rms_norm(o, norm_w, HG_HEADS) * jax.nn.silu(g)
    return (readout(oc, gc) if with_ctx_out else None), readout(ol, gl)


def cmul(ar, ai, br, bi):
    return ar * br - ai * bi, ar * bi + ai * br


def s5_combine(e1, e2):
    a1r, a1i, x1r, x1i = e1
    a2r, a2i, x2r, x2i = e2
    ar, ai = cmul(a2r, a2i, a1r, a1i)
    yr, yi = cmul(a2r, a2i, x1r, x1i)
    return ar, ai, yr + x2r, yi + x2i


def s5_mixer(uc, ul, lam_re, lam_im, log_dt, b_re, b_im, c_re, c_im, d_skip, glu_w, glu_b, with_ctx_out):
    def groups(u):
        return u.astype(F32).reshape(u.shape[:2] + (S5_GROUPS, S5_CH))
    gc, gl = groups(uc), groups(ul)
    b_re, b_im, c_re, c_im = (a.astype(F32) for a in (b_re, b_im, c_re, c_im))
    def b_proj(u):
        return jnp.einsum('bngc,gpc->bngp', u, b_re), jnp.einsum('bngc,gpc->bngp', u, b_im)
    def c_read(xr, xi):
        return jnp.einsum('bngp,gcp->bngc', xr, c_re) - jnp.einsum('bngp,gcp->bngc', xi, c_im)
    buc, bul = b_proj(gc), b_proj(gl)
    yc, yl = 0.0, 0.0
    for d in range(2):
        lr, li = lam_re[d].astype(F32), lam_im[d].astype(F32)
        dt = jnp.exp(log_dt[d].astype(F32))[:, None]
        mag = jnp.exp(lr * dt)
        ab_re, ab_im = mag * jnp.cos(li * dt), mag * jnp.sin(li * dt)
        den = lr * lr + li * li
        co_re = ((ab_re - 1.0) * lr + ab_im * li) / den
        co_im = (ab_im * lr - (ab_re - 1.0) * li) / den
        def run(bu, x0):
            br, bi = cmul(co_re, co_im, flip_seq(bu[0], d), flip_seq(bu[1], d))
            a_shape = (1, br.shape[1]) + ab_re.shape
            ar, ai, xr, xi = lax.associative_scan(
                s5_combine, (jnp.broadcast_to(ab_re, a_shape), jnp.broadcast_to(ab_im, a_shape), br, bi), axis=1)
            if x0 is not None:
                pr, pm = cmul(ar, ai, x0[0][:, None], x0[1][:, None])
                xr, xi = xr + pr, xi + pm
            return xr, xi
        xcr, xci = run(buc, None)
        xlr, xli = run(bul, (xcr[:, -1], xci[:, -1]))
        yl = yl + flip_seq(c_read(xlr, xli), d)
        if with_ctx_out:
            yc = yc + flip_seq(c_read(xcr, xci), d)
    dsk = d_skip.astype(F32).reshape(S5_GROUPS, S5_CH)
    def glu(y, g, like):
        y = jax.nn.gelu((y + dsk * g).reshape(g.shape[:2] + (GROUP_W,)))
        return (y * jax.nn.sigmoid(y @ glu_w.astype(F32) + glu_b.astype(F32))).astype(like.dtype)
    return (glu(yc, gc, uc) if with_ctx_out else None), glu(yl, gl, ul)


def diff_softmax_blocks(q, k, v, lam):
    bsz, nq = q.shape[:2]
    nb = nq // Q_BLOCK
    scale = DA_DQK ** -0.5
    def block(qb):
        s = jnp.einsum('bqhmd,bkhmd->bhmqk', qb, k).astype(F32) * scale
        p = jax.nn.softmax(s, axis=-1)
        a = p[:, :, 0] - lam * p[:, :, 1]
        return jnp.einsum('bhqk,bkhv->bqhv', a.astype(v.dtype), v)
    qb = jnp.moveaxis(q.reshape((bsz, nb, Q_BLOCK) + q.shape[2:]), 1, 0)
    o = lax.map(block, qb)
    return jnp.moveaxis(o, 0, 1).reshape((bsz, nq) + o.shape[3:])


def diff_attention(zc, zl, lam_q1, lam_k1, lam_q2, lam_k2, norm_w, layer_idx, with_ctx_out):
    def split(z):
        q, k, v = jnp.split(z, 3, axis=-1)
        shp = z.shape[:2]
        return (q.reshape(shp + (DA_HEADS, 2, DA_DQK)), k.reshape(shp + (DA_HEADS, 2, DA_DQK)),
                v.reshape(shp + (DA_HEADS, DA_DV)))
    qc, kc, vc = split(zc)
    ql, kl, vl = split(zl)
    cos, sin = rope_2d(zl.shape[1], DA_DQK)
    ql, kl = apply_rope(ql, cos, sin), apply_rope(kl, cos, sin)
    lam_init = 0.8 - 0.6 * math.exp(-0.3 * layer_idx)
    lam = (jnp.exp(jnp.sum(lam_q1.astype(F32) * lam_k1.astype(F32)))
           - jnp.exp(jnp.sum(lam_q2.astype(F32) * lam_k2.astype(F32))) + lam_init)
    def readout(o):
        return head_rms_norm(o.reshape(o.shape[:2] + (-1,)), norm_w, DA_HEADS) * (1.0 - lam_init)
    out_l = readout(diff_softmax_blocks(ql, jnp.concatenate([kc, kl], axis=1),
                                        jnp.concatenate([vc, vl], axis=1), lam))
    out_c = readout(diff_softmax_blocks(qc, kc, vc, lam)) if with_ctx_out else None
    return out_c, out_l


def mlstm_chunked(q, k, v, log_i, log_f, state, with_out=True):
    bsz, t_len, nh, _ = q.shape
    nc = t_len // ML_CHUNK
    def chunks(a):
        return a.astype(F32).reshape((bsz, nc, ML_CHUNK) + a.shape[2:])
    q, k, v, log_i, log_f = chunks(q), chunks(k), chunks(v), chunks(log_i), chunks(log_f)
    b = jnp.cumsum(log_f, axis=2)
    b_end = b[:, :, -1]
    w_end = b_end[:, :, None] - b + log_i
    m_loc = jnp.max(w_end, axis=2)
    e_end = jnp.exp(w_end - m_loc[:, :, None])
    c_loc = jnp.einsum('bcshv,bcshd->bchvd', e_end[..., None] * v, k)
    n_loc = jnp.einsum('bcsh,bcshd->bchd', e_end, k)
    def step(carry, inp):
        c_st, n_st, m_st = carry
        be, ml, cl, nl = inp
        m_new = jnp.maximum(be + m_st, ml)
        a_old, a_loc = jnp.exp(be + m_st - m_new), jnp.exp(ml - m_new)
        c_new = a_old[..., None, None] * c_st + a_loc[..., None, None] * cl
        n_new = a_old[..., None] * n_st + a_loc[..., None] * nl
        return (c_new, n_new, m_new), carry
    mv = lambda a: jnp.moveaxis(a, 1, 0)
    final, starts = lax.scan(step, state, (mv(b_end), mv(m_loc), mv(c_loc), mv(n_loc)))
    if not with_out:
        return None, final
    c_s, n_s, m_s = (jnp.moveaxis(a, 0, 1) for a in starts)
    tri = jnp.tril(jnp.ones((ML_CHUNK, ML_CHUNK), bool))[None, None, :, :, None]
    d_log = jnp.where(tri, b[:, :, :, None] - b[:, :, None] + log_i[:, :, None], -jnp.inf)
    w_inter = b + m_s[:, :, None]
    m_t = jnp.maximum(w_inter, jnp.max(d_log, axis=3))
    e_inter = jnp.exp(w_inter - m_t)
    wts = jnp.exp(d_log - m_t[:, :, :, None]) * jnp.einsum('bcthd,bcshd->bctsh', q, k)
    num = (jnp.einsum('bctsh,bcshv->bcthv', wts, v)
           + e_inter[..., None] * jnp.einsum('bcthd,bchvd->bcthv', q, c_s))
    den = jnp.sum(wts, axis=3) + e_inter * jnp.einsum('bcthd,bchd->bcth', q, n_s)
    h = num / jnp.maximum(jnp.abs(den), jnp.exp(-m_t))[..., None]
    return h.reshape(bsz, t_len, nh, -1), final


def mlstm_mixer(zc, zl, norm_w, with_ctx_out):
    def split(z):
        shp = z.shape[:2]
        q, k, v, o = (z[..., j * GROUP_W:(j + 1) * GROUP_W] for j in range(4))
        gates = z[..., 4 * GROUP_W:].astype(F32).reshape(shp + (4, ML_HEADS))
        hd = lambda a: a.reshape(shp + (ML_HEADS, -1))
        return hd(q), hd(k) * ML_DK ** -0.5, hd(v), o, gates
    qc, kc, vc, oc, gtc = split(zc)
    ql, kl, vl, ol, gtl = split(zl)
    bsz = zc.shape[0]
    hc, hl = 0.0, 0.0
    for d in range(2):
        state0 = (jnp.zeros((bsz, ML_HEADS, ML_DV, ML_DK), F32), jnp.zeros((bsz, ML_HEADS, ML_DK), F32),
                  jnp.zeros((bsz, ML_HEADS), F32))
        h_c, st_c = mlstm_chunked(flip_seq(qc, d), flip_seq(kc, d), flip_seq(vc, d), flip_seq(gtc[:, :, d], d),
                                  flip_seq(jax.nn.log_sigmoid(gtc[:, :, 2 + d]), d), state0, with_ctx_out)
        h_l, _ = mlstm_chunked(flip_seq(ql, d), flip_seq(kl, d), flip_seq(vl, d), flip_seq(gtl[:, :, d], d),
                               flip_seq(jax.nn.log_sigmoid(gtl[:, :, 2 + d]), d), st_c)
        hl = hl + flip_seq(h_l, d)
        if with_ctx_out:
            hc = hc + flip_seq(h_c, d)
    def readout(h, o):
        h = h.reshape(h.shape[:2] + (-1,)).astype(o.dtype)
        return head_rms_norm(h, norm_w, ML_HEADS) * jax.nn.sigmoid(o)
    return (readout(hc, oc) if with_ctx_out else None), readout(hl, ol)


def expert_choice_moe(x, router_w, w1, w3, w2):
    bsz, n, _ = x.shape
    cap = EC_CAPACITY * n // N_EXPERTS
    aff = jax.nn.softmax((x @ router_w).astype(F32), axis=-1)
    gate, idx = lax.top_k(jnp.swapaxes(aff, 1, 2), cap)
    bidx = jnp.arange(bsz)[:, None, None]
    xs = x[bidx, idx]
    def expert(args):
        xe, w1e, w3e, w2e = args
        return (jax.nn.silu(xe @ w1e) * (xe @ w3e)) @ w2e
    ye = lax.map(expert, (jnp.swapaxes(xs, 0, 1), w1, w3, w2))
    ye = jnp.swapaxes(ye, 0, 1) * gate[..., None].astype(x.dtype)
    return jnp.zeros_like(x).at[bidx, idx].add(ye)


def setup_inputs(seed: int = 0) -> dict:
    key = jax.random.key(seed)
    it = iter(jax.random.split(key, 48))
    def nrm(shape, scale):
        return scale * jax.random.normal(next(it), shape, F32)
    L, D = DEPTH, D_MODEL
    x = nrm((BATCH, SEQ, D), 1.0)
    c = nrm((BATCH, D), 1.0)
    ctx = nrm((BATCH, CTX_LEN, D), 1.0)
    c_ctx = nrm((D,), 1.0)
    mod_w = nrm((L, D, 6 * D), 0.5 * D ** -0.5)
    mod_b = nrm((L, 6 * D), 0.02)
    norm1_w = 1.0 + nrm((L, D), 0.02)
    norm2_w = 1.0 + nrm((L, D), 0.02)
    w_in = nrm((L, D, IN_COLS), D ** -0.5)
    fg_bias = jnp.linspace(3.0, 6.0, ML_HEADS, dtype=F32)
    b_in = nrm((L, IN_COLS), 0.02).at[:, ML_GATE_OFF + 2 * ML_HEADS:].add(jnp.concatenate([fg_bias, fg_bias]))
    hg_lb_logits = nrm((L, 2, GROUP_W), 1.0)
    hg_norm_w = 1.0 + nrm((L, GROUP_W), 0.02)
    s5_lam_re = -0.5 + nrm((L, 2, S5_GROUPS, S5_STATE), 0.01)
    s5_lam_im = math.pi * jnp.arange(S5_STATE, dtype=F32) + nrm((L, 2, S5_GROUPS, S5_STATE), 0.01)
    s5_log_dt = math.log(DT_MIN) + (math.log(DT_MAX) - math.log(DT_MIN)) * jax.random.uniform(
        next(it), (L, 2, S5_GROUPS), F32)
    s5_b_re = nrm((L, S5_GROUPS, S5_STATE, S5_CH), (2 * S5_CH) ** -0.5)
    s5_b_im = nrm((L, S5_GROUPS, S5_STATE, S5_CH), (2 * S5_CH) ** -0.5)
    s5_c_re = nrm((L, S5_GROUPS, S5_CH, S5_STATE), (2 * S5_STATE) ** -0.5)
    s5_c_im = nrm((L, S5_GROUPS, S5_CH, S5_STATE), (2 * S5_STATE) ** -0.5)
    s5_d = nrm((L, GROUP_W), 1.0)
    s5_glu_w = nrm((L, GROUP_W, GROUP_W), GROUP_W ** -0.5)
    s5_glu_b = nrm((L, GROUP_W), 0.02)
    da_lq1 = nrm((L, DA_DQK), 0.1)
    da_lk1 = nrm((L, DA_DQK), 0.1)
    da_lq2 = nrm((L, DA_DQK), 0.1)
    da_lk2 = nrm((L, DA_DQK), 0.1)
    da_norm_w = 1.0 + nrm((L, GROUP_W), 0.02)
    ml_norm_w = 1.0 + nrm((L, GROUP_W), 0.02)
    w_out = nrm((L, MIX_W, D), MIX_W ** -0.5)
    router_w = nrm((L, D, N_EXPERTS), D ** -0.5)
    exp_w1 = nrm((L, N_EXPERTS, D, D_EXPERT), D ** -0.5)
    exp_w3 = nrm((L, N_EXPERTS, D, D_EXPERT), D ** -0.5)
    exp_w2 = nrm((L, N_EXPERTS, D_EXPERT, D), D_EXPERT ** -0.5)
    final_norm_w = 1.0 + nrm((D,), 0.02)
    return {'x': x, 'c': c, 'ctx': ctx, 'c_ctx': c_ctx, 'mod_w': mod_w, 'mod_b': mod_b,
            'norm1_w': norm1_w, 'norm2_w': norm2_w, 'w_in': w_in, 'b_in': b_in,
            'hg_lb_logits': hg_lb_logits, 'hg_norm_w': hg_norm_w,
            's5_lam_re': s5_lam_re, 's5_lam_im': s5_lam_im, 's5_log_dt': s5_log_dt,
            's5_b_re': s5_b_re, 's5_b_im': s5_b_im, 's5_c_re': s5_c_re, 's5_c_im': s5_c_im,
            's5_d': s5_d, 's5_glu_w': s5_glu_w, 's5_glu_b': s5_glu_b,
            'da_lq1': da_lq1, 'da_lk1': da_lk1, 'da_lq2': da_lq2, 'da_lk2': da_lk2, 'da_norm_w': da_norm_w,
            'ml_norm_w': ml_norm_w, 'w_out': w_out, 'router_w': router_w,
            'exp_w1': exp_w1, 'exp_w3': exp_w3, 'exp_w2': exp_w2, 'final_norm_w': final_norm_w}


def reference(x, c, ctx, c_ctx, mod_w, mod_b, norm1_w, norm2_w, w_in, b_in, hg_lb_logits, hg_norm_w,
              s5_lam_re, s5_lam_im, s5_log_dt, s5_b_re, s5_b_im, s5_c_re, s5_c_im, s5_d, s5_glu_w, s5_glu_b,
              da_lq1, da_lk1, da_lq2, da_lk2, da_norm_w, ml_norm_w, w_out, router_w,
              exp_w1, exp_w3, exp_w2, final_norm_w):
    lb_all = jnp.cumsum(jax.nn.softmax(hg_lb_logits.astype(F32), axis=0), axis=0)
    lb_all = lb_all - lb_all[0]
    hl, hc = x, ctx
    sc, scc = jax.nn.silu(c), jax.nn.silu(c_ctx)
    for li in range(DEPTH):
        ctx_out = li < DEPTH - 1
        mod_l = jnp.split((sc @ mod_w[li] + mod_b[li])[:, None, :], 6, axis=-1)
        mod_c = jnp.split(scc @ mod_w[li] + mod_b[li], 6, axis=-1)
        zl = modulated(hl, norm1_w[li], mod_l[0], mod_l[1]) @ w_in[li] + b_in[li]
        zc = modulated(hc, norm1_w[li], mod_c[0], mod_c[1]) @ w_in[li] + b_in[li]
        a_c, a_l = hgrn2_mixer(zc[..., HG_OFF:S5_OFF], zl[..., HG_OFF:S5_OFF], lb_all[li], hg_norm_w[li], ctx_out)
        b_c, b_l = s5_mixer(zc[..., S5_OFF:DA_OFF], zl[..., S5_OFF:DA_OFF], s5_lam_re[li], s5_lam_im[li],
                            s5_log_dt[li], s5_b_re[li], s5_b_im[li], s5_c_re[li], s5_c_im[li], s5_d[li],
                            s5_glu_w[li], s5_glu_b[li], ctx_out)
        c_c, c_l = diff_attention(zc[..., DA_OFF:ML_OFF], zl[..., DA_OFF:ML_OFF], da_lq1[li], da_lk1[li],
                                  da_lq2[li], da_lk2[li], da_norm_w[li], li, ctx_out)
        d_c, d_l = mlstm_mixer(zc[..., ML_OFF:IN_COLS], zl[..., ML_OFF:IN_COLS], ml_norm_w[li], ctx_out)
        hl = hl + mod_l[2] * (jnp.concatenate([a_l, b_l, c_l, d_l], axis=-1) @ w_out[li])
        hl = hl + mod_l[5] * expert_choice_moe(modulated(hl, norm2_w[li], mod_l[3], mod_l[4]),
                                               router_w[li], exp_w1[li], exp_w3[li], exp_w2[li])
        if ctx_out:
            hc = hc + mod_c[2] * (jnp.concatenate([a_c, b_c, c_c, d_c], axis=-1) @ w_out[li])
            hc = hc + mod_c[5] * expert_choice_moe(modulated(hc, norm2_w[li], mod_c[3], mod_c[4]),
                                                   router_w[li], exp_w1[li], exp_w3[li], exp_w2[li])
    return rms_norm(hl, final_norm_w)
```

```python
import math
import jax
import jax.numpy as jnp
from jax import lax
from jax.experimental import pallas as pl
from jax.experimental.pallas import tpu as pltpu

D_MODEL = 1024
BATCH = 16
SEQ = 2048
DEPTH = 2
CTX_LEN = 256
GRID_W = 64
N_MIXERS = 4
GROUP_W = D_MODEL // N_MIXERS
MIX_W = N_MIXERS * GROUP_W
EPS = 1e-6
F32 = jnp.float32
HG_HEADS = 4
HG_DK = GROUP_W // HG_HEADS
HG_DV = GROUP_W // HG_HEADS
HG_CHUNK = 16
S5_CH = 16
S5_GROUPS = GROUP_W // S5_CH
S5_STATE = 64
DA_HEADS = 4
DA_DQK = GROUP_W // (2 * DA_HEADS)
DA_DV = GROUP_W // DA_HEADS
Q_BLOCK = 128
ROPE_BASE = 10000.0
ML_HEADS = 4
ML_DK = GROUP_W // ML_HEADS
ML_DV = GROUP_W // ML_HEADS
ML_CHUNK = 64
N_EXPERTS = 16
EC_CAPACITY = 2
D_EXPERT = 2 * D_MODEL
HG_OFF = 0
S5_OFF = HG_OFF + 5 * GROUP_W
DA_OFF = S5_OFF + GROUP_W
ML_OFF = DA_OFF + 3 * GROUP_W
ML_GATE_OFF = ML_OFF + 4 * GROUP_W
IN_COLS = ML_GATE_OFF + 4 * ML_HEADS


def rms_norm(x, w):
    xf = x.astype(F32)
    y = xf * lax.rsqrt(jnp.mean(xf * xf, axis=-1, keepdims=True) + EPS)
    return (y * w.astype(F32)).astype(x.dtype)


def head_rms_norm(x, w, n_heads):
    shp = x.shape
    xh = x.reshape(shp[:-1] + (n_heads, -1))
    return rms_norm(xh, w.reshape(n_heads, -1)).reshape(shp)


def modulated(h, norm_w, shift, scale):
    return rms_norm(h, norm_w) * (1.0 + scale) + shift


def flip_seq(a, reverse):
    return a[:, ::-1] if reverse else a


def rope_2d(n, dim):
    axis_dim = dim // 2
    inv = ROPE_BASE ** (-jnp.arange(0, axis_dim, 2, dtype=F32) / axis_dim)
    t = jnp.arange(n, dtype=jnp.int32)
    row = (t // GRID_W).astype(F32)
    col = (t % GRID_W).astype(F32)
    ang = jnp.concatenate([row[:, None] * inv, col[:, None] * inv], axis=-1)
    return jnp.cos(ang), jnp.sin(ang)


def apply_rope(x, cos, sin):
    xr = x.astype(F32).reshape(x.shape[:-1] + (-1, 2))
    x1, x2 = xr[..., 0], xr[..., 1]
    shp = (1, cos.shape[0]) + (1,) * (x.ndim - 3) + (cos.shape[1],)
    c, s = cos.reshape(shp), sin.reshape(shp)
    y = jnp.stack([x1 * c - x2 * s, x1 * s + x2 * c], axis=-1)
    return y.reshape(x.shape).astype(x.dtype)


def gla_chunked(q, k, v, log_f, s0, with_out=True):
    bsz, t_len, nh, _ = q.shape
    nc = t_len // HG_CHUNK
    def chunks(a):
        return a.astype(F32).reshape(bsz, nc, HG_CHUNK, nh, a.shape[-1])
    q, k, v, log_f = chunks(q), chunks(k), chunks(v), chunks(log_f)
    b = jnp.cumsum(log_f, axis=2)
    b_end = b[:, :, -1]
    ds = jnp.einsum('bcshd,bcshv->bchdv', k * jnp.exp(b_end[:, :, None] - b), v)
    def step(s, inp):
        g_c, ds_c = inp
        return jnp.exp(g_c)[..., None] * s + ds_c, s
    s_fin, s_start = lax.scan(step, s0, (jnp.moveaxis(b_end, 1, 0), jnp.moveaxis(ds, 1, 0)))
    if not with_out:
        return None, s_fin
    s_start = jnp.moveaxis(s_start, 0, 1)
    tri = jnp.tril(jnp.ones((HG_CHUNK, HG_CHUNK), bool))[None, None, :, :, None, None]
    decay = jnp.exp(jnp.where(tri, b[:, :, :, None] - b[:, :, None], -jnp.inf))
    scores = jnp.sum(q[:, :, :, None] * decay * k[:, :, None], axis=-1)
    o = (jnp.einsum('bctsh,bcshv->bcthv', scores, v)
         + jnp.einsum('bcthd,bchdv->bcthv', q * jnp.exp(b), s_start))
    return o.reshape(bsz, t_len, nh, -1), s_fin


def hgrn2_mixer(zc, zl, lb, norm_w, with_ctx_out):
    def heads(a):
        return a.reshape(a.shape[:2] + (HG_HEADS, -1))
    def split(z):
        q, i, ff, fb, g = jnp.split(z, 5, axis=-1)
        return heads(q) * HG_DK ** -0.5, heads(i), (heads(ff), heads(fb)), g
    qc, ic, fc, gc = split(zc)
    ql, il, fl, gl = split(zl)
    oc, ol = 0.0, 0.0
    for d in range(2):
        lbd = lb[d].reshape(HG_HEADS, HG_DK)
        f_c = lbd + (1.0 - lbd) * jax.nn.sigmoid(fc[d].astype(F32))
        f_l = lbd + (1.0 - lbd) * jax.nn.sigmoid(fl[d].astype(F32))
        s0 = jnp.zeros((zc.shape[0], HG_HEADS, HG_DK, HG_DV), F32)
        o_c, s_c = gla_chunked(flip_seq(qc, d), flip_seq(1.0 - f_c, d), flip_seq(ic, d),
                               flip_seq(jnp.log(f_c), d), s0, with_ctx_out)
        o_l, _ = gla_chunked(flip_seq(ql, d), flip_seq(1.0 - f_l, d), flip_seq(il, d),
                             flip_seq(jnp.log(f_l), d), s_c)
        ol = ol + flip_seq(o_l, d)
        if with_ctx_out:
            oc = oc + flip_seq(o_c, d)
    def readout(o, g):
        o = o.reshape(o.shape[:2] + (-1,)).astype(g.dtype)
        return head_rms_norm(o, norm_w, HG_HEADS) * jax.nn.silu(g)
    return (readout(oc, gc) if with_ctx_out else None), readout(ol, gl)


def cmul(ar, ai, br, bi):
    return ar * br - ai * bi, ar * bi + ai * br


def s5_combine(e1, e2):
    a1r, a1i, x1r, x1i = e1
    a2r, a2i, x2r, x2i = e2
    ar, ai = cmul(a2r, a2i, a1r, a1i)
    yr, yi = cmul(a2r, a2i, x1r, x1i)
    return ar, ai, yr + x2r, yi + x2i


def s5_mixer(uc, ul, lam_re, lam_im, log_dt, b_re, b_im, c_re, c_im, d_skip, glu_w, glu_b, with_ctx_out):
    def groups(u):
        return u.astype(F32).reshape(u.shape[:2] + (S5_GROUPS, S5_CH))
    gc, gl = groups(uc), groups(ul)
    b_re, b_im, c_re, c_im = (a.astype(F32) for a in (b_re, b_im, c_re, c_im))
    def b_proj(u):
        return jnp.einsum('bngc,gpc->bngp', u, b_re), jnp.einsum('bngc,gpc->bngp', u, b_im)
    def c_read(xr, xi):
        return jnp.einsum('bngp,gcp->bngc', xr, c_re) - jnp.einsum('bngp,gcp->bngc', xi, c_im)
    buc, bul = b_proj(gc), b_proj(gl)
    yc, yl = 0.0, 0.0
    for d in range(2):
        lr, li = lam_re[d].astype(F32), lam_im[d].astype(F32)
        dt = jnp.exp(log_dt[d].astype(F32))[:, None]
        mag = jnp.exp(lr * dt)
        ab_re, ab_im = mag * jnp.cos(li * dt), mag * jnp.sin(li * dt)
        den = lr * lr + li * li
        co_re = ((ab_re - 1.0) * lr + ab_im * li) / den
        co_im = (ab_im * lr - (ab_re - 1.0) * li) / den
        def run(bu, x0):
            br, bi = cmul(co_re, co_im, flip_seq(bu[0], d), flip_seq(bu[1], d))
            a_shape = (1, br.shape[1]) + ab_re.shape
            ar, ai, xr, xi = lax.associative_scan(
                s5_combine, (jnp.broadcast_to(ab_re, a_shape), jnp.broadcast_to(ab_im, a_shape), br, bi), axis=1)
            if x0 is not None:
                pr, pm = cmul(ar, ai, x0[0][:, None], x0[1][:, None])
                xr, xi = xr + pr, xi + pm
            return xr, xi
        xcr, xci = run(buc, None)
        xlr, xli = run(bul, (xcr[:, -1], xci[:, -1]))
        yl = yl + flip_seq(c_read(xlr, xli), d)
        if with_ctx_out:
            yc = yc + flip_seq(c_read(xcr, xci), d)
    dsk = d_skip.astype(F32).reshape(S5_GROUPS, S5_CH)
    def glu(y, g, like):
        y = jax.nn.gelu((y + dsk * g).reshape(g.shape[:2] + (GROUP_W,)))
        return (y * jax.nn.sigmoid(y @ glu_w.astype(F32) + glu_b.astype(F32))).astype(like.dtype)
    return (glu(yc, gc, uc) if with_ctx_out else None), glu(yl, gl, ul)


def diff_softmax_blocks(q, k, v, lam):
    bsz, nq = q.shape[:2]
    nb = nq // Q_BLOCK
    scale = DA_DQK ** -0.5
    def block(qb):
        s = jnp.einsum('bqhmd,bkhmd->bhmqk', qb, k).astype(F32) * scale
        p = jax.nn.softmax(s, axis=-1)
        a = p[:, :, 0] - lam * p[:, :, 1]
        return jnp.einsum('bhqk,bkhv->bqhv', a.astype(v.dtype), v)
    qb = jnp.moveaxis(q.reshape((bsz, nb, Q_BLOCK) + q.shape[2:]), 1, 0)
    o = lax.map(block, qb)
    return jnp.moveaxis(o, 0, 1).reshape((bsz, nq) + o.shape[3:])


def diff_attention(zc, zl, lam_q1, lam_k1, lam_q2, lam_k2, norm_w, layer_idx, with_ctx_out):
    def split(z):
        q, k, v = jnp.split(z, 3, axis=-1)
        shp = z.shape[:2]
        return (q.reshape(shp + (DA_HEADS, 2, DA_DQK)), k.reshape(shp + (DA_HEADS, 2, DA_DQK)),
                v.reshape(shp + (DA_HEADS, DA_DV)))
    qc, kc, vc = split(zc)
    ql, kl, vl = split(zl)
    cos, sin = rope_2d(zl.shape[1], DA_DQK)
    ql, kl = apply_rope(ql, cos, sin), apply_rope(kl, cos, sin)
    lam_init = 0.8 - 0.6 * math.exp(-0.3 * layer_idx)
    lam = (jnp.exp(jnp.sum(lam_q1.astype(F32) * lam_k1.astype(F32)))
           - jnp.exp(jnp.sum(lam_q2.astype(F32) * lam_k2.astype(F32))) + lam_init)
    def readout(o):
        return head_rms_norm(o.reshape(o.shape[:2] + (-1,)), norm_w, DA_HEADS) * (1.0 - lam_init)
    out_l = readout(diff_softmax_blocks(ql, jnp.concatenate([kc, kl], axis=1),
                                        jnp.concatenate([vc, vl], axis=1), lam))
    out_c = readout(diff_softmax_blocks(qc, kc, vc, lam)) if with_ctx_out else None
    return out_c, out_l


def mlstm_chunked(q, k, v, log_i, log_f, state, with_out=True):
    bsz, t_len, nh, _ = q.shape
    nc = t_len // ML_CHUNK
    def chunks(a):
        return a.astype(F32).reshape((bsz, nc, ML_CHUNK) + a.shape[2:])
    q, k, v, log_i, log_f = chunks(q), chunks(k), chunks(v), chunks(log_i), chunks(log_f)
    b = jnp.cumsum(log_f, axis=2)
    b_end = b[:, :, -1]
    w_end = b_end[:, :, None] - b + log_i
    m_loc = jnp.max(w_end, axis=2)
    e_end = jnp.exp(w_end - m_loc[:, :, None])
    c_loc = jnp.einsum('bcshv,bcshd->bchvd', e_end[..., None] * v, k)
    n_loc = jnp.einsum('bcsh,bcshd->bchd', e_end, k)
    def step(carry, inp):
        c_st, n_st, m_st = carry
        be, ml, cl, nl = inp
        m_new = jnp.maximum(be + m_st, ml)
        a_old, a_loc = jnp.exp(be + m_st - m_new), jnp.exp(ml - m_new)
        c_new = a_old[..., None, None] * c_st + a_loc[..., None, None] * cl
        n_new = a_old[..., None] * n_st + a_loc[..., None] * nl
        return (c_new, n_new, m_new), carry
    mv = lambda a: jnp.moveaxis(a, 1, 0)
    final, starts = lax.scan(step, state, (mv(b_end), mv(m_loc), mv(c_loc), mv(n_loc)))
    if not with_out:
        return None, final
    c_s, n_s, m_s = (jnp.moveaxis(a, 0, 1) for a in starts)
    tri = jnp.tril(jnp.ones((ML_CHUNK, ML_CHUNK), bool))[None, None, :, :, None]
    d_log = jnp.where(tri, b[:, :, :, None] - b[:, :, None] + log_i[:, :, None], -jnp.inf)
    w_inter = b + m_s[:, :, None]
    m_t = jnp.maximum(w_inter, jnp.max(d_log, axis=3))
    e_inter = jnp.exp(w_inter - m_t)
    wts = jnp.exp(d_log - m_t[:, :, :, None]) * jnp.einsum('bcthd,bcshd->bctsh', q, k)
    num = (jnp.einsum('bctsh,bcshv->bcthv', wts, v)
           + e_inter[..., None] * jnp.einsum('bcthd,bchvd->bcthv', q, c_s))
    den = jnp.sum(wts, axis=3) + e_inter * jnp.einsum('bcthd,bchd->bcth', q, n_s)
    h = num / jnp.maximum(jnp.abs(den), jnp.exp(-m_t))[..., None]
    return h.reshape(bsz, t_len, nh, -1), final


def mlstm_mixer(zc, zl, norm_w, with_ctx_out):
    def split(z):
        shp = z.shape[:2]
        q, k, v, o = (z[..., j * GROUP_W:(j + 1) * GROUP_W] for j in range(4))
        gates = z[..., 4 * GROUP_W:].astype(F32).reshape(shp + (4, ML_HEADS))
        hd = lambda a: a.reshape(shp + (ML_HEADS, -1))
        return hd(q), hd(k) * ML_DK ** -0.5, hd(v), o, gates
    qc, kc, vc, oc, gtc = split(zc)
    ql, kl, vl, ol, gtl = split(zl)
    bsz = zc.shape[0]
    hc, hl = 0.0, 0.0
    for d in range(2):
        state0 = (jnp.zeros((bsz, ML_HEADS, ML_DV, ML_DK), F32), jnp.zeros((bsz, ML_HEADS, ML_DK), F32),
                  jnp.zeros((bsz, ML_HEADS), F32))
        h_c, st_c = mlstm_chunked(flip_seq(qc, d), flip_seq(kc, d), flip_seq(vc, d), flip_seq(gtc[:, :, d], d),
                                  flip_seq(jax.nn.log_sigmoid(gtc[:, :, 2 + d]), d), state0, with_ctx_out)
        h_l, _ = mlstm_chunked(flip_seq(ql, d), flip_seq(kl, d), flip_seq(vl, d), flip_seq(gtl[:, :, d], d),
                               flip_seq(jax.nn.log_sigmoid(gtl[:, :, 2 + d]), d), st_c)
        hl = hl + flip_seq(h_l, d)
        if with_ctx_out:
            hc = hc + flip_seq(h_c, d)
    def readout(h, o):
        h = h.reshape(h.shape[:2] + (-1,)).astype(o.dtype)
        return head_rms_norm(h, norm_w, ML_HEADS) * jax.nn.sigmoid(o)
    return (readout(hc, oc) if with_ctx_out else None), readout(hl, ol)


def expert_choice_moe(x, router_w, w1, w3, w2):
    bsz, n, _ = x.shape
    cap = EC_CAPACITY * n // N_EXPERTS
    aff = jax.nn.softmax((x @ router_w).astype(F32), axis=-1)
    gate, idx = lax.top_k(jnp.swapaxes(aff, 1, 2), cap)
    bidx = jnp.arange(bsz)[:, None, None]
    xs = x[bidx, idx]
    def expert(args):
        xe, w1e, w3e, w2e = args
        return (jax.nn.silu(xe @ w1e) * (xe @ w3e)) @ w2e
    ye = lax.map(expert, (jnp.swapaxes(xs, 0, 1), w1, w3, w2))
    ye = jnp.swapaxes(ye, 0, 1) * gate[..., None].astype(x.dtype)
    return jnp.zeros_like(x).at[bidx, idx].add(ye)


def _final_norm_body(x_ref, w_ref, o_ref):
    x = x_ref[...]
    y = x * lax.rsqrt(jnp.mean(x * x, axis=-1, keepdims=True) + EPS)
    o_ref[...] = y * w_ref[...]


def _final_norm(h, w):
    bsz, n, d = h.shape
    h2 = h.reshape(bsz * n, d)
    tm = 1024
    out = pl.pallas_call(
        _final_norm_body,
        grid=(bsz * n // tm,),
        in_specs=[pl.BlockSpec((tm, d), lambda i: (i, 0)), pl.BlockSpec((1, d), lambda i: (0, 0))],
        out_specs=pl.BlockSpec((tm, d), lambda i: (i, 0)),
        out_shape=jax.ShapeDtypeStruct((bsz * n, d), F32),
    )(h2, w.reshape(1, d))
    return out.reshape(bsz, n, d)


def kernel(x, c, ctx, c_ctx, mod_w, mod_b, norm1_w, norm2_w, w_in, b_in, hg_lb_logits, hg_norm_w,
           s5_lam_re, s5_lam_im, s5_log_dt, s5_b_re, s5_b_im, s5_c_re, s5_c_im, s5_d, s5_glu_w, s5_glu_b,
           da_lq1, da_lk1, da_lq2, da_lk2, da_norm_w, ml_norm_w, w_out, router_w,
           exp_w1, exp_w3, exp_w2, final_norm_w):
    lb_all = jnp.cumsum(jax.nn.softmax(hg_lb_logits.astype(F32), axis=0), axis=0)
    lb_all = lb_all - lb_all[0]
    hl, hc = x, ctx
    sc, scc = jax.nn.silu(c), jax.nn.silu(c_ctx)
    for li in range(DEPTH):
        ctx_out = li < DEPTH - 1
        mod_l = jnp.split((sc @ mod_w[li] + mod_b[li])[:, None, :], 6, axis=-1)
        mod_c = jnp.split(scc @ mod_w[li] + mod_b[li], 6, axis=-1)
        zl = modulated(hl, norm1_w[li], mod_l[0], mod_l[1]) @ w_in[li] + b_in[li]
        zc = modulated(hc, norm1_w[li], mod_c[0], mod_c[1]) @ w_in[li] + b_in[li]
        a_c, a_l = hgrn2_mixer(zc[..., HG_OFF:S5_OFF], zl[..., HG_OFF:S5_OFF], lb_all[li], hg_norm_w[li], ctx_out)
        b_c, b_l = s5_mixer(zc[..., S5_OFF:DA_OFF], zl[..., S5_OFF:DA_OFF], s5_lam_re[li], s5_lam_im[li],
                            s5_log_dt[li], s5_b_re[li], s5_b_im[li], s5_c_re[li], s5_c_im[li], s5_d[li],
                            s5_glu_w[li], s5_glu_b[li], ctx_out)
        c_c, c_l = diff_attention(zc[..., DA_OFF:ML_OFF], zl[..., DA_OFF:ML_OFF], da_lq1[li], da_lk1[li],
                                  da_lq2[li], da_lk2[li], da_norm_w[li], li, ctx_out)
        d_c, d_l = mlstm_mixer(zc[..., ML_OFF:IN_COLS], zl[..., ML_OFF:IN_COLS], ml_norm_w[li], ctx_out)
        hl = hl + mod_l[2] * (jnp.concatenate([a_l, b_l, c_l, d_l], axis=-1) @ w_out[li])
        hl = hl + mod_l[5] * expert_choice_moe(modulated(hl, norm2_w[li], mod_l[3], mod_l[4]),
                                               router_w[li], exp_w1[li], exp_w3[li], exp_w2[li])
        if ctx_out:
            hc = hc + mod_c[2] * (jnp.concatenate([a_c, b_c, c_c, d_c], axis=-1) @ w_out[li])
            hc = hc + mod_c[5] * expert_choice_moe(modulated(hc, norm2_w[li], mod_c[3], mod_c[4]),
                                                   router_w[li], exp_w1[li], exp_w3[li], exp_w2[li])
    return _final_norm(hl, final_norm_w)
```

```python
import functools
import math
import jax
import jax.numpy as jnp
from jax import lax
from jax.experimental import pallas as pl
from jax.experimental.pallas import tpu as pltpu

D_MODEL = 1024
DEPTH = 2
CTX_LEN = 256
GRID_W = 64
N_MIXERS = 4
GROUP_W = D_MODEL // N_MIXERS
MIX_W = N_MIXERS * GROUP_W
EPS = 1e-6
F32 = jnp.float32
BF16 = jnp.bfloat16
HG_HEADS = 4
HG_DK = GROUP_W // HG_HEADS
HG_DV = GROUP_W // HG_HEADS
HG_CHUNK = 16
S5_CH = 16
S5_GROUPS = GROUP_W // S5_CH
S5_STATE = 64
DA_HEADS = 4
DA_DQK = GROUP_W // (2 * DA_HEADS)
DA_DV = GROUP_W // DA_HEADS
ROPE_BASE = 10000.0
ML_HEADS = 4
ML_DK = GROUP_W // ML_HEADS
ML_DV = GROUP_W // ML_HEADS
ML_CHUNK = 64
N_EXPERTS = 16
EC_CAPACITY = 2
D_EXPERT = 2 * D_MODEL
HG_OFF = 0
S5_OFF = HG_OFF + 5 * GROUP_W
DA_OFF = S5_OFF + GROUP_W
ML_OFF = DA_OFF + 3 * GROUP_W
ML_GATE_OFF = ML_OFF + 4 * GROUP_W
IN_COLS = ML_GATE_OFF + 4 * ML_HEADS

LANES = 128
ROW_TILE = 256
IN_COLS_PAD = ML_GATE_OFF + LANES
N_GATES = 4 * ML_HEADS
S5_LANES = S5_GROUPS * S5_STATE
S5_HALF = S5_LANES // 2
VMEM_LIMIT = 56 * 1024 * 1024


def _cparams(*sem):
    return pltpu.CompilerParams(dimension_semantics=sem, vmem_limit_bytes=VMEM_LIMIT)


def _rms(x):
    return x * lax.rsqrt(jnp.mean(x * x, axis=-1, keepdims=True) + EPS)


def _mod_body(cv_ref, w_ref, b_ref, o_ref):
    cv = cv_ref[...]
    s = cv * jax.nn.sigmoid(cv)
    o_ref[0] = jnp.dot(s.astype(BF16), w_ref[0].astype(BF16), preferred_element_type=F32) + b_ref[0]


def _modulation(cv, mod_w, mod_b):
    n_l, d, n6 = mod_w.shape
    r = cv.shape[0]
    tn = n6 // 4
    return pl.pallas_call(
        _mod_body,
        grid=(n_l, 4),
        in_specs=[pl.BlockSpec((r, d), lambda l, j: (0, 0)),
                  pl.BlockSpec((1, d, tn), lambda l, j: (l, 0, j)),
                  pl.BlockSpec((1, 1, tn), lambda l, j: (l, 0, j))],
        out_specs=pl.BlockSpec((1, r, tn), lambda l, j: (l, 0, j)),
        out_shape=jax.ShapeDtypeStruct((n_l, r, n6), F32),
        compiler_params=_cparams("arbitrary", "arbitrary"),
    )(cv, mod_w, mod_b.reshape(n_l, 1, n6))


def _inproj_body(h_ref, m_ref, nw_ref, w_ref, b_ref, wg_ref, bg_ref, z_ref, gt_ref):
    m = m_ref[0, 0]
    xn = _rms(h_ref[0]) * nw_ref[...] * (1.0 + m[1:2]) + m[0:1]
    xb = xn.astype(BF16)
    z_ref[0] = jnp.dot(xb, w_ref[...], preferred_element_type=F32) + b_ref[...]
    gt_ref[0] = lax.dot_general(wg_ref[...], xb, (((1,), (1,)), ((), ())),
                                preferred_element_type=F32) + bg_ref[...]


def _in_projection(h, modsel, norm_w, w_pad, b_pad, wg_t, bg_col):
    bsz, t, d = h.shape
    nt = t // ROW_TILE
    return pl.pallas_call(
        _inproj_body,
        grid=(bsz, nt),
        in_specs=[pl.BlockSpec((1, ROW_TILE, d), lambda b, j: (b, j, 0)),
                  pl.BlockSpec((1, 1, 6, d), lambda b, j: (b, jnp.minimum(j, 1), 0, 0)),
                  pl.BlockSpec((1, d), lambda b, j: (0, 0)),
                  pl.BlockSpec((d, IN_COLS_PAD), lambda b, j: (0, 0)),
                  pl.BlockSpec((1, IN_COLS_PAD), lambda b, j: (0, 0)),
                  pl.BlockSpec((N_GATES, d), lambda b, j: (0, 0)),
                  pl.BlockSpec((N_GATES, 1), lambda b, j: (0, 0))],
        out_specs=[pl.BlockSpec((1, ROW_TILE, IN_COLS_PAD), lambda b, j: (b, j, 0)),
                   pl.BlockSpec((1, N_GATES, ROW_TILE), lambda b, j: (b, 0, j))],
        out_shape=[jax.ShapeDtypeStruct((bsz, t, IN_COLS_PAD), F32),
                   jax.ShapeDtypeStruct((bsz, N_GATES, t), F32)],
        compiler_params=_cparams("arbitrary", "arbitrary"),
    )(h, modsel, norm_w.reshape(1, d), w_pad, b_pad, wg_t, bg_col)


def _outproj_body(a_ref, b_ref, c_ref, d_ref, h_ref, m_ref, nw_ref, wo_ref, rw_ref, hm_ref, xm_ref, aff_ref):
    m = m_ref[0, 0]
    y = jnp.concatenate([a_ref[0], b_ref[0], c_ref[0], d_ref[0]], axis=-1).astype(BF16)
    hm = h_ref[0] + m[2:3] * jnp.dot(y, wo_ref[...], preferred_element_type=F32)
    hm_ref[0] = hm
    xm = (_rms(hm) * nw_ref[...] * (1.0 + m[4:5]) + m[3:4]).astype(BF16)
    xm_ref[0] = xm
    logit = lax.dot_general(rw_ref[...], xm, (((1,), (1,)), ((), ())), preferred_element_type=F32)
    e = jnp.exp(logit - jnp.max(logit, axis=0, keepdims=True))
    aff_ref[0] = e / jnp.sum(e, axis=0, keepdims=True)


def _out_projection(mix, h, modsel, norm_w, wo_bf, rw_t):
    bsz, t, d = h.shape
    nt = t // ROW_TILE
    mix_spec = pl.BlockSpec((1, ROW_TILE, GROUP_W), lambda b, j: (b, j, 0))
    row_spec = pl.BlockSpec((1, ROW_TILE, d), lambda b, j: (b, j, 0))
    return pl.pallas_call(
        _outproj_body,
        grid=(bsz, nt),
        in_specs=[mix_spec, mix_spec, mix_spec, mix_spec, row_spec,
                  pl.BlockSpec((1, 1, 6, d), lambda b, j: (b, jnp.minimum(j, 1), 0, 0)),
                  pl.BlockSpec((1, d), lambda b, j: (0, 0)),
                  pl.BlockSpec((MIX_W, d), lambda b, j: (0, 0)),
                  pl.BlockSpec((N_EXPERTS, d), lambda b, j: (0, 0))],
        out_specs=[row_spec, row_spec, pl.BlockSpec((1, N_EXPERTS, ROW_TILE), lambda b, j: (b, 0, j))],
        out_shape=[jax.ShapeDtypeStruct((bsz, t, d), F32), jax.ShapeDtypeStruct((bsz, t, d), BF16),
                   jax.ShapeDtypeStruct((bsz, N_EXPERTS, t), F32)],
        compiler_params=_cparams("arbitrary", "arbitrary"),
    )(*mix, h, modsel, norm_w.reshape(1, d), wo_bf, rw_t)


def _topk_body(aff_ref, slot_ref, slot_t_ref, *, cap, n, row0):
    aff = aff_ref[0, :, row0:row0 + n]
    bits = pltpu.bitcast(aff, jnp.int32)
    thr = jnp.zeros((N_EXPERTS, 1), jnp.int32)
    for bit in range(30, -1, -1):
        cand = thr | (1 << bit)
        cnt = jnp.sum(jnp.where(bits >= cand, 1.0, 0.0), axis=-1, keepdims=True)
        thr = jnp.where(cnt >= cap, cand, thr)
    room = cap - jnp.sum(jnp.where(bits > thr, 1.0, 0.0), axis=-1, keepdims=True)
    ri = lax.broadcasted_iota(jnp.int32, (LANES, LANES), 0)
    ci = lax.broadcasted_iota(jnp.int32, (LANES, LANES), 1)
    incl = jnp.where(ri <= ci, 1.0, 0.0).astype(BF16)
    off_eq = jnp.zeros((N_EXPERTS, 1), F32)
    off_sel = jnp.zeros((N_EXPERTS, 1), F32)
    pieces = []
    for j in range(n // LANES):
        sl = slice(j * LANES, (j + 1) * LANES)
        bits_b = bits[:, sl]
        eq_b = jnp.where(bits_b == thr, 1.0, 0.0)
        pos_eq = jnp.dot(eq_b.astype(BF16), incl, preferred_element_type=F32) - eq_b + off_eq
        sel = jnp.where(bits_b > thr, 1.0, jnp.where(pos_eq < room, eq_b, 0.0))
        pos_sel = jnp.dot(sel.astype(BF16), incl, preferred_element_type=F32) - sel + off_sel
        piece = jnp.where(sel > 0.5, pos_sel, -1.0)
        slot_ref[0, :, sl] = piece.astype(jnp.int32)
        pieces.append(piece)
        off_eq = off_eq + jnp.sum(eq_b, axis=-1, keepdims=True)
        off_sel = off_sel + jnp.sum(sel, axis=-1, keepdims=True)
    pad = jnp.full((LANES - N_EXPERTS, LANES), -1.0, F32)
    for j in range(n // LANES):
        tile = jnp.concatenate([pieces[j], pad], axis=0)
        slot_t_ref[0, j * LANES:(j + 1) * LANES, :] = tile.T.astype(jnp.int32)


def _route(aff_t, row0, n):
    bsz, _, t = aff_t.shape
    cap = EC_CAPACITY * n // N_EXPERTS
    return pl.pallas_call(
        functools.partial(_topk_body, cap=cap, n=n, row0=row0),
        grid=(bsz,),
        in_specs=[pl.BlockSpec((1, N_EXPERTS, t), lambda b: (b, 0, 0))],
        out_specs=[pl.BlockSpec((1, N_EXPERTS, n), lambda b: (b, 0, 0)),
                   pl.BlockSpec((1, n, LANES), lambda b: (b, 0, 0))],
        out_shape=[jax.ShapeDtypeStruct((bsz, N_EXPERTS, n), jnp.int32),
                   jax.ShapeDtypeStruct((bsz, n, LANES), jnp.int32)],
        compiler_params=_cparams("arbitrary"),
    )(aff_t)


def _moe_ffn_body(x_ref, slot_ref, aff_ref, w1_ref, w3_ref, w2_ref, yh_ref, yl_ref, xe_scr, g_scr, *, nb, cap, n, off):
    for i in range(nb):
        hit = lax.broadcasted_iota(jnp.int32, (cap, n), 0) == slot_ref[i, 0]
        onehot = jnp.where(hit, 1.0, 0.0).astype(BF16)
        xe_scr[i * cap:(i + 1) * cap, :] = jnp.dot(onehot, x_ref[i, off:off + n, :],
                                                   preferred_element_type=F32).astype(BF16)
        g_scr[i * cap:(i + 1) * cap, :] = jnp.sum(jnp.where(hit, aff_ref[i, 0, :, off:off + n], 0.0),
                                                  axis=-1, keepdims=True)
    xe = xe_scr[...]
    h1 = jnp.dot(xe, w1_ref[0], preferred_element_type=F32)
    h3 = jnp.dot(xe, w3_ref[0], preferred_element_type=F32)
    act = (h1 * jax.nn.sigmoid(h1) * h3).astype(BF16)
    y = jnp.dot(act, w2_ref[0], preferred_element_type=F32) * g_scr[...]
    y_hi = y.astype(BF16)
    y_lo = (y - y_hi.astype(F32)).astype(BF16)
    for i in range(nb):
        yh_ref[i, 0] = y_hi[i * cap:(i + 1) * cap]
        yl_ref[i, 0] = y_lo[i * cap:(i + 1) * cap]


def _moe_ffn(xm, slot, aff_t, row0, n, nb, w1, w3, w2):
    bsz, t, d = xm.shape
    cap = EC_CAPACITY * n // N_EXPERTS
    rows, off = (n, 0) if row0 % n == 0 else (t, row0)
    blk = row0 // n if off == 0 else 0
    slot4 = slot.reshape(bsz, N_EXPERTS, 1, n)
    aff4 = aff_t.reshape(bsz, N_EXPERTS, 1, t)
    y_spec = pl.BlockSpec((nb, 1, cap, d), lambda e, b: (b, e, 0, 0))
    y_shape = jax.ShapeDtypeStruct((bsz, N_EXPERTS, cap, d), BF16)
    return pl.pallas_call(
        functools.partial(_moe_ffn_body, nb=nb, cap=cap, n=n, off=off),
        grid=(N_EXPERTS, bsz // nb),
        in_specs=[pl.BlockSpec((nb, rows, d), lambda e, b: (b, blk, 0)),
                  pl.BlockSpec((nb, 1, 1, n), lambda e, b: (b, e, 0, 0)),
                  pl.BlockSpec((nb, 1, 1, rows), lambda e, b: (b, e, 0, blk)),
                  pl.BlockSpec((1, d, D_EXPERT), lambda e, b: (e, 0, 0)),
                  pl.BlockSpec((1, d, D_EXPERT), lambda e, b: (e, 0, 0)),
                  pl.BlockSpec((1, D_EXPERT, d), lambda e, b: (e, 0, 0))],
        out_specs=[y_spec, y_spec],
        out_shape=[y_shape, y_shape],
        scratch_shapes=[pltpu.VMEM((nb * cap, d), BF16), pltpu.VMEM((nb * cap, 1), F32)],
        compiler_params=_cparams("arbitrary", "arbitrary"),
    )(xm, slot4, aff4, w1, w3, w2)


def _moe_combine_body(yh_ref, yl_ref, st_ref, hm_ref, m_ref, o_ref, *, cap):
    st = st_ref[0]
    lane = lax.broadcasted_iota(jnp.int32, (st.shape[0], cap), 1)
    acc = jnp.zeros(o_ref.shape[1:], F32)
    for e in range(N_EXPERTS):
        onehot = jnp.where(lane == st[:, e:e + 1], 1.0, 0.0).astype(BF16)
        acc = acc + jnp.dot(onehot, yh_ref[0, e], preferred_element_type=F32)
        acc = acc + jnp.dot(onehot, yl_ref[0, e], preferred_element_type=F32)
    o_ref[0] = hm_ref[0] + m_ref[0, 0][5:6] * acc


def _moe_combine(y_hi, y_lo, slot_t, hm, modsel, row0, n, is_latent):
    bsz, _, cap, d = y_hi.shape
    nt = n // ROW_TILE
    blk0 = row0 // ROW_TILE
    y_spec = pl.BlockSpec((1, N_EXPERTS, cap, d), lambda b, j: (b, 0, 0, 0))
    return pl.pallas_call(
        functools.partial(_moe_combine_body, cap=cap),
        grid=(bsz, nt),
        in_specs=[y_spec, y_spec,
                  pl.BlockSpec((1, ROW_TILE, LANES), lambda b, j: (b, j, 0)),
                  pl.BlockSpec((1, ROW_TILE, d), lambda b, j: (b, blk0 + j, 0)),
                  pl.BlockSpec((1, 1, 6, d), lambda b, j: (b, is_latent, 0, 0))],
        out_specs=pl.BlockSpec((1, ROW_TILE, d), lambda b, j: (b, j, 0)),
        out_shape=jax.ShapeDtypeStruct((bsz, n, d), F32),
        compiler_params=_cparams("arbitrary", "arbitrary"),
    )(y_hi, y_lo, slot_t, hm, modsel)


def _swap_pairs(x):
    w = x.shape[-1]
    lane = lax.broadcasted_iota(jnp.int32, x.shape, x.ndim - 1)
    return jnp.where(lane % 2 == 0, pltpu.roll(x, w - 1, x.ndim - 1), pltpu.roll(x, 1, x.ndim - 1))


def _attn_body(q_ref, k_ref, v_ref, cq_ref, sq_ref, ck_ref, sk_ref, lq1_ref, lk1_ref, lq2_ref, lk2_ref, nw_ref,
               o_ref, k_scr, v_scr, *, rope, lam_init):
    @pl.when(pl.program_id(1) == 0)
    def _():
        k = k_ref[0]
        if rope:
            k = k * ck_ref[...] + _swap_pairs(k) * sk_ref[...]
        k_scr[...] = k.astype(BF16)
        v_scr[...] = v_ref[0].astype(BF16)

    q = q_ref[0]
    if rope:
        q = q * cq_ref[...] + _swap_pairs(q) * sq_ref[...]
    lam = (jnp.exp(jnp.sum(lq1_ref[...] * lk1_ref[...], axis=-1, keepdims=True))
           - jnp.exp(jnp.sum(lq2_ref[...] * lk2_ref[...], axis=-1, keepdims=True)) + lam_init)
    scale = DA_DQK ** -0.5
    lane = lax.broadcasted_iota(jnp.int32, (1, GROUP_W), 1)
    kk = k_scr[...]
    vv = v_scr[...]
    o = jnp.zeros(q.shape, F32)
    inv = jnp.zeros(q.shape, F32)
    for h in range(DA_HEADS):
        probs = []
        for mi in range(2):
            qm = jnp.where(lane // DA_DQK == 2 * h + mi, q, 0.0).astype(BF16)
            s = lax.dot_general(qm, kk, (((1,), (1,)), ((), ())), preferred_element_type=F32) * scale
            e = jnp.exp(s - jnp.max(s, axis=-1, keepdims=True))
            probs.append(e / jnp.sum(e, axis=-1, keepdims=True))
        a = (probs[0] - lam * probs[1]).astype(BF16)
        head = lane // DA_DV == h
        oh = jnp.where(head, jnp.dot(a, vv, preferred_element_type=F32), 0.0)
        ms = jnp.sum(oh * oh, axis=-1, keepdims=True) * (1.0 / DA_DV)
        o = o + oh
        inv = jnp.where(head, lax.rsqrt(ms + EPS), inv)
    o_ref[0] = o * inv * nw_ref[...] * (1.0 - lam_init)


def _diff_attention(z, q_row0, nq, nk, rope, cos_t, sin_t, lq1, lk1, lq2, lk2, norm_w, lam_init):
    bsz = z.shape[0]
    qb = ROW_TILE
    q0 = q_row0 // qb
    col = DA_OFF // GROUP_W
    vec = lambda a: a.reshape(1, -1)
    small = pl.BlockSpec((1, DA_DQK), lambda b, j: (0, 0))
    return pl.pallas_call(
        functools.partial(_attn_body, rope=rope, lam_init=lam_init),
        grid=(bsz, nq // qb),
        in_specs=[pl.BlockSpec((1, qb, GROUP_W), lambda b, j: (b, q0 + j, col)),
                  pl.BlockSpec((1, nk, GROUP_W), lambda b, j: (b, 0, col + 1)),
                  pl.BlockSpec((1, nk, GROUP_W), lambda b, j: (b, 0, col + 2)),
                  pl.BlockSpec((qb, GROUP_W), lambda b, j: (q0 + j, 0)),
                  pl.BlockSpec((qb, GROUP_W), lambda b, j: (q0 + j, 0)),
                  pl.BlockSpec((nk, GROUP_W), lambda b, j: (0, 0)),
                  pl.BlockSpec((nk, GROUP_W), lambda b, j: (0, 0)),
                  small, small, small, small,
                  pl.BlockSpec((1, GROUP_W), lambda b, j: (0, 0))],
        out_specs=pl.BlockSpec((1, qb, GROUP_W), lambda b, j: (b, j, 0)),
        out_shape=jax.ShapeDtypeStruct((bsz, nq, GROUP_W), F32),
        scratch_shapes=[pltpu.VMEM((nk, GROUP_W), BF16), pltpu.VMEM((nk, GROUP_W), BF16)],
        compiler_params=_cparams("arbitrary", "arbitrary"),
    )(z, z, z, cos_t, sin_t, cos_t, sin_t, vec(lq1), vec(lk1), vec(lq2), vec(lk2), vec(norm_w))


def _rope_tables(t):
    n = t - CTX_LEN
    axis_dim = DA_DQK // 2
    inv = ROPE_BASE ** (-jnp.arange(0, axis_dim, 2, dtype=F32) / axis_dim)
    tok = jnp.arange(n, dtype=jnp.int32)
    row = (tok // GRID_W).astype(F32)
    colp = (tok % GRID_W).astype(F32)
    ang = jnp.concatenate([row[:, None] * inv, colp[:, None] * inv], axis=-1)
    cos = jnp.repeat(jnp.cos(ang), 2, axis=-1)
    sin = jnp.repeat(jnp.sin(ang), 2, axis=-1) * jnp.tile(jnp.array([-1.0, 1.0], F32), DA_DQK // 2)
    reps = GROUP_W // DA_DQK
    cos = jnp.concatenate([jnp.ones((CTX_LEN, GROUP_W), F32), jnp.tile(cos, (1, reps))], axis=0)
    sin = jnp.concatenate([jnp.zeros((CTX_LEN, GROUP_W), F32), jnp.tile(sin, (1, reps))], axis=0)
    return cos, sin


def _s5_body(u_ref, bre_ref, bim_ref, cre_ref, cim_ref, disc_ref, dsk_ref, gw_ref, gb_ref, o_ref,
             xrf, xif, xrb, xib, y_scr, *, t_len):
    u = u_ref[0]
    ub = u.astype(BF16)
    y_scr[...] = u * dsk_ref[...]
    n_lat = t_len - CTX_LEN
    for half in range(2):
        ls = slice(half * S5_HALF, (half + 1) * S5_HALF)
        bu_r = jnp.dot(ub, bre_ref[:, ls], preferred_element_type=F32)
        bu_i = jnp.dot(ub, bim_ref[:, ls], preferred_element_type=F32)
        disc = disc_ref[:, ls]
        xrf[...] = disc[2:3] * bu_r - disc[3:4] * bu_i
        xif[...] = disc[2:3] * bu_i + disc[3:4] * bu_r
        xrb[...] = disc[6:7] * bu_r - disc[7:8] * bu_i
        xib[...] = disc[6:7] * bu_i + disc[7:8] * bu_r
        af_r, af_i, ab_r, ab_i = disc[0:1], disc[1:2], disc[4:5], disc[5:6]

        def step(i, carry):
            fr, fi, br, bi = carry
            tb = jnp.where(i < CTX_LEN, CTX_LEN - 1 - i, t_len + CTX_LEN - 1 - i)
            nfr = af_r * fr - af_i * fi + xrf[pl.ds(i, 1), :]
            nfi = af_r * fi + af_i * fr + xif[pl.ds(i, 1), :]
            nbr = ab_r * br - ab_i * bi + xrb[pl.ds(tb, 1), :]
            nbi = ab_r * bi + ab_i * br + xib[pl.ds(tb, 1), :]
            xrf[pl.ds(i, 1), :] = nfr
            xif[pl.ds(i, 1), :] = nfi
            xrb[pl.ds(tb, 1), :] = nbr
            xib[pl.ds(tb, 1), :] = nbi
            return nfr, nfi, nbr, nbi

        zero = jnp.zeros((1, S5_HALF), F32)
        lax.fori_loop(0, t_len, step, (zero, zero, zero, zero), unroll=8)
        cre = cre_ref[ls, :]
        cim = cim_ref[ls, :]
        y_scr[...] += (jnp.dot(xrf[...].astype(BF16), cre, preferred_element_type=F32)
                       - jnp.dot(xif[...].astype(BF16), cim, preferred_element_type=F32)
                       + jnp.dot(xrb[...].astype(BF16), cre, preferred_element_type=F32)
                       - jnp.dot(xib[...].astype(BF16), cim, preferred_element_type=F32))
    y = jax.nn.gelu(y_scr[...])
    gate = jax.nn.sigmoid(jnp.dot(y.astype(BF16), gw_ref[...], preferred_element_type=F32) + gb_ref[...])
    o_ref[0] = y * gate


def _s5_mixer(z, lam_re, lam_im, log_dt, b_re, b_im, c_re, c_im, d_skip, glu_w, glu_b):
    bsz, t, _ = z.shape
    eye = jnp.eye(S5_GROUPS, dtype=F32)
    bbd = lambda w: jnp.einsum('gpc,gh->gchp', w.astype(F32), eye).reshape(GROUP_W, S5_LANES).astype(BF16)
    cbd = lambda w: jnp.einsum('gcp,gh->gphc', w.astype(F32), eye).reshape(S5_LANES, GROUP_W).astype(BF16)
    rows = []
    for d in range(2):
        lr, li = lam_re[d].astype(F32), lam_im[d].astype(F32)
        dt = jnp.exp(log_dt[d].astype(F32))[:, None]
        mag = jnp.exp(lr * dt)
        ab_re, ab_im = mag * jnp.cos(li * dt), mag * jnp.sin(li * dt)
        den = lr * lr + li * li
        co_re = ((ab_re - 1.0) * lr + ab_im * li) / den
        co_im = (ab_im * lr - (ab_re - 1.0) * li) / den
        rows += [ab_re, ab_im, co_re, co_im]
    disc = jnp.stack([r.reshape(S5_LANES) for r in rows], axis=0)
    full = lambda shape: pl.BlockSpec(shape, lambda b: (0,) * len(shape))
    return pl.pallas_call(
        functools.partial(_s5_body, t_len=t),
        grid=(bsz,),
        in_specs=[pl.BlockSpec((1, t, GROUP_W), lambda b: (b, 0, S5_OFF // GROUP_W)),
                  full((GROUP_W, S5_LANES)), full((GROUP_W, S5_LANES)),
                  full((S5_LANES, GROUP_W)), full((S5_LANES, GROUP_W)),
                  full((8, S5_LANES)), full((1, GROUP_W)), full((GROUP_W, GROUP_W)), full((1, GROUP_W))],
        out_specs=pl.BlockSpec((1, t, GROUP_W), lambda b: (b, 0, 0)),
        out_shape=jax.ShapeDtypeStruct((bsz, t, GROUP_W), F32),
        scratch_shapes=[pltpu.VMEM((t, S5_HALF), F32)] * 4 + [pltpu.VMEM((t, GROUP_W), F32)],
        compiler_params=_cparams("arbitrary"),
    )(z, bbd(b_re), bbd(b_im), cbd(c_re), cbd(c_im), disc, d_skip.astype(F32).reshape(1, GROUP_W),
      glu_w.astype(BF16), glu_b.astype(F32).reshape(1, GROUP_W))


def rms_norm(x, w):
    xf = x.astype(F32)
    y = xf * lax.rsqrt(jnp.mean(xf * xf, axis=-1, keepdims=True) + EPS)
    return (y * w.astype(F32)).astype(x.dtype)


def head_rms_norm(x, w, n_heads):
    shp = x.shape
    xh = x.reshape(shp[:-1] + (n_heads, -1))
    return rms_norm(xh, w.reshape(n_heads, -1)).reshape(shp)


def flip_seq(a, reverse):
    return a[:, ::-1] if reverse else a


def gla_chunked(q, k, v, log_f, s0, with_out=True):
    bsz, t_len, nh, _ = q.shape
    nc = t_len // HG_CHUNK
    def chunks(a):
        return a.astype(F32).reshape(bsz, nc, HG_CHUNK, nh, a.shape[-1])
    q, k, v, log_f = chunks(q), chunks(k), chunks(v), chunks(log_f)
    b = jnp.cumsum(log_f, axis=2)
    b_end = b[:, :, -1]
    ds = jnp.einsum('bcshd,bcshv->bchdv', k * jnp.exp(b_end[:, :, None] - b), v)
    def step(s, inp):
        g_c, ds_c = inp
        return jnp.exp(g_c)[..., None] * s + ds_c, s
    s_fin, s_start = lax.scan(step, s0, (jnp.moveaxis(b_end, 1, 0), jnp.moveaxis(ds, 1, 0)))
    if not with_out:
        return None, s_fin
    s_start = jnp.moveaxis(s_start, 0, 1)
    tri = jnp.tril(jnp.ones((HG_CHUNK, HG_CHUNK), bool))[None, None, :, :, None, None]
    decay = jnp.exp(jnp.where(tri, b[:, :, :, None] - b[:, :, None], -jnp.inf))
    scores = jnp.sum(q[:, :, :, None] * decay * k[:, :, None], axis=-1)
    o = (jnp.einsum('bctsh,bcshv->bcthv', scores, v)
         + jnp.einsum('bcthd,bchdv->bcthv', q * jnp.exp(b), s_start))
    return o.reshape(bsz, t_len, nh, -1), s_fin


def hgrn2_mixer(zc, zl, lb, norm_w, with_ctx_out):
    def heads(a):
        return a.reshape(a.shape[:2] + (HG_HEADS, -1))
    def split(z):
        q, i, ff, fb, g = jnp.split(z, 5, axis=-1)
        return heads(q) * HG_DK ** -0.5, heads(i), (heads(ff), heads(fb)), g
    qc, ic, fc, gc = split(zc)
    ql, il, fl, gl = split(zl)
    oc, ol = 0.0, 0.0
    for d in range(2):
        lbd = lb[d].reshape(HG_HEADS, HG_DK)
        f_c = lbd + (1.0 - lbd) * jax.nn.sigmoid(fc[d].astype(F32))
        f_l = lbd + (1.0 - lbd) * jax.nn.sigmoid(fl[d].astype(F32))
        s0 = jnp.zeros((zc.shape[0], HG_HEADS, HG_DK, HG_DV), F32)
        o_c, s_c = gla_chunked(flip_seq(qc, d), flip_seq(1.0 - f_c, d), flip_seq(ic, d),
                               flip_seq(jnp.log(f_c), d), s0, with_ctx_out)
        o_l, _ = gla_chunked(flip_seq(ql, d), flip_seq(1.0 - f_l, d), flip_seq(il, d),
                             flip_seq(jnp.log(f_l), d), s_c)
        ol = ol + flip_seq(o_l, d)
        if with_ctx_out:
            oc = oc + flip_seq(o_c, d)
    def readout(o, g):
        o = o.reshape(o.shape[:2] + (-1,)).astype(g.dtype)
        return head_rms_norm(o, norm_w, HG_HEADS) * jax.nn.silu(g)
    return (readout(oc, gc) if with_ctx_out else None), readout(ol, gl)


def mlstm_chunked(q, k, v, log_i, log_f, state, with_out=True):
    bsz, t_len, nh, _ = q.shape
    nc = t_len // ML_CHUNK
    def chunks(a):
        return a.astype(F32).reshape((bsz, nc, ML_CHUNK) + a.shape[2:])
    q, k, v, log_i, log_f = chunks(q), chunks(k), chunks(v), chunks(log_i), chunks(log_f)
    b = jnp.cumsum(log_f, axis=2)
    b_end = b[:, :, -1]
    w_end = b_end[:, :, None] - b + log_i
    m_loc = jnp.max(w_end, axis=2)
    e_end = jnp.exp(w_end - m_loc[:, :, None])
    c_loc = jnp.einsum('bcshv,bcshd->bchvd', e_end[..., None] * v, k)
    n_loc = jnp.einsum('bcsh,bcshd->bchd', e_end, k)
    def step(carry, inp):
        c_st, n_st, m_st = carry
        be, ml, cl, nl = inp
        m_new = jnp.maximum(be + m_st, ml)
        a_old, a_loc = jnp.exp(be + m_st - m_new), jnp.exp(ml - m_new)
        c_new = a_old[..., None, None] * c_st + a_loc[..., None, None] * cl
        n_new = a_old[..., None] * n_st + a_loc[..., None] * nl
        return (c_new, n_new, m_new), carry
    mv = lambda a: jnp.moveaxis(a, 1, 0)
    final, starts = lax.scan(step, state, (mv(b_end), mv(m_loc), mv(c_loc), mv(n_loc)))
    if not with_out:
        return None, final
    c_s, n_s, m_s = (jnp.moveaxis(a, 0, 1) for a in starts)
    tri = jnp.tril(jnp.ones((ML_CHUNK, ML_CHUNK), bool))[None, None, :, :, None]
    d_log = jnp.where(tri, b[:, :, :, None] - b[:, :, None] + log_i[:, :, None], -jnp.inf)
    w_inter = b + m_s[:, :, None]
    m_t = jnp.maximum(w_inter, jnp.max(d_log, axis=3))
    e_inter = jnp.exp(w_inter - m_t)
    wts = jnp.exp(d_log - m_t[:, :, :, None]) * jnp.einsum('bcthd,bcshd->bctsh', q, k)
    num = (jnp.einsum('bctsh,bcshv->bcthv', wts, v)
           + e_inter[..., None] * jnp.einsum('bcthd,bchvd->bcthv', q, c_s))
    den = jnp.sum(wts, axis=3) + e_inter * jnp.einsum('bcthd,bchd->bcth', q, n_s)
    h = num / jnp.maximum(jnp.abs(den), jnp.exp(-m_t))[..., None]
    return h.reshape(bsz, t_len, nh, -1), final


def mlstm_mixer(zc, zl, norm_w, with_ctx_out):
    def split(z):
        shp = z.shape[:2]
        q, k, v, o = (z[..., j * GROUP_W:(j + 1) * GROUP_W] for j in range(4))
        gates = z[..., 4 * GROUP_W:].astype(F32).reshape(shp + (4, ML_HEADS))
        hd = lambda a: a.reshape(shp + (ML_HEADS, -1))
        return hd(q), hd(k) * ML_DK ** -0.5, hd(v), o, gates
    qc, kc, vc, oc, gtc = split(zc)
    ql, kl, vl, ol, gtl = split(zl)
    bsz = zc.shape[0]
    hc, hl = 0.0, 0.0
    for d in range(2):
        state0 = (jnp.zeros((bsz, ML_HEADS, ML_DV, ML_DK), F32), jnp.zeros((bsz, ML_HEADS, ML_DK), F32),
                  jnp.zeros((bsz, ML_HEADS), F32))
        h_c, st_c = mlstm_chunked(flip_seq(qc, d), flip_seq(kc, d), flip_seq(vc, d), flip_seq(gtc[:, :, d], d),
                                  flip_seq(jax.nn.log_sigmoid(gtc[:, :, 2 + d]), d), state0, with_ctx_out)
        h_l, _ = mlstm_chunked(flip_seq(ql, d), flip_seq(kl, d), flip_seq(vl, d), flip_seq(gtl[:, :, d], d),
                               flip_seq(jax.nn.log_sigmoid(gtl[:, :, 2 + d]), d), st_c)
        hl = hl + flip_seq(h_l, d)
        if with_ctx_out:
            hc = hc + flip_seq(h_c, d)
    def readout(h, o):
        h = h.reshape(h.shape[:2] + (-1,)).astype(o.dtype)
        return head_rms_norm(h, norm_w, ML_HEADS) * jax.nn.sigmoid(o)
    return (readout(hc, oc) if with_ctx_out else None), readout(hl, ol)


def _final_norm_body(x_ref, w_ref, o_ref):
    o_ref[0] = _rms(x_ref[0]) * w_ref[...]


def _final_norm(h, w):
    bsz, t, d = h.shape
    n = t - CTX_LEN
    blk0 = CTX_LEN // ROW_TILE
    return pl.pallas_call(
        _final_norm_body,
        grid=(bsz, n // ROW_TILE),
        in_specs=[pl.BlockSpec((1, ROW_TILE, d), lambda b, j: (b, blk0 + j, 0)),
                  pl.BlockSpec((1, d), lambda b, j: (0, 0))],
        out_specs=pl.BlockSpec((1, ROW_TILE, d), lambda b, j: (b, j, 0)),
        out_shape=jax.ShapeDtypeStruct((bsz, n, d), F32),
        compiler_params=_cparams("arbitrary", "arbitrary"),
    )(h, w.reshape(1, d))


def kernel(x, c, ctx, c_ctx, mod_w, mod_b, norm1_w, norm2_w, w_in, b_in, hg_lb_logits, hg_norm_w,
           s5_lam_re, s5_lam_im, s5_log_dt, s5_b_re, s5_b_im, s5_c_re, s5_c_im, s5_d, s5_glu_w, s5_glu_b,
           da_lq1, da_lk1, da_lq2, da_lk2, da_norm_w, ml_norm_w, w_out, router_w,
           exp_w1, exp_w3, exp_w2, final_norm_w):
    bsz, seq, d = x.shape
    assert ctx.shape[1] == CTX_LEN == ROW_TILE and seq % ROW_TILE == 0 and d == D_MODEL
    t = CTX_LEN + seq
    lb_all = jnp.cumsum(jax.nn.softmax(hg_lb_logits.astype(F32), axis=0), axis=0)
    lb_all = lb_all - lb_all[0]

    n_rows = -(-(bsz + 1) // 8) * 8
    cv = jnp.zeros((n_rows, d), F32).at[:bsz].set(c).at[bsz].set(c_ctx)
    mod_all = _modulation(cv, mod_w, mod_b)
    cos_t, sin_t = _rope_tables(t)
    h = jnp.concatenate([ctx, x], axis=1)

    for li in range(DEPTH):
        ctx_out = li < DEPTH - 1
        m6 = mod_all[li].reshape(n_rows, 6, d)
        modsel = jnp.stack([jnp.broadcast_to(m6[bsz], (bsz, 6, d)), m6[:bsz]], axis=1)
        w_pad = jnp.pad(w_in[li], ((0, 0), (0, IN_COLS_PAD - IN_COLS))).astype(BF16)
        b_pad = jnp.pad(b_in[li], (0, IN_COLS_PAD - IN_COLS)).reshape(1, IN_COLS_PAD)
        wg_t = w_in[li][:, ML_GATE_OFF:].T.astype(BF16)
        bg_col = b_in[li][ML_GATE_OFF:].reshape(N_GATES, 1)
        z, gates_t = _in_projection(h, modsel, norm1_w[li], w_pad, b_pad, wg_t, bg_col)

        zc, zl = z[:, :CTX_LEN, :IN_COLS], z[:, CTX_LEN:, :IN_COLS]
        a_c, a_l = hgrn2_mixer(zc[..., HG_OFF:S5_OFF], zl[..., HG_OFF:S5_OFF], lb_all[li], hg_norm_w[li], True)
        mix_a = jnp.concatenate([a_c, a_l], axis=1)
        mix_b = _s5_mixer(z, s5_lam_re[li], s5_lam_im[li], s5_log_dt[li], s5_b_re[li], s5_b_im[li],
                          s5_c_re[li], s5_c_im[li], s5_d[li], s5_glu_w[li], s5_glu_b[li])
        lam_init = 0.8 - 0.6 * math.exp(-0.3 * li)
        att_args = (da_lq1[li], da_lk1[li], da_lq2[li], da_lk2[li], da_norm_w[li], lam_init)
        c_l = _diff_attention(z, CTX_LEN, seq, t, True, cos_t, sin_t, *att_args)
        if ctx_out:
            c_c = _diff_attention(z, 0, CTX_LEN, CTX_LEN, False, cos_t, sin_t, *att_args)
        else:
            c_c = jnp.zeros((bsz, CTX_LEN, GROUP_W), F32)
        mix_c = jnp.concatenate([c_c, c_l], axis=1)
        d_c, d_l = mlstm_mixer(zc[..., ML_OFF:IN_COLS], zl[..., ML_OFF:IN_COLS], ml_norm_w[li], True)
        mix_d = jnp.concatenate([d_c, d_l], axis=1)

        hm, xm, aff_t = _out_projection((mix_a, mix_b, mix_c, mix_d), h, modsel, norm2_w[li],
                                        w_out[li].astype(BF16), router_w[li].T.astype(BF16))
        w1, w3, w2 = exp_w1[li].astype(BF16), exp_w3[li].astype(BF16), exp_w2[li].astype(BF16)
        slot, slot_t = _route(aff_t, CTX_LEN, seq)
        y_hi, y_lo = _moe_ffn(xm, slot, aff_t, CTX_LEN, seq, 1, w1, w3, w2)
        h_lat = _moe_combine(y_hi, y_lo, slot_t, hm, modsel, CTX_LEN, seq, 1)
        if ctx_out:
            slot, slot_t = _route(aff_t, 0, CTX_LEN)
            y_hi, y_lo = _moe_ffn(xm, slot, aff_t, 0, CTX_LEN, bsz, w1, w3, w2)
            h_ctx = _moe_combine(y_hi, y_lo, slot_t, hm, modsel, 0, CTX_LEN, 0)
        else:
            h_ctx = hm[:, :CTX_LEN]
        h = jnp.concatenate([h_ctx, h_lat], axis=1)
    return _final_norm(h, final_norm_w)
```

```python
import functools
import math
import jax
import jax.numpy as jnp
from jax import lax
from jax.experimental import pallas as pl
from jax.experimental.pallas import tpu as pltpu

D_MODEL = 1024
DEPTH = 2
CTX_LEN = 256
GRID_W = 64
N_MIXERS = 4
GROUP_W = D_MODEL // N_MIXERS
MIX_W = N_MIXERS * GROUP_W
EPS = 1e-6
F32 = jnp.float32
BF16 = jnp.bfloat16
HG_HEADS = 4
HG_DK = GROUP_W // HG_HEADS
HG_DV = GROUP_W // HG_HEADS
HG_CHUNK = 16
S5_CH = 16
S5_GROUPS = GROUP_W // S5_CH
S5_STATE = 64
DA_HEADS = 4
DA_DQK = GROUP_W // (2 * DA_HEADS)
DA_DV = GROUP_W // DA_HEADS
ROPE_BASE = 10000.0
ML_HEADS = 4
ML_DK = GROUP_W // ML_HEADS
ML_DV = GROUP_W // ML_HEADS
ML_CHUNK = 64
N_EXPERTS = 16
EC_CAPACITY = 2
D_EXPERT = 2 * D_MODEL
HG_OFF = 0
S5_OFF = HG_OFF + 5 * GROUP_W
DA_OFF = S5_OFF + GROUP_W
ML_OFF = DA_OFF + 3 * GROUP_W
ML_GATE_OFF = ML_OFF + 4 * GROUP_W
IN_COLS = ML_GATE_OFF + 4 * ML_HEADS

LANES = 128
ROW_TILE = 256
IN_COLS_PAD = ML_GATE_OFF + LANES
N_GATES = 4 * ML_HEADS
S5_LANES = S5_GROUPS * S5_STATE
S5_HALF = S5_LANES // 2
VMEM_LIMIT = 56 * 1024 * 1024


def _cparams(*sem):
    return pltpu.CompilerParams(dimension_semantics=sem, vmem_limit_bytes=VMEM_LIMIT)


def _rms(x):
    return x * lax.rsqrt(jnp.mean(x * x, axis=-1, keepdims=True) + EPS)


def _mod_body(cv_ref, w_ref, b_ref, o_ref):
    cv = cv_ref[...]
    s = cv * jax.nn.sigmoid(cv)
    o_ref[0] = jnp.dot(s.astype(BF16), w_ref[0].astype(BF16), preferred_element_type=F32) + b_ref[0]


def _modulation(cv, mod_w, mod_b):
    n_l, d, n6 = mod_w.shape
    r = cv.shape[0]
    tn = n6 // 4
    return pl.pallas_call(
        _mod_body,
        grid=(n_l, 4),
        in_specs=[pl.BlockSpec((r, d), lambda l, j: (0, 0)),
                  pl.BlockSpec((1, d, tn), lambda l, j: (l, 0, j)),
                  pl.BlockSpec((1, 1, tn), lambda l, j: (l, 0, j))],
        out_specs=pl.BlockSpec((1, r, tn), lambda l, j: (l, 0, j)),
        out_shape=jax.ShapeDtypeStruct((n_l, r, n6), F32),
        compiler_params=_cparams("arbitrary", "arbitrary"),
    )(cv, mod_w, mod_b.reshape(n_l, 1, n6))


def _inproj_body(h_ref, m_ref, nw_ref, w_ref, b_ref, wg_ref, bg_ref, z_ref, gt_ref):
    m = m_ref[0, 0]
    xn = _rms(h_ref[0]) * nw_ref[...] * (1.0 + m[1:2]) + m[0:1]
    xb = xn.astype(BF16)
    z_ref[0] = jnp.dot(xb, w_ref[...], preferred_element_type=F32) + b_ref[...]
    gt_ref[0] = lax.dot_general(wg_ref[...], xb, (((1,), (1,)), ((), ())),
                                preferred_element_type=F32) + bg_ref[...]


def _in_projection(h, modsel, norm_w, w_pad, b_pad, wg_t, bg_col):
    bsz, t, d = h.shape
    nt = t // ROW_TILE
    return pl.pallas_call(
        _inproj_body,
        grid=(bsz, nt),
        in_specs=[pl.BlockSpec((1, ROW_TILE, d), lambda b, j: (b, j, 0)),
                  pl.BlockSpec((1, 1, 6, d), lambda b, j: (b, jnp.minimum(j, 1), 0, 0)),
                  pl.BlockSpec((1, d), lambda b, j: (0, 0)),
                  pl.BlockSpec((d, IN_COLS_PAD), lambda b, j: (0, 0)),
                  pl.BlockSpec((1, IN_COLS_PAD), lambda b, j: (0, 0)),
                  pl.BlockSpec((N_GATES, d), lambda b, j: (0, 0)),
                  pl.BlockSpec((N_GATES, 1), lambda b, j: (0, 0))],
        out_specs=[pl.BlockSpec((1, ROW_TILE, IN_COLS_PAD), lambda b, j: (b, j, 0)),
                   pl.BlockSpec((1, N_GATES, ROW_TILE), lambda b, j: (b, 0, j))],
        out_shape=[jax.ShapeDtypeStruct((bsz, t, IN_COLS_PAD), F32),
                   jax.ShapeDtypeStruct((bsz, N_GATES, t), F32)],
        compiler_params=_cparams("arbitrary", "arbitrary"),
    )(h, modsel, norm_w.reshape(1, d), w_pad, b_pad, wg_t, bg_col)


def _outproj_body(a_ref, b_ref, c_ref, d_ref, h_ref, m_ref, nw_ref, wo_ref, rw_ref, hm_ref, xm_ref, aff_ref):
    m = m_ref[0, 0]
    y = jnp.concatenate([a_ref[0], b_ref[0], c_ref[0], d_ref[0]], axis=-1).astype(BF16)
    hm = h_ref[0] + m[2:3] * jnp.dot(y, wo_ref[...], preferred_element_type=F32)
    hm_ref[0] = hm
    xm = (_rms(hm) * nw_ref[...] * (1.0 + m[4:5]) + m[3:4]).astype(BF16)
    xm_ref[0] = xm
    logit = lax.dot_general(rw_ref[...], xm, (((1,), (1,)), ((), ())), preferred_element_type=F32)
    e = jnp.exp(logit - jnp.max(logit, axis=0, keepdims=True))
    aff_ref[0] = e / jnp.sum(e, axis=0, keepdims=True)


def _out_projection(mix, h, modsel, norm_w, wo_bf, rw_t):
    bsz, t, d = h.shape
    nt = t // ROW_TILE
    mix_spec = pl.BlockSpec((1, ROW_TILE, GROUP_W), lambda b, j: (b, j, 0))
    row_spec = pl.BlockSpec((1, ROW_TILE, d), lambda b, j: (b, j, 0))
    return pl.pallas_call(
        _outproj_body,
        grid=(bsz, nt),
        in_specs=[mix_spec, mix_spec, mix_spec, mix_spec, row_spec,
                  pl.BlockSpec((1, 1, 6, d), lambda b, j: (b, jnp.minimum(j, 1), 0, 0)),
                  pl.BlockSpec((1, d), lambda b, j: (0, 0)),
                  pl.BlockSpec((MIX_W, d), lambda b, j: (0, 0)),
                  pl.BlockSpec((N_EXPERTS, d), lambda b, j: (0, 0))],
        out_specs=[row_spec, row_spec, pl.BlockSpec((1, N_EXPERTS, ROW_TILE), lambda b, j: (b, 0, j))],
        out_shape=[jax.ShapeDtypeStruct((bsz, t, d), F32), jax.ShapeDtypeStruct((bsz, t, d), BF16),
                   jax.ShapeDtypeStruct((bsz, N_EXPERTS, t), F32)],
        compiler_params=_cparams("arbitrary", "arbitrary"),
    )(*mix, h, modsel, norm_w.reshape(1, d), wo_bf, rw_t)


def _topk_body(aff_ref, slot_ref, slot_t_ref, *, cap, n, row0):
    aff = aff_ref[0, :, row0:row0 + n]
    bits = pltpu.bitcast(aff, jnp.int32)
    thr = jnp.zeros((N_EXPERTS, 1), jnp.int32)
    for bit in range(30, -1, -1):
        cand = thr | (1 << bit)
        cnt = jnp.sum(jnp.where(bits >= cand, 1.0, 0.0), axis=-1, keepdims=True)
        thr = jnp.where(cnt >= cap, cand, thr)
    room = cap - jnp.sum(jnp.where(bits > thr, 1.0, 0.0), axis=-1, keepdims=True)
    ri = lax.broadcasted_iota(jnp.int32, (LANES, LANES), 0)
    ci = lax.broadcasted_iota(jnp.int32, (LANES, LANES), 1)
    incl = jnp.where(ri <= ci, 1.0, 0.0).astype(BF16)
    off_eq = jnp.zeros((N_EXPERTS, 1), F32)
    off_sel = jnp.zeros((N_EXPERTS, 1), F32)
    pieces = []
    for j in range(n // LANES):
        sl = slice(j * LANES, (j + 1) * LANES)
        bits_b = bits[:, sl]
        eq_b = jnp.where(bits_b == thr, 1.0, 0.0)
        pos_eq = jnp.dot(eq_b.astype(BF16), incl, preferred_element_type=F32) - eq_b + off_eq
        sel = jnp.where(bits_b > thr, 1.0, jnp.where(pos_eq < room, eq_b, 0.0))
        pos_sel = jnp.dot(sel.astype(BF16), incl, preferred_element_type=F32) - sel + off_sel
        piece = jnp.where(sel > 0.5, pos_sel, -1.0)
        slot_ref[0, :, sl] = piece.astype(jnp.int32)
        pieces.append(piece)
        off_eq = off_eq + jnp.sum(eq_b, axis=-1, keepdims=True)
        off_sel = off_sel + jnp.sum(sel, axis=-1, keepdims=True)
    pad = jnp.full((LANES - N_EXPERTS, LANES), -1.0, F32)
    for j in range(n // LANES):
        tile = jnp.concatenate([pieces[j], pad], axis=0)
        slot_t_ref[0, j * LANES:(j + 1) * LANES, :] = tile.T.astype(jnp.int32)


def _route(aff_t, row0, n):
    bsz, _, t = aff_t.shape
    cap = EC_CAPACITY * n // N_EXPERTS
    return pl.pallas_call(
        functools.partial(_topk_body, cap=cap, n=n, row0=row0),
        grid=(bsz,),
        in_specs=[pl.BlockSpec((1, N_EXPERTS, t), lambda b: (b, 0, 0))],
        out_specs=[pl.BlockSpec((1, N_EXPERTS, n), lambda b: (b, 0, 0)),
                   pl.BlockSpec((1, n, LANES), lambda b: (b, 0, 0))],
        out_shape=[jax.ShapeDtypeStruct((bsz, N_EXPERTS, n), jnp.int32),
                   jax.ShapeDtypeStruct((bsz, n, LANES), jnp.int32)],
        compiler_params=_cparams("arbitrary"),
    )(aff_t)


def _moe_ffn_body(x_ref, slot_ref, aff_ref, w1_ref, w3_ref, w2_ref, yh_ref, yl_ref, xe_scr, g_scr, *, nb, cap, n, off):
    for i in range(nb):
        hit = lax.broadcasted_iota(jnp.int32, (cap, n), 0) == slot_ref[i, 0]
        onehot = jnp.where(hit, 1.0, 0.0).astype(BF16)
        xe_scr[i * cap:(i + 1) * cap, :] = jnp.dot(onehot, x_ref[i, off:off + n, :],
                                                   preferred_element_type=F32).astype(BF16)
        g_scr[i * cap:(i + 1) * cap, :] = jnp.sum(jnp.where(hit, aff_ref[i, 0, :, off:off + n], 0.0),
                                                  axis=-1, keepdims=True)
    xe = xe_scr[...]
    h1 = jnp.dot(xe, w1_ref[0], preferred_element_type=F32)
    h3 = jnp.dot(xe, w3_ref[0], preferred_element_type=F32)
    act = (h1 * jax.nn.sigmoid(h1) * h3).astype(BF16)
    y = jnp.dot(act, w2_ref[0], preferred_element_type=F32) * g_scr[...]
    y_hi = y.astype(BF16)
    y_lo = (y - y_hi.astype(F32)).astype(BF16)
    for i in range(nb):
        yh_ref[i, 0] = y_hi[i * cap:(i + 1) * cap]
        yl_ref[i, 0] = y_lo[i * cap:(i + 1) * cap]


def _moe_ffn(xm, slot, aff_t, row0, n, nb, w1, w3, w2):
    bsz, t, d = xm.shape
    cap = EC_CAPACITY * n // N_EXPERTS
    rows, off = (n, 0) if row0 % n == 0 else (t, row0)
    blk = row0 // n if off == 0 else 0
    slot4 = slot.reshape(bsz, N_EXPERTS, 1, n)
    aff4 = aff_t.reshape(bsz, N_EXPERTS, 1, t)
    y_spec = pl.BlockSpec((nb, 1, cap, d), lambda e, b: (b, e, 0, 0))
    y_shape = jax.ShapeDtypeStruct((bsz, N_EXPERTS, cap, d), BF16)
    return pl.pallas_call(
        functools.partial(_moe_ffn_body, nb=nb, cap=cap, n=n, off=off),
        grid=(N_EXPERTS, bsz // nb),
        in_specs=[pl.BlockSpec((nb, rows, d), lambda e, b: (b, blk, 0)),
                  pl.BlockSpec((nb, 1, 1, n), lambda e, b: (b, e, 0, 0)),
                  pl.BlockSpec((nb, 1, 1, rows), lambda e, b: (b, e, 0, blk)),
                  pl.BlockSpec((1, d, D_EXPERT), lambda e, b: (e, 0, 0)),
                  pl.BlockSpec((1, d, D_EXPERT), lambda e, b: (e, 0, 0)),
                  pl.BlockSpec((1, D_EXPERT, d), lambda e, b: (e, 0, 0))],
        out_specs=[y_spec, y_spec],
        out_shape=[y_shape, y_shape],
        scratch_shapes=[pltpu.VMEM((nb * cap, d), BF16), pltpu.VMEM((nb * cap, 1), F32)],
        compiler_params=_cparams("arbitrary", "arbitrary"),
    )(xm, slot4, aff4, w1, w3, w2)


def _moe_combine_body(yh_ref, yl_ref, st_ref, hm_ref, m_ref, o_ref, *, cap):
    st = st_ref[0]
    lane = lax.broadcasted_iota(jnp.int32, (st.shape[0], cap), 1)
    acc = jnp.zeros(o_ref.shape[1:], F32)
    for e in range(N_EXPERTS):
        onehot = jnp.where(lane == st[:, e:e + 1], 1.0, 0.0).astype(BF16)
        acc = acc + jnp.dot(onehot, yh_ref[0, e], preferred_element_type=F32)
        acc = acc + jnp.dot(onehot, yl_ref[0, e], preferred_element_type=F32)
    o_ref[0] = hm_ref[0] + m_ref[0, 0][5:6] * acc


def _moe_combine(y_hi, y_lo, slot_t, hm, modsel, row0, n, is_latent):
    bsz, _, cap, d = y_hi.shape
    nt = n // ROW_TILE
    blk0 = row0 // ROW_TILE
    y_spec = pl.BlockSpec((1, N_EXPERTS, cap, d), lambda b, j: (b, 0, 0, 0))
    return pl.pallas_call(
        functools.partial(_moe_combine_body, cap=cap),
        grid=(bsz, nt),
        in_specs=[y_spec, y_spec,
                  pl.BlockSpec((1, ROW_TILE, LANES), lambda b, j: (b, j, 0)),
                  pl.BlockSpec((1, ROW_TILE, d), lambda b, j: (b, blk0 + j, 0)),
                  pl.BlockSpec((1, 1, 6, d), lambda b, j: (b, is_latent, 0, 0))],
        out_specs=pl.BlockSpec((1, ROW_TILE, d), lambda b, j: (b, j, 0)),
        out_shape=jax.ShapeDtypeStruct((bsz, n, d), F32),
        compiler_params=_cparams("arbitrary", "arbitrary"),
    )(y_hi, y_lo, slot_t, hm, modsel)


def _swap_pairs(x):
    w = x.shape[-1]
    lane = lax.broadcasted_iota(jnp.int32, x.shape, x.ndim - 1)
    return jnp.where(lane % 2 == 0, pltpu.roll(x, w - 1, x.ndim - 1), pltpu.roll(x, 1, x.ndim - 1))


def _attn_body(q_ref, k_ref, v_ref, cq_ref, sq_ref, ck_ref, sk_ref, lq1_ref, lk1_ref, lq2_ref, lk2_ref, nw_ref,
               o_ref, k_scr, v_scr, *, rope, lam_init):
    @pl.when(pl.program_id(1) == 0)
    def _():
        k = k_ref[0]
        if rope:
            k = k * ck_ref[...] + _swap_pairs(k) * sk_ref[...]
        k_scr[...] = k.astype(BF16)
        v_scr[...] = v_ref[0].astype(BF16)

    q = q_ref[0]
    if rope:
        q = q * cq_ref[...] + _swap_pairs(q) * sq_ref[...]
    lam = (jnp.exp(jnp.sum(lq1_ref[...] * lk1_ref[...], axis=-1, keepdims=True))
           - jnp.exp(jnp.sum(lq2_ref[...] * lk2_ref[...], axis=-1, keepdims=True)) + lam_init)
    scale = DA_DQK ** -0.5
    lane = lax.broadcasted_iota(jnp.int32, (1, GROUP_W), 1)
    kk = k_scr[...]
    vv = v_scr[...]
    o = jnp.zeros(q.shape, F32)
    inv = jnp.zeros(q.shape, F32)
    for h in range(DA_HEADS):
        probs = []
        for mi in range(2):
            qm = jnp.where(lane // DA_DQK == 2 * h + mi, q, 0.0).astype(BF16)
            s = lax.dot_general(qm, kk, (((1,), (1,)), ((), ())), preferred_element_type=F32) * scale
            e = jnp.exp(s - jnp.max(s, axis=-1, keepdims=True))
            probs.append(e / jnp.sum(e, axis=-1, keepdims=True))
        a = (probs[0] - lam * probs[1]).astype(BF16)
        head = lane // DA_DV == h
        oh = jnp.where(head, jnp.dot(a, vv, preferred_element_type=F32), 0.0)
        ms = jnp.sum(oh * oh, axis=-1, keepdims=True) * (1.0 / DA_DV)
        o = o + oh
        inv = jnp.where(head, lax.rsqrt(ms + EPS), inv)
    o_ref[0] = o * inv * nw_ref[...] * (1.0 - lam_init)


def _diff_attention(z, q_row0, nq, nk, rope, cos_t, sin_t, lq1, lk1, lq2, lk2, norm_w, lam_init):
    bsz = z.shape[0]
    qb = ROW_TILE
    q0 = q_row0 // qb
    col = DA_OFF // GROUP_W
    vec = lambda a: a.reshape(1, -1)
    small = pl.BlockSpec((1, DA_DQK), lambda b, j: (0, 0))
    return pl.pallas_call(
        functools.partial(_attn_body, rope=rope, lam_init=lam_init),
        grid=(bsz, nq // qb),
        in_specs=[pl.BlockSpec((1, qb, GROUP_W), lambda b, j: (b, q0 + j, col)),
                  pl.BlockSpec((1, nk, GROUP_W), lambda b, j: (b, 0, col + 1)),
                  pl.BlockSpec((1, nk, GROUP_W), lambda b, j: (b, 0, col + 2)),
                  pl.BlockSpec((qb, GROUP_W), lambda b, j: (q0 + j, 0)),
                  pl.BlockSpec((qb, GROUP_W), lambda b, j: (q0 + j, 0)),
                  pl.BlockSpec((nk, GROUP_W), lambda b, j: (0, 0)),
                  pl.BlockSpec((nk, GROUP_W), lambda b, j: (0, 0)),
                  small, small, small, small,
                  pl.BlockSpec((1, GROUP_W), lambda b, j: (0, 0))],
        out_specs=pl.BlockSpec((1, qb, GROUP_W), lambda b, j: (b, j, 0)),
        out_shape=jax.ShapeDtypeStruct((bsz, nq, GROUP_W), F32),
        scratch_shapes=[pltpu.VMEM((nk, GROUP_W), BF16), pltpu.VMEM((nk, GROUP_W), BF16)],
        compiler_params=_cparams("arbitrary", "arbitrary"),
    )(z, z, z, cos_t, sin_t, cos_t, sin_t, vec(lq1), vec(lk1), vec(lq2), vec(lk2), vec(norm_w))


def _rope_tables(t):
    n = t - CTX_LEN
    axis_dim = DA_DQK // 2
    inv = ROPE_BASE ** (-jnp.arange(0, axis_dim, 2, dtype=F32) / axis_dim)
    tok = jnp.arange(n, dtype=jnp.int32)
    row = (tok // GRID_W).astype(F32)
    colp = (tok % GRID_W).astype(F32)
    ang = jnp.concatenate([row[:, None] * inv, colp[:, None] * inv], axis=-1)
    cos = jnp.repeat(jnp.cos(ang), 2, axis=-1)
    sin = jnp.repeat(jnp.sin(ang), 2, axis=-1) * jnp.tile(jnp.array([-1.0, 1.0], F32), DA_DQK // 2)
    reps = GROUP_W // DA_DQK
    cos = jnp.concatenate([jnp.ones((CTX_LEN, GROUP_W), F32), jnp.tile(cos, (1, reps))], axis=0)
    sin = jnp.concatenate([jnp.zeros((CTX_LEN, GROUP_W), F32), jnp.tile(sin, (1, reps))], axis=0)
    return cos, sin


def _s5_body(u_ref, bre_ref, bim_ref, cre_ref, cim_ref, disc_ref, dsk_ref, gw_ref, gb_ref, o_ref,
             xrf, xif, xrb, xib, y_scr, *, t_len):
    u = u_ref[0]
    ub = u.astype(BF16)
    y_scr[...] = u * dsk_ref[...]
    n_lat = t_len - CTX_LEN
    for half in range(2):
        ls = slice(half * S5_HALF, (half + 1) * S5_HALF)
        bu_r = jnp.dot(ub, bre_ref[:, ls], preferred_element_type=F32)
        bu_i = jnp.dot(ub, bim_ref[:, ls], preferred_element_type=F32)
        disc = disc_ref[:, ls]
        xrf[...] = disc[2:3] * bu_r - disc[3:4] * bu_i
        xif[...] = disc[2:3] * bu_i + disc[3:4] * bu_r
        xrb[...] = disc[6:7] * bu_r - disc[7:8] * bu_i
        xib[...] = disc[6:7] * bu_i + disc[7:8] * bu_r
        af_r, af_i, ab_r, ab_i = disc[0:1], disc[1:2], disc[4:5], disc[5:6]

        def step(i, carry):
            fr, fi, br, bi = carry
            tb = jnp.where(i < CTX_LEN, CTX_LEN - 1 - i, t_len + CTX_LEN - 1 - i)
            nfr = af_r * fr - af_i * fi + xrf[pl.ds(i, 1), :]
            nfi = af_r * fi + af_i * fr + xif[pl.ds(i, 1), :]
            nbr = ab_r * br - ab_i * bi + xrb[pl.ds(tb, 1), :]
            nbi = ab_r * bi + ab_i * br + xib[pl.ds(tb, 1), :]
            xrf[pl.ds(i, 1), :] = nfr
            xif[pl.ds(i, 1), :] = nfi
            xrb[pl.ds(tb, 1), :] = nbr
            xib[pl.ds(tb, 1), :] = nbi
            return nfr, nfi, nbr, nbi

        zero = jnp.zeros((1, S5_HALF), F32)
        lax.fori_loop(0, t_len, step, (zero, zero, zero, zero), unroll=8)
        cre = cre_ref[ls, :]
        cim = cim_ref[ls, :]
        y_scr[...] += (jnp.dot(xrf[...].astype(BF16), cre, preferred_element_type=F32)
                       - jnp.dot(xif[...].astype(BF16), cim, preferred_element_type=F32)
                       + jnp.dot(xrb[...].astype(BF16), cre, preferred_element_type=F32)
                       - jnp.dot(xib[...].astype(BF16), cim, preferred_element_type=F32))
    y = jax.nn.gelu(y_scr[...])
    gate = jax.nn.sigmoid(jnp.dot(y.astype(BF16), gw_ref[...], preferred_element_type=F32) + gb_ref[...])
    o_ref[0] = y * gate


def _s5_mixer(z, lam_re, lam_im, log_dt, b_re, b_im, c_re, c_im, d_skip, glu_w, glu_b):
    bsz, t, _ = z.shape
    eye = jnp.eye(S5_GROUPS, dtype=F32)
    bbd = lambda w: jnp.einsum('gpc,gh->gchp', w.astype(F32), eye).reshape(GROUP_W, S5_LANES).astype(BF16)
    cbd = lambda w: jnp.einsum('gcp,gh->gphc', w.astype(F32), eye).reshape(S5_LANES, GROUP_W).astype(BF16)
    rows = []
    for d in range(2):
        lr, li = lam_re[d].astype(F32), lam_im[d].astype(F32)
        dt = jnp.exp(log_dt[d].astype(F32))[:, None]
        mag = jnp.exp(lr * dt)
        ab_re, ab_im = mag * jnp.cos(li * dt), mag * jnp.sin(li * dt)
        den = lr * lr + li * li
        co_re = ((ab_re - 1.0) * lr + ab_im * li) / den
        co_im = (ab_im * lr - (ab_re - 1.0) * li) / den
        rows += [ab_re, ab_im, co_re, co_im]
    disc = jnp.stack([r.reshape(S5_LANES) for r in rows], axis=0)
    full = lambda shape: pl.BlockSpec(shape, lambda b: (0,) * len(shape))
    return pl.pallas_call(
        functools.partial(_s5_body, t_len=t),
        grid=(bsz,),
        in_specs=[pl.BlockSpec((1, t, GROUP_W), lambda b: (b, 0, S5_OFF // GROUP_W)),
                  full((GROUP_W, S5_LANES)), full((GROUP_W, S5_LANES)),
                  full((S5_LANES, GROUP_W)), full((S5_LANES, GROUP_W)),
                  full((8, S5_LANES)), full((1, GROUP_W)), full((GROUP_W, GROUP_W)), full((1, GROUP_W))],
        out_specs=pl.BlockSpec((1, t, GROUP_W), lambda b: (b, 0, 0)),
        out_shape=jax.ShapeDtypeStruct((bsz, t, GROUP_W), F32),
        scratch_shapes=[pltpu.VMEM((t, S5_HALF), F32)] * 4 + [pltpu.VMEM((t, GROUP_W), F32)],
        compiler_params=_cparams("arbitrary"),
    )(z, bbd(b_re), bbd(b_im), cbd(c_re), cbd(c_im), disc, d_skip.astype(F32).reshape(1, GROUP_W),
      glu_w.astype(BF16), glu_b.astype(F32).reshape(1, GROUP_W))


def rms_norm(x, w):
    xf = x.astype(F32)
    y = xf * lax.rsqrt(jnp.mean(xf * xf, axis=-1, keepdims=True) + EPS)
    return (y * w.astype(F32)).astype(x.dtype)


def head_rms_norm(x, w, n_heads):
    shp = x.shape
    xh = x.reshape(shp[:-1] + (n_heads, -1))
    return rms_norm(xh, w.reshape(n_heads, -1)).reshape(shp)


def flip_seq(a, reverse):
    return a[:, ::-1] if reverse else a


def gla_chunked(q, k, v, log_f, s0, with_out=True):
    bsz, t_len, nh, _ = q.shape
    nc = t_len // HG_CHUNK
    def chunks(a):
        return a.astype(F32).reshape(bsz, nc, HG_CHUNK, nh, a.shape[-1])
    q, k, v, log_f = chunks(q), chunks(k), chunks(v), chunks(log_f)
    b = jnp.cumsum(log_f, axis=2)
    b_end = b[:, :, -1]
    ds = jnp.einsum('bcshd,bcshv->bchdv', k * jnp.exp(b_end[:, :, None] - b), v)
    def step(s, inp):
        g_c, ds_c = inp
        return jnp.exp(g_c)[..., None] * s + ds_c, s
    s_fin, s_start = lax.scan(step, s0, (jnp.moveaxis(b_end, 1, 0), jnp.moveaxis(ds, 1, 0)))
    if not with_out:
        return None, s_fin
    s_start = jnp.moveaxis(s_start, 0, 1)
    tri = jnp.tril(jnp.ones((HG_CHUNK, HG_CHUNK), bool))[None, None, :, :, None, None]
    decay = jnp.exp(jnp.where(tri, b[:, :, :, None] - b[:, :, None], -jnp.inf))
    scores = jnp.sum(q[:, :, :, None] * decay * k[:, :, None], axis=-1)
    o = (jnp.einsum('bctsh,bcshv->bcthv', scores, v)
         + jnp.einsum('bcthd,bchdv->bcthv', q * jnp.exp(b), s_start))
    return o.reshape(bsz, t_len, nh, -1), s_fin


def hgrn2_mixer(zc, zl, lb, norm_w, with_ctx_out):
    def heads(a):
        return a.reshape(a.shape[:2] + (HG_HEADS, -1))
    def split(z):
        q, i, ff, fb, g = jnp.split(z, 5, axis=-1)
        return heads(q) * HG_DK ** -0.5, heads(i), (heads(ff), heads(fb)), g
    qc, ic, fc, gc = split(zc)
    ql, il, fl, gl = split(zl)
    oc, ol = 0.0, 0.0
    for d in range(2):
        lbd = lb[d].reshape(HG_HEADS, HG_DK)
        f_c = lbd + (1.0 - lbd) * jax.nn.sigmoid(fc[d].astype(F32))
        f_l = lbd + (1.0 - lbd) * jax.nn.sigmoid(fl[d].astype(F32))
        s0 = jnp.zeros((zc.shape[0], HG_HEADS, HG_DK, HG_DV), F32)
        o_c, s_c = gla_chunked(flip_seq(qc, d), flip_seq(1.0 - f_c, d), flip_seq(ic, d),
                               flip_seq(jnp.log(f_c), d), s0, with_ctx_out)
        o_l, _ = gla_chunked(flip_seq(ql, d), flip_seq(1.0 - f_l, d), flip_seq(il, d),
                             flip_seq(jnp.log(f_l), d), s_c)
        ol = ol + flip_seq(o_l, d)
        if with_ctx_out:
            oc = oc + flip_seq(o_c, d)
    def readout(o, g):
        o = o.reshape(o.shape[:2] + (-1,)).astype(g.dtype)
        return head_rms_norm(o, norm_w, HG_HEADS) * jax.nn.silu(g)
    return (readout(oc, gc) if with_ctx_out else None), readout(ol, gl)


def mlstm_chunked(q, k, v, log_i, log_f, state, with_out=True):
    bsz, t_len, nh, _ = q.shape
    nc = t_len // ML_CHUNK
    def chunks(a):
        return a.astype(F32).reshape((bsz, nc, ML_CHUNK) + a.shape[2:])
    q, k, v, log_i, log_f = chunks(q), chunks(k), chunks(v), chunks(log_i), chunks(log_f)
    b = jnp.cumsum(log_f, axis=2)
    b_end = b[:, :, -1]
    w_end = b_end[:, :, None] - b + log_i
    m_loc = jnp.max(w_end, axis=2)
    e_end = jnp.exp(w_end - m_loc[:, :, None])
    c_loc = jnp.einsum('bcshv,bcshd->bchvd', e_end[..., None] * v, k)
    n_loc = jnp.einsum('bcsh,bcshd->bchd', e_end, k)
    def step(carry, inp):
        c_st, n_st, m_st = carry
        be, ml, cl, nl = inp
        m_new = jnp.maximum(be + m_st, ml)
        a_old, a_loc = jnp.exp(be + m_st - m_new), jnp.exp(ml - m_new)
        c_new = a_old[..., None, None] * c_st + a_loc[..., None, None] * cl
        n_new = a_old[..., None] * n_st + a_loc[..., None] * nl
        return (c_new, n_new, m_new), carry
    mv = lambda a: jnp.moveaxis(a, 1, 0)
    final, starts = lax.scan(step, state, (mv(b_end), mv(m_loc), mv(c_loc), mv(n_loc)))
    if not with_out:
        return None, final
    c_s, n_s, m_s = (jnp.moveaxis(a, 0, 1) for a in starts)
    tri = jnp.tril(jnp.ones((ML_CHUNK, ML_CHUNK), bool))[None, None, :, :, None]
    d_log = jnp.where(tri, b[:, :, :, None] - b[:, :, None] + log_i[:, :, None], -jnp.inf)
    w_inter = b + m_s[:, :, None]
    m_t = jnp.maximum(w_inter, jnp.max(d_log, axis=3))
    e_inter = jnp.exp(w_inter - m_t)
    wts = jnp.exp(d_log - m_t[:, :, :, None]) * jnp.einsum('bcthd,bcshd->bctsh', q, k)
    num = (jnp.einsum('bctsh,bcshv->bcthv', wts, v)
           + e_inter[..., None] * jnp.einsum('bcthd,bchvd->bcthv', q, c_s))
    den = jnp.sum(wts, axis=3) + e_inter * jnp.einsum('bcthd,bchd->bcth', q, n_s)
    h = num / jnp.maximum(jnp.abs(den), jnp.exp(-m_t))[..., None]
    return h.reshape(bsz, t_len, nh, -1), final


def mlstm_mixer(zc, zl, norm_w, with_ctx_out):
    def split(z):
        shp = z.shape[:2]
        q, k, v, o = (z[..., j * GROUP_W:(j + 1) * GROUP_W] for j in range(4))
        gates = z[..., 4 * GROUP_W:].astype(F32).reshape(shp + (4, ML_HEADS))
        hd = lambda a: a.reshape(shp + (ML_HEADS, -1))
        return hd(q), hd(k) * ML_DK ** -0.5, hd(v), o, gates
    qc, kc, vc, oc, gtc = split(zc)
    ql, kl, vl, ol, gtl = split(zl)
    bsz = zc.shape[0]
    hc, hl = 0.0, 0.0
    for d in range(2):
        state0 = (jnp.zeros((bsz, ML_HEADS, ML_DV, ML_DK), F32), jnp.zeros((bsz, ML_HEADS, ML_DK), F32),
                  jnp.zeros((bsz, ML_HEADS), F32))
        h_c, st_c = mlstm_chunked(flip_seq(qc, d), flip_seq(kc, d), flip_seq(vc, d), flip_seq(gtc[:, :, d], d),
                                  flip_seq(jax.nn.log_sigmoid(gtc[:, :, 2 + d]), d), state0, with_ctx_out)
        h_l, _ = mlstm_chunked(flip_seq(ql, d), flip_seq(kl, d), flip_seq(vl, d), flip_seq(gtl[:, :, d], d),
                               flip_seq(jax.nn.log_sigmoid(gtl[:, :, 2 + d]), d), st_c)
        hl = hl + flip_seq(h_l, d)
        if with_ctx_out:
            hc = hc + flip_seq(h_c, d)
    def readout(h, o):
        h = h.reshape(h.shape[:2] + (-1,)).astype(o.dtype)
        return head_rms_norm(h, norm_w, ML_HEADS) * jax.nn.sigmoid(o)
    return (readout(hc, oc) if with_ctx_out else None), readout(hl, ol)


def _split3(x):
    hi = x.astype(BF16)
    r = x - hi.astype(F32)
    mid = r.astype(BF16)
    return hi, mid, (r - mid.astype(F32)).astype(BF16)


def _dot3(x, sel):
    return sum(jnp.dot(p, sel, preferred_element_type=F32) for p in _split3(x))


def _mdot3(sel, x):
    return sum(jnp.dot(sel, p, preferred_element_type=F32) for p in _split3(x))


def _dot2(x, sel):
    hi = x.astype(BF16)
    lo = (x - hi.astype(F32)).astype(BF16)
    return jnp.dot(hi, sel, preferred_element_type=F32) + jnp.dot(lo, sel, preferred_element_type=F32)


def _head_blocks(n, seg):
    ri = lax.broadcasted_iota(jnp.int32, (n, n), 0)
    ci = lax.broadcasted_iota(jnp.int32, (n, n), 1)
    return ri // seg == ci // seg, ri, ci


def _chunk_cumsum_cols(x, chunk, reverse):
    same, ri, ci = _head_blocks(ROW_TILE, chunk)
    tri = jnp.where(same & ((ci >= ri) if reverse else (ci <= ri)), 1.0, 0.0).astype(BF16)
    return [_mdot3(tri, x[r:r + ROW_TILE]) for r in range(0, x.shape[0], ROW_TILE)]


def _bwd_chunk(i, n_ctx, n_all):
    return jnp.where(i < n_ctx, n_ctx - 1 - i, n_all + n_ctx - 1 - i)


def _head_rms(x, ebd, seg):
    return x * lax.rsqrt(_dot2(x * x, ebd) * (1.0 / seg) + EPS)


def _hgrn2_body(z_ref, lb_ref, nw_ref, out_ref, bsc, ksc, osc, st, *, t_len):
    for d in range(2):
        lb = lb_ref[d:d + 1, :]
        f = lb + (1.0 - lb) * jax.nn.sigmoid(z_ref[0, :, (2 + d) * GROUP_W:(3 + d) * GROUP_W])
        ksc[d] = 1.0 - f
        for j, blk in enumerate(_chunk_cumsum_cols(jnp.log(f), HG_CHUNK, d == 1)):
            bsc[d, j * ROW_TILE:(j + 1) * ROW_TILE, :] = blk
    st[...] = jnp.zeros(st.shape, F32)
    bd, _, _ = _head_blocks(GROUP_W, HG_DK)
    ebd = jnp.where(bd, 1.0, 0.0).astype(BF16)
    t3 = lax.broadcasted_iota(jnp.int32, (HG_CHUNK, HG_CHUNK, GROUP_W), 0)
    s3 = lax.broadcasted_iota(jnp.int32, (HG_CHUNK, HG_CHUNK, GROUP_W), 1)
    n_all = t_len // HG_CHUNK
    n_ctx = CTX_LEN // HG_CHUNK

    def chunk(d, c):
        r0 = pl.multiple_of(c * HG_CHUNK, HG_CHUNK)
        rows = pl.ds(r0, HG_CHUNK)
        q = z_ref[0, rows, 0:GROUP_W] * (HG_DK ** -0.5)
        v = z_ref[0, rows, GROUP_W:2 * GROUP_W]
        k = ksc[d, rows, :]
        b = bsc[d, rows, :]
        b_end = b[0:1] if d else b[HG_CHUNK - 1:HG_CHUNK]
        valid = (s3 >= t3) if d else (s3 <= t3)
        a = jnp.where(valid, jnp.exp(b[:, None, :] - b[None, :, :]), 0.0) * q[:, None, :] * k[None, :, :]
        sc = _dot2(a.reshape(HG_CHUNK * HG_CHUNK, GROUP_W), ebd)
        o_intra = jnp.sum(sc.reshape(HG_CHUNK, HG_CHUNK, GROUP_W) * v[None, :, :], axis=1)
        s_prev = st[d]
        o_inter = lax.dot_general((q * jnp.exp(b)).astype(BF16), s_prev.astype(BF16), (((1,), (1,)), ((), ())),
                                  preferred_element_type=F32)
        osc[d, rows, :] = o_intra + o_inter
        kd = (k * jnp.exp(b_end - b)).astype(BF16)
        upd = lax.dot_general(v.astype(BF16), kd, (((0,), (0,)), ((), ())), preferred_element_type=F32)
        st[d] = s_prev * jnp.exp(b_end) + jnp.where(bd, upd, 0.0)

    def step(i, carry):
        chunk(0, i)
        chunk(1, _bwd_chunk(i, n_ctx, n_all))
        return carry

    lax.fori_loop(0, n_all, step, 0)
    o = osc[0] + osc[1]
    g = z_ref[0, :, 4 * GROUP_W:5 * GROUP_W]
    out_ref[0] = _head_rms(o, ebd, HG_DV) * nw_ref[...] * (g * jax.nn.sigmoid(g))


def _hgrn2_mixer(z, lb, norm_w):
    bsz, t, _ = z.shape
    return pl.pallas_call(
        functools.partial(_hgrn2_body, t_len=t),
        grid=(bsz,),
        in_specs=[pl.BlockSpec((1, t, 5 * GROUP_W), lambda b: (b, 0, 0)),
                  pl.BlockSpec((2, GROUP_W), lambda b: (0, 0)),
                  pl.BlockSpec((1, GROUP_W), lambda b: (0, 0))],
        out_specs=pl.BlockSpec((1, t, GROUP_W), lambda b: (b, 0, 0)),
        out_shape=jax.ShapeDtypeStruct((bsz, t, GROUP_W), F32),
        scratch_shapes=[pltpu.VMEM((2, t, GROUP_W), F32), pltpu.VMEM((2, t, GROUP_W), F32),
                        pltpu.VMEM((2, t, GROUP_W), F32), pltpu.VMEM((2, GROUP_W, GROUP_W), F32)],
        compiler_params=_cparams("arbitrary"),
    )(z, lb.astype(F32), norm_w.astype(F32).reshape(1, GROUP_W))


def _lane_expand(cols, seg):
    lane = lax.broadcasted_iota(jnp.int32, (1, len(cols) * seg), 1)
    out = cols[-1]
    for h in range(len(cols) - 2, -1, -1):
        out = jnp.where(lane < (h + 1) * seg, cols[h], out)
    return out


def _mlstm_body(q_ref, k_ref, v_ref, o_ref, zg_ref, gr_ref, nw_ref, out_ref,
                bce, lice, brow, hsc, cst, nst, *, t_len):
    n_all = t_len // ML_CHUNK
    n_ctx = CTX_LEN // ML_CHUNK
    g = zg_ref[0]
    lf = jax.nn.log_sigmoid(g)
    src = lax.broadcasted_iota(jnp.int32, (LANES, GROUP_W), 0)
    dst_head = lax.broadcasted_iota(jnp.int32, (LANES, GROUP_W), 1) // ML_DK
    bd, _, _ = _head_blocks(GROUP_W, ML_DK)
    ebd = jnp.where(bd, 1.0, 0.0).astype(BF16)
    for d in range(2):
        lice[d] = _dot3(g, jnp.where(src == ML_HEADS * d + dst_head, 1.0, 0.0).astype(BF16))
        lf_exp = _dot3(lf, jnp.where(src == ML_HEADS * (2 + d) + dst_head, 1.0, 0.0).astype(BF16))
        for j, blk in enumerate(_chunk_cumsum_cols(lf_exp, ML_CHUNK, d == 1)):
            bce[d, j * ROW_TILE:(j + 1) * ROW_TILE, :] = blk
        _, ri, ci = _head_blocks(GROUP_W, ML_CHUNK)
        tri = jnp.where(bd & ((ri >= ci) if d else (ri <= ci)), 1.0, 0.0).astype(BF16)
        brow[d] = _dot3(jax.nn.log_sigmoid(gr_ref[0, 2 + d]), tri)
    cst[...] = jnp.zeros(cst.shape, F32)
    nst[...] = jnp.zeros(nst.shape, F32)
    tt = lax.broadcasted_iota(jnp.int32, (ML_CHUNK, GROUP_W), 0)
    ss = lax.broadcasted_iota(jnp.int32, (ML_CHUNK, GROUP_W), 1) % ML_CHUNK
    lane_h = lax.broadcasted_iota(jnp.int32, (1, LANES), 1) < ML_CHUNK

    def chunk(d, c, m_prev):
        rows = pl.ds(pl.multiple_of(c * ML_CHUNK, ML_CHUNK), ML_CHUNK)
        q = q_ref[0, rows, :]
        k = k_ref[0, rows, :] * (ML_DK ** -0.5)
        v = v_ref[0, rows, :]
        bc = bce[d, rows, :]
        lic = lice[d, rows, :]
        br = brow[d, pl.ds(c, 1), :]
        lir = gr_ref[0, d, pl.ds(c, 1), :]
        b_end = bc[0:1] if d else bc[ML_CHUNK - 1:ML_CHUNK]
        valid = (ss >= tt) if d else (ss <= tt)
        d_log = jnp.where(valid, bc - br + lir, -jnp.inf)
        cols = []
        for hp in range(GROUP_W // LANES):
            xs = d_log[:, hp * LANES:(hp + 1) * LANES]
            cols.append(jnp.max(jnp.where(lane_h, xs, -jnp.inf), axis=-1, keepdims=True))
            cols.append(jnp.max(jnp.where(lane_h, -jnp.inf, xs), axis=-1, keepdims=True))
        w_inter = bc + m_prev
        m_t = jnp.maximum(w_inter, _lane_expand(cols, ML_CHUNK))
        e_inter = jnp.exp(w_inter - m_t)
        qb = q.astype(BF16)
        kbd = jnp.where(bd, jnp.concatenate([k] * ML_HEADS, axis=0).T, 0.0).astype(BF16)
        vbd = jnp.where(bd, jnp.concatenate([v] * ML_HEADS, axis=0), 0.0).astype(BF16)
        wts = jnp.exp(d_log - m_t) * jnp.dot(qb, kbd, preferred_element_type=F32)
        w_hi = wts.astype(BF16)
        w_lo = (wts - w_hi.astype(F32)).astype(BF16)
        c_prev, n_prev = cst[d], nst[d]
        num = (jnp.dot(w_hi, vbd, preferred_element_type=F32)
               + e_inter * jnp.dot(qb, c_prev.astype(BF16), preferred_element_type=F32))
        den = (jnp.dot(w_hi, ebd, preferred_element_type=F32) + jnp.dot(w_lo, ebd, preferred_element_type=F32)
               + e_inter * jnp.dot(qb, n_prev.astype(BF16), preferred_element_type=F32))
        hsc[d, rows, :] = num / jnp.maximum(jnp.abs(den), jnp.exp(-m_t))
        w_end = b_end - bc + lic
        m_loc = jnp.max(w_end, axis=0, keepdims=True)
        e_end = jnp.exp(w_end - m_loc)
        m_new = jnp.maximum(b_end + m_prev, m_loc)
        a_old = jnp.exp(b_end + m_prev - m_new)
        a_loc = jnp.exp(m_loc - m_new)
        evbd = jnp.where(bd, jnp.concatenate([e_end * v] * ML_HEADS, axis=0), 0.0).astype(BF16)
        eebd = jnp.where(bd, jnp.concatenate([e_end] * ML_HEADS, axis=0), 0.0).astype(BF16)
        cst[d] = a_old * c_prev + a_loc * jnp.dot(kbd, evbd, preferred_element_type=F32)
        nst[d] = a_old * n_prev + a_loc * jnp.dot(kbd, eebd, preferred_element_type=F32)
        return m_new

    def step(i, carry):
        return chunk(0, i, carry[0]), chunk(1, _bwd_chunk(i, n_ctx, n_all), carry[1])

    zero = jnp.zeros((1, GROUP_W), F32)
    lax.fori_loop(0, n_all, step, (zero, zero))
    out_ref[0] = _head_rms(hsc[0] + hsc[1], ebd, ML_DV) * nw_ref[...] * jax.nn.sigmoid(o_ref[0])


def _mlstm_mixer(z, gates_t, norm_w):
    bsz, t, _ = z.shape
    nc = t // ML_CHUNK
    ncp = -(-nc // 8) * 8
    gr = gates_t.reshape(bsz, 4, ML_HEADS, nc, ML_CHUNK).transpose(0, 1, 3, 2, 4).reshape(bsz, 4, nc, GROUP_W)
    gr = jnp.pad(gr, ((0, 0), (0, 0), (0, ncp - nc), (0, 0)))
    col = ML_OFF // GROUP_W
    zspec = lambda j: pl.BlockSpec((1, t, GROUP_W), lambda b: (b, 0, col + j))
    return pl.pallas_call(
        functools.partial(_mlstm_body, t_len=t),
        grid=(bsz,),
        in_specs=[zspec(0), zspec(1), zspec(2), zspec(3),
                  pl.BlockSpec((1, t, LANES), lambda b: (b, 0, ML_GATE_OFF // LANES)),
                  pl.BlockSpec((1, 4, ncp, GROUP_W), lambda b: (b, 0, 0, 0)),
                  pl.BlockSpec((1, GROUP_W), lambda b: (0, 0))],
        out_specs=pl.BlockSpec((1, t, GROUP_W), lambda b: (b, 0, 0)),
        out_shape=jax.ShapeDtypeStruct((bsz, t, GROUP_W), F32),
        scratch_shapes=[pltpu.VMEM((2, t, GROUP_W), F32), pltpu.VMEM((2, t, GROUP_W), F32),
                        pltpu.VMEM((2, ncp, GROUP_W), F32), pltpu.VMEM((2, t, GROUP_W), F32),
                        pltpu.VMEM((2, GROUP_W, GROUP_W), F32), pltpu.VMEM((2, GROUP_W, GROUP_W), F32)],
        compiler_params=_cparams("arbitrary"),
    )(z, z, z, z, z, gr, norm_w.astype(F32).reshape(1, GROUP_W))


def _final_norm_body(x_ref, w_ref, o_ref):
    o_ref[0] = _rms(x_ref[0]) * w_ref[...]


def _final_norm(h, w):
    bsz, t, d = h.shape
    n = t - CTX_LEN
    blk0 = CTX_LEN // ROW_TILE
    return pl.pallas_call(
        _final_norm_body,
        grid=(bsz, n // ROW_TILE),
        in_specs=[pl.BlockSpec((1, ROW_TILE, d), lambda b, j: (b, blk0 + j, 0)),
                  pl.BlockSpec((1, d), lambda b, j: (0, 0))],
        out_specs=pl.BlockSpec((1, ROW_TILE, d), lambda b, j: (b, j, 0)),
        out_shape=jax.ShapeDtypeStruct((bsz, n, d), F32),
        compiler_params=_cparams("arbitrary", "arbitrary"),
    )(h, w.reshape(1, d))


def kernel(x, c, ctx, c_ctx, mod_w, mod_b, norm1_w, norm2_w, w_in, b_in, hg_lb_logits, hg_norm_w,
           s5_lam_re, s5_lam_im, s5_log_dt, s5_b_re, s5_b_im, s5_c_re, s5_c_im, s5_d, s5_glu_w, s5_glu_b,
           da_lq1, da_lk1, da_lq2, da_lk2, da_norm_w, ml_norm_w, w_out, router_w,
           exp_w1, exp_w3, exp_w2, final_norm_w):
    bsz, seq, d = x.shape
    assert ctx.shape[1] == CTX_LEN == ROW_TILE and seq % ROW_TILE == 0 and d == D_MODEL
    t = CTX_LEN + seq
    lb_all = jnp.cumsum(jax.nn.softmax(hg_lb_logits.astype(F32), axis=0), axis=0)
    lb_all = lb_all - lb_all[0]

    n_rows = -(-(bsz + 1) // 8) * 8
    cv = jnp.zeros((n_rows, d), F32).at[:bsz].set(c).at[bsz].set(c_ctx)
    mod_all = _modulation(cv, mod_w, mod_b)
    cos_t, sin_t = _rope_tables(t)
    h = jnp.concatenate([ctx, x], axis=1)

    for li in range(DEPTH):
        ctx_out = li < DEPTH - 1
        m6 = mod_all[li].reshape(n_rows, 6, d)
        modsel = jnp.stack([jnp.broadcast_to(m6[bsz], (bsz, 6, d)), m6[:bsz]], axis=1)
        w_pad = jnp.pad(w_in[li], ((0, 0), (0, IN_COLS_PAD - IN_COLS))).astype(BF16)
        b_pad = jnp.pad(b_in[li], (0, IN_COLS_PAD - IN_COLS)).reshape(1, IN_COLS_PAD)
        wg_t = w_in[li][:, ML_GATE_OFF:].T.astype(BF16)
        bg_col = b_in[li][ML_GATE_OFF:].reshape(N_GATES, 1)
        z, gates_t = _in_projection(h, modsel, norm1_w[li], w_pad, b_pad, wg_t, bg_col)

        mix_a = _hgrn2_mixer(z, lb_all[li], hg_norm_w[li])
        mix_b = _s5_mixer(z, s5_lam_re[li], s5_lam_im[li], s5_log_dt[li], s5_b_re[li], s5_b_im[li],
                          s5_c_re[li], s5_c_im[li], s5_d[li], s5_glu_w[li], s5_glu_b[li])
        lam_init = 0.8 - 0.6 * math.exp(-0.3 * li)
        att_args = (da_lq1[li], da_lk1[li], da_lq2[li], da_lk2[li], da_norm_w[li], lam_init)
        c_l = _diff_attention(z, CTX_LEN, seq, t, True, cos_t, sin_t, *att_args)
        if ctx_out:
            c_c = _diff_attention(z, 0, CTX_LEN, CTX_LEN, False, cos_t, sin_t, *att_args)
        else:
            c_c = jnp.zeros((bsz, CTX_LEN, GROUP_W), F32)
        mix_c = jnp.concatenate([c_c, c_l], axis=1)
        mix_d = _mlstm_mixer(z, gates_t, ml_norm_w[li])

        hm, xm, aff_t = _out_projection((mix_a, mix_b, mix_c, mix_d), h, modsel, norm2_w[li],
                                        w_out[li].astype(BF16), router_w[li].T.astype(BF16))
        w1, w3, w2 = exp_w1[li].astype(BF16), exp_w3[li].astype(BF16), exp_w2[li].astype(BF16)
        slot, slot_t = _route(aff_t, CTX_LEN, seq)
        y_hi, y_lo = _moe_ffn(xm, slot, aff_t, CTX_LEN, seq, 1, w1, w3, w2)
        h_lat = _moe_combine(y_hi, y_lo, slot_t, hm, modsel, CTX_LEN, seq, 1)
        if ctx_out:
            slot, slot_t = _route(aff_t, 0, CTX_LEN)
            y_hi, y_lo = _moe_ffn(xm, slot, aff_t, 0, CTX_LEN, bsz, w1, w3, w2)
            h_ctx = _moe_combine(y_hi, y_lo, slot_t, hm, modsel, 0, CTX_LEN, 0)
        else:
            h_ctx = hm[:, :CTX_LEN]
        h = jnp.concatenate([h_ctx, h_lat], axis=1)
    return _final_norm(h, final_norm_w)
```

```python
import functools
import math
import jax
import jax.numpy as jnp
from jax import lax
from jax.experimental import pallas as pl
from jax.experimental.pallas import tpu as pltpu

D_MODEL = 1024
DEPTH = 2
CTX_LEN = 256
GRID_W = 64
N_MIXERS = 4
GROUP_W = D_MODEL // N_MIXERS
MIX_W = N_MIXERS * GROUP_W
EPS = 1e-6
F32 = jnp.float32
BF16 = jnp.bfloat16
HG_HEADS = 4
HG_DK = GROUP_W // HG_HEADS
HG_DV = GROUP_W // HG_HEADS
HG_CHUNK = 16
S5_CH = 16
S5_GROUPS = GROUP_W // S5_CH
S5_STATE = 64
DA_HEADS = 4
DA_DQK = GROUP_W // (2 * DA_HEADS)
DA_DV = GROUP_W // DA_HEADS
ROPE_BASE = 10000.0
ML_HEADS = 4
ML_DK = GROUP_W // ML_HEADS
ML_DV = GROUP_W // ML_HEADS
ML_CHUNK = 64
N_EXPERTS = 16
EC_CAPACITY = 2
D_EXPERT = 2 * D_MODEL
HG_OFF = 0
S5_OFF = HG_OFF + 5 * GROUP_W
DA_OFF = S5_OFF + GROUP_W
ML_OFF = DA_OFF + 3 * GROUP_W
ML_GATE_OFF = ML_OFF + 4 * GROUP_W
IN_COLS = ML_GATE_OFF + 4 * ML_HEADS

LANES = 128
ROW_TILE = 256
IN_COLS_PAD = ML_GATE_OFF + LANES
N_GATES = 4 * ML_HEADS
S5_LANES = S5_GROUPS * S5_STATE
S5_HALF = S5_LANES // 2
S5_SEGS = 8
VMEM_LIMIT = 56 * 1024 * 1024


def _cparams(*sem):
    return pltpu.CompilerParams(dimension_semantics=sem, vmem_limit_bytes=VMEM_LIMIT)


def _rms(x):
    return x * lax.rsqrt(jnp.mean(x * x, axis=-1, keepdims=True) + EPS)


def _mod_body(cv_ref, w_ref, b_ref, o_ref):
    cv = cv_ref[...]
    s = cv * jax.nn.sigmoid(cv)
    o_ref[0] = jnp.dot(s.astype(BF16), w_ref[0].astype(BF16), preferred_element_type=F32) + b_ref[0]


def _modulation(cv, mod_w, mod_b):
    n_l, d, n6 = mod_w.shape
    r = cv.shape[0]
    tn = n6 // 4
    return pl.pallas_call(
        _mod_body,
        grid=(n_l, 4),
        in_specs=[pl.BlockSpec((r, d), lambda l, j: (0, 0)),
                  pl.BlockSpec((1, d, tn), lambda l, j: (l, 0, j)),
                  pl.BlockSpec((1, 1, tn), lambda l, j: (l, 0, j))],
        out_specs=pl.BlockSpec((1, r, tn), lambda l, j: (l, 0, j)),
        out_shape=jax.ShapeDtypeStruct((n_l, r, n6), F32),
        compiler_params=_cparams("arbitrary", "arbitrary"),
    )(cv, mod_w, mod_b.reshape(n_l, 1, n6))


def _inproj_body(h_ref, m_ref, nw_ref, w_ref, b_ref, wg_ref, bg_ref, z_ref, gt_ref):
    m = m_ref[0, 0]
    xn = _rms(h_ref[0]) * nw_ref[...] * (1.0 + m[1:2]) + m[0:1]
    xb = xn.astype(BF16)
    z_ref[0] = jnp.dot(xb, w_ref[...], preferred_element_type=F32) + b_ref[...]
    gt_ref[0] = lax.dot_general(wg_ref[...], xb, (((1,), (1,)), ((), ())),
                                preferred_element_type=F32) + bg_ref[...]


def _in_projection(h, modsel, norm_w, w_pad, b_pad, wg_t, bg_col):
    bsz, t, d = h.shape
    nt = t // ROW_TILE
    return pl.pallas_call(
        _inproj_body,
        grid=(bsz, nt),
        in_specs=[pl.BlockSpec((1, ROW_TILE, d), lambda b, j: (b, j, 0)),
                  pl.BlockSpec((1, 1, 6, d), lambda b, j: (b, jnp.minimum(j, 1), 0, 0)),
                  pl.BlockSpec((1, d), lambda b, j: (0, 0)),
                  pl.BlockSpec((d, IN_COLS_PAD), lambda b, j: (0, 0)),
                  pl.BlockSpec((1, IN_COLS_PAD), lambda b, j: (0, 0)),
                  pl.BlockSpec((N_GATES, d), lambda b, j: (0, 0)),
                  pl.BlockSpec((N_GATES, 1), lambda b, j: (0, 0))],
        out_specs=[pl.BlockSpec((1, ROW_TILE, IN_COLS_PAD), lambda b, j: (b, j, 0)),
                   pl.BlockSpec((1, N_GATES, ROW_TILE), lambda b, j: (b, 0, j))],
        out_shape=[jax.ShapeDtypeStruct((bsz, t, IN_COLS_PAD), F32),
                   jax.ShapeDtypeStruct((bsz, N_GATES, t), F32)],
        compiler_params=_cparams("arbitrary", "arbitrary"),
    )(h, modsel, norm_w.reshape(1, d), w_pad, b_pad, wg_t, bg_col)


def _outproj_body(a_ref, b_ref, c_ref, d_ref, h_ref, m_ref, nw_ref, wo_ref, rw_ref, hm_ref, xm_ref, aff_ref):
    m = m_ref[0, 0]
    y = jnp.concatenate([a_ref[0], b_ref[0], c_ref[0], d_ref[0]], axis=-1).astype(BF16)
    hm = h_ref[0] + m[2:3] * jnp.dot(y, wo_ref[...], preferred_element_type=F32)
    hm_ref[0] = hm
    xm = (_rms(hm) * nw_ref[...] * (1.0 + m[4:5]) + m[3:4]).astype(BF16)
    xm_ref[0] = xm
    logit = lax.dot_general(rw_ref[...], xm, (((1,), (1,)), ((), ())), preferred_element_type=F32)
    e = jnp.exp(logit - jnp.max(logit, axis=0, keepdims=True))
    aff_ref[0] = e / jnp.sum(e, axis=0, keepdims=True)


def _out_projection(mix, h, modsel, norm_w, wo_bf, rw_t):
    bsz, t, d = h.shape
    nt = t // ROW_TILE
    mix_spec = pl.BlockSpec((1, ROW_TILE, GROUP_W), lambda b, j: (b, j, 0))
    row_spec = pl.BlockSpec((1, ROW_TILE, d), lambda b, j: (b, j, 0))
    return pl.pallas_call(
        _outproj_body,
        grid=(bsz, nt),
        in_specs=[mix_spec, mix_spec, mix_spec, mix_spec, row_spec,
                  pl.BlockSpec((1, 1, 6, d), lambda b, j: (b, jnp.minimum(j, 1), 0, 0)),
                  pl.BlockSpec((1, d), lambda b, j: (0, 0)),
                  pl.BlockSpec((MIX_W, d), lambda b, j: (0, 0)),
                  pl.BlockSpec((N_EXPERTS, d), lambda b, j: (0, 0))],
        out_specs=[row_spec, row_spec, pl.BlockSpec((1, N_EXPERTS, ROW_TILE), lambda b, j: (b, 0, j))],
        out_shape=[jax.ShapeDtypeStruct((bsz, t, d), F32), jax.ShapeDtypeStruct((bsz, t, d), BF16),
                   jax.ShapeDtypeStruct((bsz, N_EXPERTS, t), F32)],
        compiler_params=_cparams("arbitrary", "arbitrary"),
    )(*mix, h, modsel, norm_w.reshape(1, d), wo_bf, rw_t)


def _topk_body(aff_ref, slot_ref, slot_t_ref, *, cap, n, row0):
    aff = aff_ref[0, :, row0:row0 + n]
    bits = pltpu.bitcast(aff, jnp.int32)
    thr = jnp.zeros((N_EXPERTS, 1), jnp.int32)
    for bit in range(30, -1, -1):
        cand = thr | (1 << bit)
        cnt = jnp.sum(jnp.where(bits >= cand, 1.0, 0.0), axis=-1, keepdims=True)
        thr = jnp.where(cnt >= cap, cand, thr)
    room = cap - jnp.sum(jnp.where(bits > thr, 1.0, 0.0), axis=-1, keepdims=True)
    ri = lax.broadcasted_iota(jnp.int32, (LANES, LANES), 0)
    ci = lax.broadcasted_iota(jnp.int32, (LANES, LANES), 1)
    incl = jnp.where(ri <= ci, 1.0, 0.0).astype(BF16)
    off_eq = jnp.zeros((N_EXPERTS, 1), F32)
    off_sel = jnp.zeros((N_EXPERTS, 1), F32)
    pieces = []
    for j in range(n // LANES):
        sl = slice(j * LANES, (j + 1) * LANES)
        bits_b = bits[:, sl]
        eq_b = jnp.where(bits_b == thr, 1.0, 0.0)
        pos_eq = jnp.dot(eq_b.astype(BF16), incl, preferred_element_type=F32) - eq_b + off_eq
        sel = jnp.where(bits_b > thr, 1.0, jnp.where(pos_eq < room, eq_b, 0.0))
        pos_sel = jnp.dot(sel.astype(BF16), incl, preferred_element_type=F32) - sel + off_sel
        piece = jnp.where(sel > 0.5, pos_sel, -1.0)
        slot_ref[0, :, sl] = piece.astype(jnp.int32)
        pieces.append(piece)
        off_eq = off_eq + jnp.sum(eq_b, axis=-1, keepdims=True)
        off_sel = off_sel + jnp.sum(sel, axis=-1, keepdims=True)
    pad = jnp.full((LANES - N_EXPERTS, LANES), -1.0, F32)
    for j in range(n // LANES):
        tile = jnp.concatenate([pieces[j], pad], axis=0)
        slot_t_ref[0, j * LANES:(j + 1) * LANES, :] = tile.T.astype(jnp.int32)


def _route(aff_t, row0, n):
    bsz, _, t = aff_t.shape
    cap = EC_CAPACITY * n // N_EXPERTS
    return pl.pallas_call(
        functools.partial(_topk_body, cap=cap, n=n, row0=row0),
        grid=(bsz,),
        in_specs=[pl.BlockSpec((1, N_EXPERTS, t), lambda b: (b, 0, 0))],
        out_specs=[pl.BlockSpec((1, N_EXPERTS, n), lambda b: (b, 0, 0)),
                   pl.BlockSpec((1, n, LANES), lambda b: (b, 0, 0))],
        out_shape=[jax.ShapeDtypeStruct((bsz, N_EXPERTS, n), jnp.int32),
                   jax.ShapeDtypeStruct((bsz, n, LANES), jnp.int32)],
        compiler_params=_cparams("arbitrary"),
    )(aff_t)


def _moe_ffn_body(x_ref, slot_ref, aff_ref, w1_ref, w3_ref, w2_ref, yh_ref, yl_ref, xe_scr, g_scr, *, nb, cap, n, off):
    for i in range(nb):
        hit = lax.broadcasted_iota(jnp.int32, (cap, n), 0) == slot_ref[i, 0]
        onehot = jnp.where(hit, 1.0, 0.0).astype(BF16)
        xe_scr[i * cap:(i + 1) * cap, :] = jnp.dot(onehot, x_ref[i, off:off + n, :],
                                                   preferred_element_type=F32).astype(BF16)
        g_scr[i * cap:(i + 1) * cap, :] = jnp.sum(jnp.where(hit, aff_ref[i, 0, :, off:off + n], 0.0),
                                                  axis=-1, keepdims=True)
    xe = xe_scr[...]
    h1 = jnp.dot(xe, w1_ref[0], preferred_element_type=F32)
    h3 = jnp.dot(xe, w3_ref[0], preferred_element_type=F32)
    act = (h1 * jax.nn.sigmoid(h1) * h3).astype(BF16)
    y = jnp.dot(act, w2_ref[0], preferred_element_type=F32) * g_scr[...]
    y_hi = y.astype(BF16)
    y_lo = (y - y_hi.astype(F32)).astype(BF16)
    for i in range(nb):
        yh_ref[i, 0] = y_hi[i * cap:(i + 1) * cap]
        yl_ref[i, 0] = y_lo[i * cap:(i + 1) * cap]


def _moe_ffn(xm, slot, aff_t, row0, n, nb, w1, w3, w2):
    bsz, t, d = xm.shape
    cap = EC_CAPACITY * n // N_EXPERTS
    rows, off = (n, 0) if row0 % n == 0 else (t, row0)
    blk = row0 // n if off == 0 else 0
    slot4 = slot.reshape(bsz, N_EXPERTS, 1, n)
    aff4 = aff_t.reshape(bsz, N_EXPERTS, 1, t)
    y_spec = pl.BlockSpec((nb, 1, cap, d), lambda e, b: (b, e, 0, 0))
    y_shape = jax.ShapeDtypeStruct((bsz, N_EXPERTS, cap, d), BF16)
    return pl.pallas_call(
        functools.partial(_moe_ffn_body, nb=nb, cap=cap, n=n, off=off),
        grid=(N_EXPERTS, bsz // nb),
        in_specs=[pl.BlockSpec((nb, rows, d), lambda e, b: (b, blk, 0)),
                  pl.BlockSpec((nb, 1, 1, n), lambda e, b: (b, e, 0, 0)),
                  pl.BlockSpec((nb, 1, 1, rows), lambda e, b: (b, e, 0, blk)),
                  pl.BlockSpec((1, d, D_EXPERT), lambda e, b: (e, 0, 0)),
                  pl.BlockSpec((1, d, D_EXPERT), lambda e, b: (e, 0, 0)),
                  pl.BlockSpec((1, D_EXPERT, d), lambda e, b: (e, 0, 0))],
        out_specs=[y_spec, y_spec],
        out_shape=[y_shape, y_shape],
        scratch_shapes=[pltpu.VMEM((nb * cap, d), BF16), pltpu.VMEM((nb * cap, 1), F32)],
        compiler_params=_cparams("arbitrary", "arbitrary"),
    )(xm, slot4, aff4, w1, w3, w2)


def _moe_combine_body(yh_ref, yl_ref, st_ref, hm_ref, m_ref, o_ref, *, cap):
    st = st_ref[0]
    lane = lax.broadcasted_iota(jnp.int32, (st.shape[0], cap), 1)
    acc = jnp.zeros(o_ref.shape[1:], F32)
    for e in range(N_EXPERTS):
        onehot = jnp.where(lane == st[:, e:e + 1], 1.0, 0.0).astype(BF16)
        acc = acc + jnp.dot(onehot, yh_ref[0, e], preferred_element_type=F32)
        acc = acc + jnp.dot(onehot, yl_ref[0, e], preferred_element_type=F32)
    o_ref[0] = hm_ref[0] + m_ref[0, 0][5:6] * acc


def _moe_combine(y_hi, y_lo, slot_t, hm, modsel, row0, n, is_latent):
    bsz, _, cap, d = y_hi.shape
    nt = n // ROW_TILE
    blk0 = row0 // ROW_TILE
    y_spec = pl.BlockSpec((1, N_EXPERTS, cap, d), lambda b, j: (b, 0, 0, 0))
    return pl.pallas_call(
        functools.partial(_moe_combine_body, cap=cap),
        grid=(bsz, nt),
        in_specs=[y_spec, y_spec,
                  pl.BlockSpec((1, ROW_TILE, LANES), lambda b, j: (b, j, 0)),
                  pl.BlockSpec((1, ROW_TILE, d), lambda b, j: (b, blk0 + j, 0)),
                  pl.BlockSpec((1, 1, 6, d), lambda b, j: (b, is_latent, 0, 0))],
        out_specs=pl.BlockSpec((1, ROW_TILE, d), lambda b, j: (b, j, 0)),
        out_shape=jax.ShapeDtypeStruct((bsz, n, d), F32),
        compiler_params=_cparams("arbitrary", "arbitrary"),
    )(y_hi, y_lo, slot_t, hm, modsel)


def _swap_pairs(x):
    w = x.shape[-1]
    lane = lax.broadcasted_iota(jnp.int32, x.shape, x.ndim - 1)
    return jnp.where(lane % 2 == 0, pltpu.roll(x, w - 1, x.ndim - 1), pltpu.roll(x, 1, x.ndim - 1))


def _attn_body(q_ref, k_ref, v_ref, cq_ref, sq_ref, ck_ref, sk_ref, lq1_ref, lk1_ref, lq2_ref, lk2_ref, nw_ref,
               o_ref, k_scr, vt_scr, *, rope, lam_init):
    @pl.when(pl.program_id(1) == 0)
    def _():
        k = k_ref[0]
        if rope:
            k = k * ck_ref[...] + _swap_pairs(k) * sk_ref[...]
        k_scr[...] = k.astype(BF16)
        for r in range(0, k_scr.shape[0], ROW_TILE):
            vt_scr[:, r:r + ROW_TILE] = v_ref[0, r:r + ROW_TILE, :].T.astype(BF16)

    q = q_ref[0]
    if rope:
        q = q * cq_ref[...] + _swap_pairs(q) * sq_ref[...]
    q_t = (q * (DA_DQK ** -0.5)).T
    lam = (jnp.exp(jnp.sum(lq1_ref[...] * lk1_ref[...], axis=-1, keepdims=True))
           - jnp.exp(jnp.sum(lq2_ref[...] * lk2_ref[...], axis=-1, keepdims=True)) + lam_init)
    row = lax.broadcasted_iota(jnp.int32, (GROUP_W, 1), 0)
    kk = k_scr[...]
    outs = []
    for h in range(DA_HEADS):
        es, rs = [], []
        for mi in range(2):
            qm = jnp.where(row // DA_DQK == 2 * h + mi, q_t, 0.0).astype(BF16)
            s = jnp.dot(kk, qm, preferred_element_type=F32)
            e = jnp.exp(s - jnp.max(s, axis=0, keepdims=True))
            es.append(e)
            rs.append(1.0 / jnp.sum(e, axis=0, keepdims=True))
        a = (es[0] * rs[0] - es[1] * (lam * rs[1])).astype(BF16)
        oh = jnp.dot(vt_scr[h * DA_DV:(h + 1) * DA_DV, :], a, preferred_element_type=F32)
        outs.append(oh * lax.rsqrt(jnp.mean(oh * oh, axis=0, keepdims=True) + EPS))
    o_ref[0] = jnp.concatenate(outs, axis=0).T * nw_ref[...] * (1.0 - lam_init)


def _diff_attention(z, q_row0, nq, nk, rope, cos_t, sin_t, lq1, lk1, lq2, lk2, norm_w, lam_init):
    bsz = z.shape[0]
    qb = ROW_TILE
    q0 = q_row0 // qb
    col = DA_OFF // GROUP_W
    vec = lambda a: a.reshape(1, -1)
    small = pl.BlockSpec((1, DA_DQK), lambda b, j: (0, 0))
    return pl.pallas_call(
        functools.partial(_attn_body, rope=rope, lam_init=lam_init),
        grid=(bsz, nq // qb),
        in_specs=[pl.BlockSpec((1, qb, GROUP_W), lambda b, j: (b, q0 + j, col)),
                  pl.BlockSpec((1, nk, GROUP_W), lambda b, j: (b, 0, col + 1)),
                  pl.BlockSpec((1, nk, GROUP_W), lambda b, j: (b, 0, col + 2)),
                  pl.BlockSpec((qb, GROUP_W), lambda b, j: (q0 + j, 0)),
                  pl.BlockSpec((qb, GROUP_W), lambda b, j: (q0 + j, 0)),
                  pl.BlockSpec((nk, GROUP_W), lambda b, j: (0, 0)),
                  pl.BlockSpec((nk, GROUP_W), lambda b, j: (0, 0)),
                  small, small, small, small,
                  pl.BlockSpec((1, GROUP_W), lambda b, j: (0, 0))],
        out_specs=pl.BlockSpec((1, qb, GROUP_W), lambda b, j: (b, j, 0)),
        out_shape=jax.ShapeDtypeStruct((bsz, nq, GROUP_W), F32),
        scratch_shapes=[pltpu.VMEM((nk, GROUP_W), BF16), pltpu.VMEM((GROUP_W, nk), BF16)],
        compiler_params=_cparams("arbitrary", "arbitrary"),
    )(z, z, z, cos_t, sin_t, cos_t, sin_t, vec(lq1), vec(lk1), vec(lq2), vec(lk2), vec(norm_w))


def _rope_tables(t):
    n = t - CTX_LEN
    axis_dim = DA_DQK // 2
    inv = ROPE_BASE ** (-jnp.arange(0, axis_dim, 2, dtype=F32) / axis_dim)
    tok = jnp.arange(n, dtype=jnp.int32)
    row = (tok // GRID_W).astype(F32)
    colp = (tok % GRID_W).astype(F32)
    ang = jnp.concatenate([row[:, None] * inv, colp[:, None] * inv], axis=-1)
    cos = jnp.repeat(jnp.cos(ang), 2, axis=-1)
    sin = jnp.repeat(jnp.sin(ang), 2, axis=-1) * jnp.tile(jnp.array([-1.0, 1.0], F32), DA_DQK // 2)
    reps = GROUP_W // DA_DQK
    cos = jnp.concatenate([jnp.ones((CTX_LEN, GROUP_W), F32), jnp.tile(cos, (1, reps))], axis=0)
    sin = jnp.concatenate([jnp.zeros((CTX_LEN, GROUP_W), F32), jnp.tile(sin, (1, reps))], axis=0)
    return cos, sin


def _cmul(ar, ai, br, bi):
    return ar * br - ai * bi, ar * bi + ai * br


def _s5_scan_part(bufs, coef, base, seg_len, carry):
    xrf, xif, xrb, xib = bufs
    af_r, af_i, ab_r, ab_i = coef
    assert seg_len & (seg_len - 1) == 0

    def sweep(store, init):
        def step(i, st):
            fr, fi, br, bi = st
            rf = pl.ds(pl.multiple_of(base + S5_SEGS * i, S5_SEGS), S5_SEGS)
            rb = pl.ds(pl.multiple_of(base + S5_SEGS * (seg_len - 1 - i), S5_SEGS), S5_SEGS)
            nfr = af_r * fr - af_i * fi + xrf[rf, :]
            nfi = af_r * fi + af_i * fr + xif[rf, :]
            nbr = ab_r * br - ab_i * bi + xrb[rb, :]
            nbi = ab_r * bi + ab_i * br + xib[rb, :]
            if store:
                xrf[rf, :] = nfr
                xif[rf, :] = nfi
                xrb[rb, :] = nbr
                xib[rb, :] = nbi
            return nfr, nfi, nbr, nbi
        return lax.fori_loop(0, seg_len, step, init, unroll=4)

    zero = jnp.zeros(af_r.shape, F32)
    ef_r, ef_i, eb_r, eb_i = sweep(False, (zero,) * 4)
    pf = (af_r[0:1], af_i[0:1])
    pb = (ab_r[0:1], ab_i[0:1])
    for _ in range(seg_len.bit_length() - 1):
        pf = _cmul(*pf, *pf)
        pb = _cmul(*pb, *pb)
    cf_r, cf_i, cb_r, cb_i = carry
    ins_f, ins_b = [], []
    for j in range(S5_SEGS):
        ins_f.append((cf_r, cf_i))
        gr, gi = _cmul(*pf, cf_r, cf_i)
        cf_r, cf_i = ef_r[j:j + 1] + gr, ef_i[j:j + 1] + gi
    for j in range(S5_SEGS - 1, -1, -1):
        ins_b.append((cb_r, cb_i))
        gr, gi = _cmul(*pb, cb_r, cb_i)
        cb_r, cb_i = eb_r[j:j + 1] + gr, eb_i[j:j + 1] + gi
    ins_b.reverse()
    stack = lambda rows: jnp.concatenate(rows, axis=0)
    sweep(True, (stack([a for a, _ in ins_f]), stack([b for _, b in ins_f]),
                 stack([a for a, _ in ins_b]), stack([b for _, b in ins_b])))
    return cf_r, cf_i, cb_r, cb_i


def _seg_interleave(x):
    b, r, c = x.shape
    return x.reshape(b, S5_SEGS, r // S5_SEGS, c).transpose(0, 2, 1, 3).reshape(b, r, c)


def _seg_deinterleave(x):
    b, r, c = x.shape
    return x.reshape(b, r // S5_SEGS, S5_SEGS, c).transpose(0, 2, 1, 3).reshape(b, r, c)


def _s5_body(u_ref, bre_ref, bim_ref, cre_ref, cim_ref, disc_ref, dsk_ref, gw_ref, gb_ref, o_ref,
             xrf, xif, xrb, xib, y_scr, *, t_len):
    u = u_ref[0]
    ub = u.astype(BF16)
    y_scr[...] = u * dsk_ref[...]
    n_lat = t_len - CTX_LEN
    for half in range(2):
        ls = slice(half * S5_HALF, (half + 1) * S5_HALF)
        bu_r = jnp.dot(ub, bre_ref[:, ls], preferred_element_type=F32)
        bu_i = jnp.dot(ub, bim_ref[:, ls], preferred_element_type=F32)
        disc = disc_ref[:, ls]
        xrf[...] = disc[2:3] * bu_r - disc[3:4] * bu_i
        xif[...] = disc[2:3] * bu_i + disc[3:4] * bu_r
        xrb[...] = disc[6:7] * bu_r - disc[7:8] * bu_i
        xib[...] = disc[6:7] * bu_i + disc[7:8] * bu_r
        coef = tuple(jnp.broadcast_to(disc[r:r + 1], (S5_SEGS, S5_HALF)) for r in (0, 1, 4, 5))
        zero = jnp.zeros((1, S5_HALF), F32)
        carry = _s5_scan_part((xrf, xif, xrb, xib), coef, 0, CTX_LEN // S5_SEGS, (zero,) * 4)
        _s5_scan_part((xrf, xif, xrb, xib), coef, CTX_LEN, n_lat // S5_SEGS, carry)
        cre = cre_ref[ls, :]
        cim = cim_ref[ls, :]
        y_scr[...] += (jnp.dot(xrf[...].astype(BF16), cre, preferred_element_type=F32)
                       - jnp.dot(xif[...].astype(BF16), cim, preferred_element_type=F32)
                       + jnp.dot(xrb[...].astype(BF16), cre, preferred_element_type=F32)
                       - jnp.dot(xib[...].astype(BF16), cim, preferred_element_type=F32))
    y = jax.nn.gelu(y_scr[...])
    gate = jax.nn.sigmoid(jnp.dot(y.astype(BF16), gw_ref[...], preferred_element_type=F32) + gb_ref[...])
    o_ref[0] = y * gate


def _s5_mixer(z, lam_re, lam_im, log_dt, b_re, b_im, c_re, c_im, d_skip, glu_w, glu_b):
    bsz, t, _ = z.shape
    eye = jnp.eye(S5_GROUPS, dtype=F32)
    bbd = lambda w: jnp.einsum('gpc,gh->gchp', w.astype(F32), eye).reshape(GROUP_W, S5_LANES).astype(BF16)
    cbd = lambda w: jnp.einsum('gcp,gh->gphc', w.astype(F32), eye).reshape(S5_LANES, GROUP_W).astype(BF16)
    rows = []
    for d in range(2):
        lr, li = lam_re[d].astype(F32), lam_im[d].astype(F32)
        dt = jnp.exp(log_dt[d].astype(F32))[:, None]
        mag = jnp.exp(lr * dt)
        ab_re, ab_im = mag * jnp.cos(li * dt), mag * jnp.sin(li * dt)
        den = lr * lr + li * li
        co_re = ((ab_re - 1.0) * lr + ab_im * li) / den
        co_im = (ab_im * lr - (ab_re - 1.0) * li) / den
        rows += [ab_re, ab_im, co_re, co_im]
    disc = jnp.stack([r.reshape(S5_LANES) for r in rows], axis=0)
    full = lambda shape: pl.BlockSpec(shape, lambda b: (0,) * len(shape))
    u = z[:, :, S5_OFF:S5_OFF + GROUP_W]
    u = jnp.concatenate([_seg_interleave(u[:, :CTX_LEN]), _seg_interleave(u[:, CTX_LEN:])], axis=1)
    out = pl.pallas_call(
        functools.partial(_s5_body, t_len=t),
        grid=(bsz,),
        in_specs=[pl.BlockSpec((1, t, GROUP_W), lambda b: (b, 0, 0)),
                  full((GROUP_W, S5_LANES)), full((GROUP_W, S5_LANES)),
                  full((S5_LANES, GROUP_W)), full((S5_LANES, GROUP_W)),
                  full((8, S5_LANES)), full((1, GROUP_W)), full((GROUP_W, GROUP_W)), full((1, GROUP_W))],
        out_specs=pl.BlockSpec((1, t, GROUP_W), lambda b: (b, 0, 0)),
        out_shape=jax.ShapeDtypeStruct((bsz, t, GROUP_W), F32),
        scratch_shapes=[pltpu.VMEM((t, S5_HALF), F32)] * 4 + [pltpu.VMEM((t, GROUP_W), F32)],
        compiler_params=_cparams("arbitrary"),
    )(u, bbd(b_re), bbd(b_im), cbd(c_re), cbd(c_im), disc, d_skip.astype(F32).reshape(1, GROUP_W),
      glu_w.astype(BF16), glu_b.astype(F32).reshape(1, GROUP_W))
    return jnp.concatenate([_seg_deinterleave(out[:, :CTX_LEN]), _seg_deinterleave(out[:, CTX_LEN:])], axis=1)


def rms_norm(x, w):
    xf = x.astype(F32)
    y = xf * lax.rsqrt(jnp.mean(xf * xf, axis=-1, keepdims=True) + EPS)
    return (y * w.astype(F32)).astype(x.dtype)


def head_rms_norm(x, w, n_heads):
    shp = x.shape
    xh = x.reshape(shp[:-1] + (n_heads, -1))
    return rms_norm(xh, w.reshape(n_heads, -1)).reshape(shp)


def flip_seq(a, reverse):
    return a[:, ::-1] if reverse else a


def gla_chunked(q, k, v, log_f, s0, with_out=True):
    bsz, t_len, nh, _ = q.shape
    nc = t_len // HG_CHUNK
    def chunks(a):
        return a.astype(F32).reshape(bsz, nc, HG_CHUNK, nh, a.shape[-1])
    q, k, v, log_f = chunks(q), chunks(k), chunks(v), chunks(log_f)
    b = jnp.cumsum(log_f, axis=2)
    b_end = b[:, :, -1]
    ds = jnp.einsum('bcshd,bcshv->bchdv', k * jnp.exp(b_end[:, :, None] - b), v)
    def step(s, inp):
        g_c, ds_c = inp
        return jnp.exp(g_c)[..., None] * s + ds_c, s
    s_fin, s_start = lax.scan(step, s0, (jnp.moveaxis(b_end, 1, 0), jnp.moveaxis(ds, 1, 0)))
    if not with_out:
        return None, s_fin
    s_start = jnp.moveaxis(s_start, 0, 1)
    tri = jnp.tril(jnp.ones((HG_CHUNK, HG_CHUNK), bool))[None, None, :, :, None, None]
    decay = jnp.exp(jnp.where(tri, b[:, :, :, None] - b[:, :, None], -jnp.inf))
    scores = jnp.sum(q[:, :, :, None] * decay * k[:, :, None], axis=-1)
    o = (jnp.einsum('bctsh,bcshv->bcthv', scores, v)
         + jnp.einsum('bcthd,bchdv->bcthv', q * jnp.exp(b), s_start))
    return o.reshape(bsz, t_len, nh, -1), s_fin


def hgrn2_mixer(zc, zl, lb, norm_w, with_ctx_out):
    def heads(a):
        return a.reshape(a.shape[:2] + (HG_HEADS, -1))
    def split(z):
        q, i, ff, fb, g = jnp.split(z, 5, axis=-1)
        return heads(q) * HG_DK ** -0.5, heads(i), (heads(ff), heads(fb)), g
    qc, ic, fc, gc = split(zc)
    ql, il, fl, gl = split(zl)
    oc, ol = 0.0, 0.0
    for d in range(2):
        lbd = lb[d].reshape(HG_HEADS, HG_DK)
        f_c = lbd + (1.0 - lbd) * jax.nn.sigmoid(fc[d].astype(F32))
        f_l = lbd + (1.0 - lbd) * jax.nn.sigmoid(fl[d].astype(F32))
        s0 = jnp.zeros((zc.shape[0], HG_HEADS, HG_DK, HG_DV), F32)
        o_c, s_c = gla_chunked(flip_seq(qc, d), flip_seq(1.0 - f_c, d), flip_seq(ic, d),
                               flip_seq(jnp.log(f_c), d), s0, with_ctx_out)
        o_l, _ = gla_chunked(flip_seq(ql, d), flip_seq(1.0 - f_l, d), flip_seq(il, d),
                             flip_seq(jnp.log(f_l), d), s_c)
        ol = ol + flip_seq(o_l, d)
        if with_ctx_out:
            oc = oc + flip_seq(o_c, d)
    def readout(o, g):
        o = o.reshape(o.shape[:2] + (-1,)).astype(g.dtype)
        return head_rms_norm(o, norm_w, HG_HEADS) * jax.nn.silu(g)
    return (readout(oc, gc) if with_ctx_out else None), readout(ol, gl)


def mlstm_chunked(q, k, v, log_i, log_f, state, with_out=True):
    bsz, t_len, nh, _ = q.shape
    nc = t_len // ML_CHUNK
    def chunks(a):
        return a.astype(F32).reshape((bsz, nc, ML_CHUNK) + a.shape[2:])
    q, k, v, log_i, log_f = chunks(q), chunks(k), chunks(v), chunks(log_i), chunks(log_f)
    b = jnp.cumsum(log_f, axis=2)
    b_end = b[:, :, -1]
    w_end = b_end[:, :, None] - b + log_i
    m_loc = jnp.max(w_end, axis=2)
    e_end = jnp.exp(w_end - m_loc[:, :, None])
    c_loc = jnp.einsum('bcshv,bcshd->bchvd', e_end[..., None] * v, k)
    n_loc = jnp.einsum('bcsh,bcshd->bchd', e_end, k)
    def step(carry, inp):
        c_st, n_st, m_st = carry
        be, ml, cl, nl = inp
        m_new = jnp.maximum(be + m_st, ml)
        a_old, a_loc = jnp.exp(be + m_st - m_new), jnp.exp(ml - m_new)
        c_new = a_old[..., None, None] * c_st + a_loc[..., None, None] * cl
        n_new = a_old[..., None] * n_st + a_loc[..., None] * nl
        return (c_new, n_new, m_new), carry
    mv = lambda a: jnp.moveaxis(a, 1, 0)
    final, starts = lax.scan(step, state, (mv(b_end), mv(m_loc), mv(c_loc), mv(n_loc)))
    if not with_out:
        return None, final
    c_s, n_s, m_s = (jnp.moveaxis(a, 0, 1) for a in starts)
    tri = jnp.tril(jnp.ones((ML_CHUNK, ML_CHUNK), bool))[None, None, :, :, None]
    d_log = jnp.where(tri, b[:, :, :, None] - b[:, :, None] + log_i[:, :, None], -jnp.inf)
    w_inter = b + m_s[:, :, None]
    m_t = jnp.maximum(w_inter, jnp.max(d_log, axis=3))
    e_inter = jnp.exp(w_inter - m_t)
    wts = jnp.exp(d_log - m_t[:, :, :, None]) * jnp.einsum('bcthd,bcshd->bctsh', q, k)
    num = (jnp.einsum('bctsh,bcshv->bcthv', wts, v)
           + e_inter[..., None] * jnp.einsum('bcthd,bchvd->bcthv', q, c_s))
    den = jnp.sum(wts, axis=3) + e_inter * jnp.einsum('bcthd,bchd->bcth', q, n_s)
    h = num / jnp.maximum(jnp.abs(den), jnp.exp(-m_t))[..., None]
    return h.reshape(bsz, t_len, nh, -1), final


def mlstm_mixer(zc, zl, norm_w, with_ctx_out):
    def split(z):
        shp = z.shape[:2]
        q, k, v, o = (z[..., j * GROUP_W:(j + 1) * GROUP_W] for j in range(4))
        gates = z[..., 4 * GROUP_W:].astype(F32).reshape(shp + (4, ML_HEADS))
        hd = lambda a: a.reshape(shp + (ML_HEADS, -1))
        return hd(q), hd(k) * ML_DK ** -0.5, hd(v), o, gates
    qc, kc, vc, oc, gtc = split(zc)
    ql, kl, vl, ol, gtl = split(zl)
    bsz = zc.shape[0]
    hc, hl = 0.0, 0.0
    for d in range(2):
        state0 = (jnp.zeros((bsz, ML_HEADS, ML_DV, ML_DK), F32), jnp.zeros((bsz, ML_HEADS, ML_DK), F32),
                  jnp.zeros((bsz, ML_HEADS), F32))
        h_c, st_c = mlstm_chunked(flip_seq(qc, d), flip_seq(kc, d), flip_seq(vc, d), flip_seq(gtc[:, :, d], d),
                                  flip_seq(jax.nn.log_sigmoid(gtc[:, :, 2 + d]), d), state0, with_ctx_out)
        h_l, _ = mlstm_chunked(flip_seq(ql, d), flip_seq(kl, d), flip_seq(vl, d), flip_seq(gtl[:, :, d], d),
                               flip_seq(jax.nn.log_sigmoid(gtl[:, :, 2 + d]), d), st_c)
        hl = hl + flip_seq(h_l, d)
        if with_ctx_out:
            hc = hc + flip_seq(h_c, d)
    def readout(h, o):
        h = h.reshape(h.shape[:2] + (-1,)).astype(o.dtype)
        return head_rms_norm(h, norm_w, ML_HEADS) * jax.nn.sigmoid(o)
    return (readout(hc, oc) if with_ctx_out else None), readout(hl, ol)


def _split3(x):
    hi = x.astype(BF16)
    r = x - hi.astype(F32)
    mid = r.astype(BF16)
    return hi, mid, (r - mid.astype(F32)).astype(BF16)


def _dot3(x, sel):
    return sum(jnp.dot(p, sel, preferred_element_type=F32) for p in _split3(x))


def _mdot3(sel, x):
    return sum(jnp.dot(sel, p, preferred_element_type=F32) for p in _split3(x))


def _dot2(x, sel):
    hi = x.astype(BF16)
    lo = (x - hi.astype(F32)).astype(BF16)
    return jnp.dot(hi, sel, preferred_element_type=F32) + jnp.dot(lo, sel, preferred_element_type=F32)


def _head_blocks(n, seg):
    ri = lax.broadcasted_iota(jnp.int32, (n, n), 0)
    ci = lax.broadcasted_iota(jnp.int32, (n, n), 1)
    return ri // seg == ci // seg, ri, ci


def _chunk_cumsum_cols(x, chunk, reverse):
    same, ri, ci = _head_blocks(ROW_TILE, chunk)
    tri = jnp.where(same & ((ci >= ri) if reverse else (ci <= ri)), 1.0, 0.0).astype(BF16)
    return [_mdot3(tri, x[r:r + ROW_TILE]) for r in range(0, x.shape[0], ROW_TILE)]


def _bwd_chunk(i, n_ctx, n_all):
    return jnp.where(i < n_ctx, n_ctx - 1 - i, n_all + n_ctx - 1 - i)


def _head_rms(x, ebd, seg):
    return x * lax.rsqrt(_dot2(x * x, ebd) * (1.0 / seg) + EPS)


def _hgrn2_body(z_ref, lb_ref, nw_ref, out_ref, bsc, ksc, osc, st, *, t_len):
    for d in range(2):
        lb = lb_ref[d:d + 1, :]
        f = lb + (1.0 - lb) * jax.nn.sigmoid(z_ref[0, :, (2 + d) * GROUP_W:(3 + d) * GROUP_W])
        ksc[d] = 1.0 - f
        for j, blk in enumerate(_chunk_cumsum_cols(jnp.log(f), HG_CHUNK, d == 1)):
            bsc[d, j * ROW_TILE:(j + 1) * ROW_TILE, :] = blk
    st[...] = jnp.zeros(st.shape, F32)
    bd, _, _ = _head_blocks(GROUP_W, HG_DK)
    ebd = jnp.where(bd, 1.0, 0.0).astype(BF16)
    t3 = lax.broadcasted_iota(jnp.int32, (HG_CHUNK, HG_CHUNK, GROUP_W), 0)
    s3 = lax.broadcasted_iota(jnp.int32, (HG_CHUNK, HG_CHUNK, GROUP_W), 1)
    n_all = t_len // HG_CHUNK
    n_ctx = CTX_LEN // HG_CHUNK

    def chunk(d, c):
        r0 = pl.multiple_of(c * HG_CHUNK, HG_CHUNK)
        rows = pl.ds(r0, HG_CHUNK)
        q = z_ref[0, rows, 0:GROUP_W] * (HG_DK ** -0.5)
        v = z_ref[0, rows, GROUP_W:2 * GROUP_W]
        k = ksc[d, rows, :]
        b = bsc[d, rows, :]
        b_end = b[0:1] if d else b[HG_CHUNK - 1:HG_CHUNK]
        valid = (s3 >= t3) if d else (s3 <= t3)
        a = jnp.where(valid, jnp.exp(b[:, None, :] - b[None, :, :]), 0.0) * q[:, None, :] * k[None, :, :]
        sc = _dot2(a.reshape(HG_CHUNK * HG_CHUNK, GROUP_W), ebd)
        o_intra = jnp.sum(sc.reshape(HG_CHUNK, HG_CHUNK, GROUP_W) * v[None, :, :], axis=1)
        s_prev = st[d]
        o_inter = lax.dot_general((q * jnp.exp(b)).astype(BF16), s_prev.astype(BF16), (((1,), (1,)), ((), ())),
                                  preferred_element_type=F32)
        osc[d, rows, :] = o_intra + o_inter
        kd = (k * jnp.exp(b_end - b)).astype(BF16)
        upd = lax.dot_general(v.astype(BF16), kd, (((0,), (0,)), ((), ())), preferred_element_type=F32)
        st[d] = s_prev * jnp.exp(b_end) + jnp.where(bd, upd, 0.0)

    def step(i, carry):
        chunk(0, i)
        chunk(1, _bwd_chunk(i, n_ctx, n_all))
        return carry

    lax.fori_loop(0, n_all, step, 0)
    o = osc[0] + osc[1]
    g = z_ref[0, :, 4 * GROUP_W:5 * GROUP_W]
    out_ref[0] = _head_rms(o, ebd, HG_DV) * nw_ref[...] * (g * jax.nn.sigmoid(g))


def _hgrn2_mixer(z, lb, norm_w):
    bsz, t, _ = z.shape
    return pl.pallas_call(
        functools.partial(_hgrn2_body, t_len=t),
        grid=(bsz,),
        in_specs=[pl.BlockSpec((1, t, 5 * GROUP_W), lambda b: (b, 0, 0)),
                  pl.BlockSpec((2, GROUP_W), lambda b: (0, 0)),
                  pl.BlockSpec((1, GROUP_W), lambda b: (0, 0))],
        out_specs=pl.BlockSpec((1, t, GROUP_W), lambda b: (b, 0, 0)),
        out_shape=jax.ShapeDtypeStruct((bsz, t, GROUP_W), F32),
        scratch_shapes=[pltpu.VMEM((2, t, GROUP_W), F32), pltpu.VMEM((2, t, GROUP_W), F32),
                        pltpu.VMEM((2, t, GROUP_W), F32), pltpu.VMEM((2, GROUP_W, GROUP_W), F32)],
        compiler_params=_cparams("arbitrary"),
    )(z, lb.astype(F32), norm_w.astype(F32).reshape(1, GROUP_W))


def _lane_expand(cols, seg):
    lane = lax.broadcasted_iota(jnp.int32, (1, len(cols) * seg), 1)
    out = cols[-1]
    for h in range(len(cols) - 2, -1, -1):
        out = jnp.where(lane < (h + 1) * seg, cols[h], out)
    return out


def _mlstm_body(q_ref, k_ref, v_ref, o_ref, zg_ref, gr_ref, nw_ref, out_ref,
                bce, lice, brow, hsc, cst, nst, *, t_len):
    n_all = t_len // ML_CHUNK
    n_ctx = CTX_LEN // ML_CHUNK
    g = zg_ref[0]
    lf = jax.nn.log_sigmoid(g)
    src = lax.broadcasted_iota(jnp.int32, (LANES, GROUP_W), 0)
    dst_head = lax.broadcasted_iota(jnp.int32, (LANES, GROUP_W), 1) // ML_DK
    bd, _, _ = _head_blocks(GROUP_W, ML_DK)
    ebd = jnp.where(bd, 1.0, 0.0).astype(BF16)
    for d in range(2):
        lice[d] = _dot3(g, jnp.where(src == ML_HEADS * d + dst_head, 1.0, 0.0).astype(BF16))
        lf_exp = _dot3(lf, jnp.where(src == ML_HEADS * (2 + d) + dst_head, 1.0, 0.0).astype(BF16))
        for j, blk in enumerate(_chunk_cumsum_cols(lf_exp, ML_CHUNK, d == 1)):
            bce[d, j * ROW_TILE:(j + 1) * ROW_TILE, :] = blk
        _, ri, ci = _head_blocks(GROUP_W, ML_CHUNK)
        tri = jnp.where(bd & ((ri >= ci) if d else (ri <= ci)), 1.0, 0.0).astype(BF16)
        brow[d] = _dot3(jax.nn.log_sigmoid(gr_ref[0, 2 + d]), tri)
    cst[...] = jnp.zeros(cst.shape, F32)
    nst[...] = jnp.zeros(nst.shape, F32)
    tt = lax.broadcasted_iota(jnp.int32, (ML_CHUNK, GROUP_W), 0)
    ss = lax.broadcasted_iota(jnp.int32, (ML_CHUNK, GROUP_W), 1) % ML_CHUNK
    lane_h = lax.broadcasted_iota(jnp.int32, (1, LANES), 1) < ML_CHUNK

    def chunk(d, c, m_prev):
        rows = pl.ds(pl.multiple_of(c * ML_CHUNK, ML_CHUNK), ML_CHUNK)
        q = q_ref[0, rows, :]
        k = k_ref[0, rows, :] * (ML_DK ** -0.5)
        v = v_ref[0, rows, :]
        bc = bce[d, rows, :]
        lic = lice[d, rows, :]
        br = brow[d, pl.ds(c, 1), :]
        lir = gr_ref[0, d, pl.ds(c, 1), :]
        b_end = bc[0:1] if d else bc[ML_CHUNK - 1:ML_CHUNK]
        valid = (ss >= tt) if d else (ss <= tt)
        d_log = jnp.where(valid, bc - br + lir, -jnp.inf)
        cols = []
        for hp in range(GROUP_W // LANES):
            xs = d_log[:, hp * LANES:(hp + 1) * LANES]
            cols.append(jnp.max(jnp.where(lane_h, xs, -jnp.inf), axis=-1, keepdims=True))
            cols.append(jnp.max(jnp.where(lane_h, -jnp.inf, xs), axis=-1, keepdims=True))
        w_inter = bc + m_prev
        m_t = jnp.maximum(w_inter, _lane_expand(cols, ML_CHUNK))
        e_inter = jnp.exp(w_inter - m_t)
        qb = q.astype(BF16)
        kbd = jnp.where(bd, jnp.concatenate([k] * ML_HEADS, axis=0).T, 0.0).astype(BF16)
        vbd = jnp.where(bd, jnp.concatenate([v] * ML_HEADS, axis=0), 0.0).astype(BF16)
        wts = jnp.exp(d_log - m_t) * jnp.dot(qb, kbd, preferred_element_type=F32)
        w_hi = wts.astype(BF16)
        w_lo = (wts - w_hi.astype(F32)).astype(BF16)
        c_prev, n_prev = cst[d], nst[d]
        num = (jnp.dot(w_hi, vbd, preferred_element_type=F32)
               + e_inter * jnp.dot(qb, c_prev.astype(BF16), preferred_element_type=F32))
        den = (jnp.dot(w_hi, ebd, preferred_element_type=F32) + jnp.dot(w_lo, ebd, preferred_element_type=F32)
               + e_inter * jnp.dot(qb, n_prev.astype(BF16), preferred_element_type=F32))
        hsc[d, rows, :] = num / jnp.maximum(jnp.abs(den), jnp.exp(-m_t))
        w_end = b_end - bc + lic
        m_loc = jnp.max(w_end, axis=0, keepdims=True)
        e_end = jnp.exp(w_end - m_loc)
        m_new = jnp.maximum(b_end + m_prev, m_loc)
        a_old = jnp.exp(b_end + m_prev - m_new)
        a_loc = jnp.exp(m_loc - m_new)
        evbd = jnp.where(bd, jnp.concatenate([e_end * v] * ML_HEADS, axis=0), 0.0).astype(BF16)
        eebd = jnp.where(bd, jnp.concatenate([e_end] * ML_HEADS, axis=0), 0.0).astype(BF16)
        cst[d] = a_old * c_prev + a_loc * jnp.dot(kbd, evbd, preferred_element_type=F32)
        nst[d] = a_old * n_prev + a_loc * jnp.dot(kbd, eebd, preferred_element_type=F32)
        return m_new

    def step(i, carry):
        return chunk(0, i, carry[0]), chunk(1, _bwd_chunk(i, n_ctx, n_all), carry[1])

    zero = jnp.zeros((1, GROUP_W), F32)
    lax.fori_loop(0, n_all, step, (zero, zero))
    out_ref[0] = _head_rms(hsc[0] + hsc[1], ebd, ML_DV) * nw_ref[...] * jax.nn.sigmoid(o_ref[0])


def _mlstm_mixer(z, gates_t, norm_w):
    bsz, t, _ = z.shape
    nc = t // ML_CHUNK
    ncp = -(-nc // 8) * 8
    gr = gates_t.reshape(bsz, 4, ML_HEADS, nc, ML_CHUNK).transpose(0, 1, 3, 2, 4).reshape(bsz, 4, nc, GROUP_W)
    gr = jnp.pad(gr, ((0, 0), (0, 0), (0, ncp - nc), (0, 0)))
    col = ML_OFF // GROUP_W
    zspec = lambda j: pl.BlockSpec((1, t, GROUP_W), lambda b: (b, 0, col + j))
    return pl.pallas_call(
        functools.partial(_mlstm_body, t_len=t),
        grid=(bsz,),
        in_specs=[zspec(0), zspec(1), zspec(2), zspec(3),
                  pl.BlockSpec((1, t, LANES), lambda b: (b, 0, ML_GATE_OFF // LANES)),
                  pl.BlockSpec((1, 4, ncp, GROUP_W), lambda b: (b, 0, 0, 0)),
                  pl.BlockSpec((1, GROUP_W), lambda b: (0, 0))],
        out_specs=pl.BlockSpec((1, t, GROUP_W), lambda b: (b, 0, 0)),
        out_shape=jax.ShapeDtypeStruct((bsz, t, GROUP_W), F32),
        scratch_shapes=[pltpu.VMEM((2, t, GROUP_W), F32), pltpu.VMEM((2, t, GROUP_W), F32),
                        pltpu.VMEM((2, ncp, GROUP_W), F32), pltpu.VMEM((2, t, GROUP_W), F32),
                        pltpu.VMEM((2, GROUP_W, GROUP_W), F32), pltpu.VMEM((2, GROUP_W, GROUP_W), F32)],
        compiler_params=_cparams("arbitrary"),
    )(z, z, z, z, z, gr, norm_w.astype(F32).reshape(1, GROUP_W))


def _final_norm_body(x_ref, w_ref, o_ref):
    o_ref[0] = _rms(x_ref[0]) * w_ref[...]


def _final_norm(h, w):
    bsz, t, d = h.shape
    n = t - CTX_LEN
    blk0 = CTX_LEN // ROW_TILE
    return pl.pallas_call(
        _final_norm_body,
        grid=(bsz, n // ROW_TILE),
        in_specs=[pl.BlockSpec((1, ROW_TILE, d), lambda b, j: (b, blk0 + j, 0)),
                  pl.BlockSpec((1, d), lambda b, j: (0, 0))],
        out_specs=pl.BlockSpec((1, ROW_TILE, d), lambda b, j: (b, j, 0)),
        out_shape=jax.ShapeDtypeStruct((bsz, n, d), F32),
        compiler_params=_cparams("arbitrary", "arbitrary"),
    )(h, w.reshape(1, d))


def kernel(x, c, ctx, c_ctx, mod_w, mod_b, norm1_w, norm2_w, w_in, b_in, hg_lb_logits, hg_norm_w,
           s5_lam_re, s5_lam_im, s5_log_dt, s5_b_re, s5_b_im, s5_c_re, s5_c_im, s5_d, s5_glu_w, s5_glu_b,
           da_lq1, da_lk1, da_lq2, da_lk2, da_norm_w, ml_norm_w, w_out, router_w,
           exp_w1, exp_w3, exp_w2, final_norm_w):
    bsz, seq, d = x.shape
    assert ctx.shape[1] == CTX_LEN == ROW_TILE and seq % ROW_TILE == 0 and d == D_MODEL
    t = CTX_LEN + seq
    lb_all = jnp.cumsum(jax.nn.softmax(hg_lb_logits.astype(F32), axis=0), axis=0)
    lb_all = lb_all - lb_all[0]

    n_rows = -(-(bsz + 1) // 8) * 8
    cv = jnp.zeros((n_rows, d), F32).at[:bsz].set(c).at[bsz].set(c_ctx)
    mod_all = _modulation(cv, mod_w, mod_b)
    cos_t, sin_t = _rope_tables(t)
    h = jnp.concatenate([ctx, x], axis=1)

    for li in range(DEPTH):
        ctx_out = li < DEPTH - 1
        m6 = mod_all[li].reshape(n_rows, 6, d)
        modsel = jnp.stack([jnp.broadcast_to(m6[bsz], (bsz, 6, d)), m6[:bsz]], axis=1)
        w_pad = jnp.pad(w_in[li], ((0, 0), (0, IN_COLS_PAD - IN_COLS))).astype(BF16)
        b_pad = jnp.pad(b_in[li], (0, IN_COLS_PAD - IN_COLS)).reshape(1, IN_COLS_PAD)
        wg_t = w_in[li][:, ML_GATE_OFF:].T.astype(BF16)
        bg_col = b_in[li][ML_GATE_OFF:].reshape(N_GATES, 1)
        z, gates_t = _in_projection(h, modsel, norm1_w[li], w_pad, b_pad, wg_t, bg_col)

        mix_a = _hgrn2_mixer(z, lb_all[li], hg_norm_w[li])
        mix_b = _s5_mixer(z, s5_lam_re[li], s5_lam_im[li], s5_log_dt[li], s5_b_re[li], s5_b_im[li],
                          s5_c_re[li], s5_c_im[li], s5_d[li], s5_glu_w[li], s5_glu_b[li])
        lam_init = 0.8 - 0.6 * math.exp(-0.3 * li)
        att_args = (da_lq1[li], da_lk1[li], da_lq2[li], da_lk2[li], da_norm_w[li], lam_init)
        c_l = _diff_attention(z, CTX_LEN, seq, t, True, cos_t, sin_t, *att_args)
        if ctx_out:
            c_c = _diff_attention(z, 0, CTX_LEN, CTX_LEN, False, cos_t, sin_t, *att_args)
        else:
            c_c = jnp.zeros((bsz, CTX_LEN, GROUP_W), F32)
        mix_c = jnp.concatenate([c_c, c_l], axis=1)
        mix_d = _mlstm_mixer(z, gates_t, ml_norm_w[li])

        hm, xm, aff_t = _out_projection((mix_a, mix_b, mix_c, mix_d), h, modsel, norm2_w[li],
                                        w_out[li].astype(BF16), router_w[li].T.astype(BF16))
        w1, w3, w2 = exp_w1[li].astype(BF16), exp_w3[li].astype(BF16), exp_w2[li].astype(BF16)
        slot, slot_t = _route(aff_t, CTX_LEN, seq)
        y_hi, y_lo = _moe_ffn(xm, slot, aff_t, CTX_LEN, seq, 1, w1, w3, w2)
        h_lat = _moe_combine(y_hi, y_lo, slot_t, hm, modsel, CTX_LEN, seq, 1)
        if ctx_out:
            slot, slot_t = _route(aff_t, 0, CTX_LEN)
            y_hi, y_lo = _moe_ffn(xm, slot, aff_t, 0, CTX_LEN, bsz, w1, w3, w2)
            h_ctx = _moe_combine(y_hi, y_lo, slot_t, hm, modsel, 0, CTX_LEN, 0)
        else:
            h_ctx = hm[:, :CTX_LEN]
        h = jnp.concatenate([h_ctx, h_lat], axis=1)
    return _final_norm(h, final_norm_w)
```

```python
import functools
import math
import jax
import jax.numpy as jnp
from jax import lax
from jax.experimental import pallas as pl
from jax.experimental.pallas import tpu as pltpu

D_MODEL = 1024
DEPTH = 2
CTX_LEN = 256
GRID_W = 64
N_MIXERS = 4
GROUP_W = D_MODEL // N_MIXERS
MIX_W = N_MIXERS * GROUP_W
EPS = 1e-6
F32 = jnp.float32
BF16 = jnp.bfloat16
HG_HEADS = 4
HG_DK = GROUP_W // HG_HEADS
HG_DV = GROUP_W // HG_HEADS
HG_CHUNK = 16
S5_CH = 16
S5_GROUPS = GROUP_W // S5_CH
S5_STATE = 64
DA_HEADS = 4
DA_DQK = GROUP_W // (2 * DA_HEADS)
DA_DV = GROUP_W // DA_HEADS
ROPE_BASE = 10000.0
ML_HEADS = 4
ML_DK = GROUP_W // ML_HEADS
ML_DV = GROUP_W // ML_HEADS
ML_CHUNK = 64
N_EXPERTS = 16
EC_CAPACITY = 2
D_EXPERT = 2 * D_MODEL
HG_OFF = 0
S5_OFF = HG_OFF + 5 * GROUP_W
DA_OFF = S5_OFF + GROUP_W
ML_OFF = DA_OFF + 3 * GROUP_W
ML_GATE_OFF = ML_OFF + 4 * GROUP_W
IN_COLS = ML_GATE_OFF + 4 * ML_HEADS

LANES = 128
ROW_TILE = 256
IN_COLS_PAD = ML_GATE_OFF + LANES
N_GATES = 4 * ML_HEADS
S5_LANES = S5_GROUPS * S5_STATE
S5_HALF = S5_LANES // 2
S5_SEGS = 8
VMEM_LIMIT = 56 * 1024 * 1024
LOG2_E = math.log2(math.e)


def _cparams(*sem):
    return pltpu.CompilerParams(dimension_semantics=sem, vmem_limit_bytes=VMEM_LIMIT)


def _rms(x):
    return x * lax.rsqrt(jnp.mean(x * x, axis=-1, keepdims=True) + EPS)


def _mod_body(cv_ref, w_ref, b_ref, o_ref):
    cv = cv_ref[...]
    s = cv * jax.nn.sigmoid(cv)
    o_ref[0] = jnp.dot(s.astype(BF16), w_ref[0].astype(BF16), preferred_element_type=F32) + b_ref[0]


def _modulation(cv, mod_w, mod_b):
    n_l, d, n6 = mod_w.shape
    r = cv.shape[0]
    tn = n6 // 4
    return pl.pallas_call(
        _mod_body,
        grid=(n_l, 4),
        in_specs=[pl.BlockSpec((r, d), lambda l, j: (0, 0)),
                  pl.BlockSpec((1, d, tn), lambda l, j: (l, 0, j)),
                  pl.BlockSpec((1, 1, tn), lambda l, j: (l, 0, j))],
        out_specs=pl.BlockSpec((1, r, tn), lambda l, j: (l, 0, j)),
        out_shape=jax.ShapeDtypeStruct((n_l, r, n6), F32),
        compiler_params=_cparams("arbitrary", "arbitrary"),
    )(cv, mod_w, mod_b.reshape(n_l, 1, n6))


def _inproj_body(h_ref, m_ref, nw_ref, w_ref, b_ref, wg_ref, bg_ref, z_ref, gt_ref):
    m = m_ref[0, 0]
    xn = _rms(h_ref[0]) * nw_ref[...] * (1.0 + m[1:2]) + m[0:1]
    xb = xn.astype(BF16)
    z_ref[0] = jnp.dot(xb, w_ref[...], preferred_element_type=F32) + b_ref[...]
    gt_ref[0] = lax.dot_general(wg_ref[...], xb, (((1,), (1,)), ((), ())),
                                preferred_element_type=F32) + bg_ref[...]


def _in_projection(h, modsel, norm_w, w_pad, b_pad, wg_t, bg_col):
    bsz, t, d = h.shape
    nt = t // ROW_TILE
    return pl.pallas_call(
        _inproj_body,
        grid=(bsz, nt),
        in_specs=[pl.BlockSpec((1, ROW_TILE, d), lambda b, j: (b, j, 0)),
                  pl.BlockSpec((1, 1, 6, d), lambda b, j: (b, jnp.minimum(j, 1), 0, 0)),
                  pl.BlockSpec((1, d), lambda b, j: (0, 0)),
                  pl.BlockSpec((d, IN_COLS_PAD), lambda b, j: (0, 0)),
                  pl.BlockSpec((1, IN_COLS_PAD), lambda b, j: (0, 0)),
                  pl.BlockSpec((N_GATES, d), lambda b, j: (0, 0)),
                  pl.BlockSpec((N_GATES, 1), lambda b, j: (0, 0))],
        out_specs=[pl.BlockSpec((1, ROW_TILE, IN_COLS_PAD), lambda b, j: (b, j, 0)),
                   pl.BlockSpec((1, N_GATES, ROW_TILE), lambda b, j: (b, 0, j))],
        out_shape=[jax.ShapeDtypeStruct((bsz, t, IN_COLS_PAD), F32),
                   jax.ShapeDtypeStruct((bsz, N_GATES, t), F32)],
        compiler_params=_cparams("arbitrary", "arbitrary"),
    )(h, modsel, norm_w.reshape(1, d), w_pad, b_pad, wg_t, bg_col)


def _outproj_body(a_ref, b_ref, c_ref, d_ref, h_ref, m_ref, nw_ref, wo_ref, rw_ref, hm_ref, xm_ref, aff_ref):
    m = m_ref[0, 0]
    y = jnp.concatenate([a_ref[0], b_ref[0], c_ref[0], d_ref[0]], axis=-1).astype(BF16)
    hm = h_ref[0] + m[2:3] * jnp.dot(y, wo_ref[...], preferred_element_type=F32)
    hm_ref[0] = hm
    xm = (_rms(hm) * nw_ref[...] * (1.0 + m[4:5]) + m[3:4]).astype(BF16)
    xm_ref[0] = xm
    logit = lax.dot_general(rw_ref[...], xm, (((1,), (1,)), ((), ())), preferred_element_type=F32)
    e = jnp.exp(logit - jnp.max(logit, axis=0, keepdims=True))
    aff_ref[0] = e / jnp.sum(e, axis=0, keepdims=True)


def _out_projection(mix, h, modsel, norm_w, wo_bf, rw_t):
    bsz, t, d = h.shape
    nt = t // ROW_TILE
    mix_spec = pl.BlockSpec((1, ROW_TILE, GROUP_W), lambda b, j: (b, j, 0))
    row_spec = pl.BlockSpec((1, ROW_TILE, d), lambda b, j: (b, j, 0))
    return pl.pallas_call(
        _outproj_body,
        grid=(bsz, nt),
        in_specs=[mix_spec, mix_spec, mix_spec, mix_spec, row_spec,
                  pl.BlockSpec((1, 1, 6, d), lambda b, j: (b, jnp.minimum(j, 1), 0, 0)),
                  pl.BlockSpec((1, d), lambda b, j: (0, 0)),
                  pl.BlockSpec((MIX_W, d), lambda b, j: (0, 0)),
                  pl.BlockSpec((N_EXPERTS, d), lambda b, j: (0, 0))],
        out_specs=[row_spec, row_spec, pl.BlockSpec((1, N_EXPERTS, ROW_TILE), lambda b, j: (b, 0, j))],
        out_shape=[jax.ShapeDtypeStruct((bsz, t, d), F32), jax.ShapeDtypeStruct((bsz, t, d), BF16),
                   jax.ShapeDtypeStruct((bsz, N_EXPERTS, t), F32)],
        compiler_params=_cparams("arbitrary", "arbitrary"),
    )(*mix, h, modsel, norm_w.reshape(1, d), wo_bf, rw_t)


def _topk_body(aff_ref, slot_ref, slot_t_ref, *, cap, n, row0):
    aff = aff_ref[0, :, row0:row0 + n]
    bits = pltpu.bitcast(aff, jnp.int32)
    thr = jnp.zeros((N_EXPERTS, 1), jnp.int32)
    for bit in range(30, -1, -1):
        cand = thr | (1 << bit)
        cnt = jnp.sum(jnp.where(bits >= cand, 1.0, 0.0), axis=-1, keepdims=True)
        thr = jnp.where(cnt >= cap, cand, thr)
    room = cap - jnp.sum(jnp.where(bits > thr, 1.0, 0.0), axis=-1, keepdims=True)
    ri = lax.broadcasted_iota(jnp.int32, (LANES, LANES), 0)
    ci = lax.broadcasted_iota(jnp.int32, (LANES, LANES), 1)
    incl = jnp.where(ri <= ci, 1.0, 0.0).astype(BF16)
    off_eq = jnp.zeros((N_EXPERTS, 1), F32)
    off_sel = jnp.zeros((N_EXPERTS, 1), F32)
    pieces = []
    for j in range(n // LANES):
        sl = slice(j * LANES, (j + 1) * LANES)
        bits_b = bits[:, sl]
        eq_b = jnp.where(bits_b == thr, 1.0, 0.0)
        pos_eq = jnp.dot(eq_b.astype(BF16), incl, preferred_element_type=F32) - eq_b + off_eq
        sel = jnp.where(bits_b > thr, 1.0, jnp.where(pos_eq < room, eq_b, 0.0))
        pos_sel = jnp.dot(sel.astype(BF16), incl, preferred_element_type=F32) - sel + off_sel
        piece = jnp.where(sel > 0.5, pos_sel, -1.0)
        slot_ref[0, :, sl] = piece.astype(jnp.int32)
        pieces.append(piece)
        off_eq = off_eq + jnp.sum(eq_b, axis=-1, keepdims=True)
        off_sel = off_sel + jnp.sum(sel, axis=-1, keepdims=True)
    pad = jnp.full((LANES - N_EXPERTS, LANES), -1.0, F32)
    for j in range(n // LANES):
        tile = jnp.concatenate([pieces[j], pad], axis=0)
        slot_t_ref[0, j * LANES:(j + 1) * LANES, :] = tile.T.astype(jnp.int32)


def _route(aff_t, row0, n):
    bsz, _, t = aff_t.shape
    cap = EC_CAPACITY * n // N_EXPERTS
    return pl.pallas_call(
        functools.partial(_topk_body, cap=cap, n=n, row0=row0),
        grid=(bsz,),
        in_specs=[pl.BlockSpec((1, N_EXPERTS, t), lambda b: (b, 0, 0))],
        out_specs=[pl.BlockSpec((1, N_EXPERTS, n), lambda b: (b, 0, 0)),
                   pl.BlockSpec((1, n, LANES), lambda b: (b, 0, 0))],
        out_shape=[jax.ShapeDtypeStruct((bsz, N_EXPERTS, n), jnp.int32),
                   jax.ShapeDtypeStruct((bsz, n, LANES), jnp.int32)],
        compiler_params=_cparams("arbitrary"),
    )(aff_t)


def _moe_ffn_body(x_ref, slot_ref, aff_ref, w1_ref, w3_ref, w2_ref, y_ref, xe_scr, g_scr, *, nb, cap, n, off):
    for i in range(nb):
        hit = lax.broadcasted_iota(jnp.int32, (cap, n), 0) == slot_ref[i, 0]
        onehot = jnp.where(hit, 1.0, 0.0).astype(BF16)
        xe_scr[i * cap:(i + 1) * cap, :] = jnp.dot(onehot, x_ref[i, off:off + n, :],
                                                   preferred_element_type=F32).astype(BF16)
        g_scr[i * cap:(i + 1) * cap, :] = jnp.sum(jnp.where(hit, aff_ref[i, 0, :, off:off + n], 0.0),
                                                  axis=-1, keepdims=True)
    xe = xe_scr[...]
    h1 = jnp.dot(xe, w1_ref[0], preferred_element_type=F32)
    h3 = jnp.dot(xe, w3_ref[0], preferred_element_type=F32)
    act = (h1 * jax.nn.sigmoid(h1) * h3).astype(BF16)
    y = (jnp.dot(act, w2_ref[0], preferred_element_type=F32) * g_scr[...]).astype(BF16)
    for i in range(nb):
        y_ref[i, 0] = y[i * cap:(i + 1) * cap]


def _moe_ffn(xm, slot, aff_t, row0, n, nb, w1, w3, w2):
    bsz, t, d = xm.shape
    cap = EC_CAPACITY * n // N_EXPERTS
    rows, off = (n, 0) if row0 % n == 0 else (t, row0)
    blk = row0 // n if off == 0 else 0
    slot4 = slot.reshape(bsz, N_EXPERTS, 1, n)
    aff4 = aff_t.reshape(bsz, N_EXPERTS, 1, t)
    y_spec = pl.BlockSpec((nb, 1, cap, d), lambda e, b: (b, e, 0, 0))
    y_shape = jax.ShapeDtypeStruct((bsz, N_EXPERTS, cap, d), BF16)
    return pl.pallas_call(
        functools.partial(_moe_ffn_body, nb=nb, cap=cap, n=n, off=off),
        grid=(N_EXPERTS, bsz // nb),
        in_specs=[pl.BlockSpec((nb, rows, d), lambda e, b: (b, blk, 0)),
                  pl.BlockSpec((nb, 1, 1, n), lambda e, b: (b, e, 0, 0)),
                  pl.BlockSpec((nb, 1, 1, rows), lambda e, b: (b, e, 0, blk)),
                  pl.BlockSpec((1, d, D_EXPERT), lambda e, b: (e, 0, 0)),
                  pl.BlockSpec((1, d, D_EXPERT), lambda e, b: (e, 0, 0)),
                  pl.BlockSpec((1, D_EXPERT, d), lambda e, b: (e, 0, 0))],
        out_specs=y_spec,
        out_shape=y_shape,
        scratch_shapes=[pltpu.VMEM((nb * cap, d), BF16), pltpu.VMEM((nb * cap, 1), F32)],
        compiler_params=_cparams("arbitrary", "arbitrary"),
    )(xm, slot4, aff4, w1, w3, w2)


def _moe_combine_body(y_ref, st_ref, hm_ref, m_ref, fw_ref, o_ref, *, cap, final):
    st = st_ref[0]
    lane = lax.broadcasted_iota(jnp.int32, (st.shape[0], cap), 1)
    hots = [jnp.where(lane == st[:, e:e + 1], 1.0, 0.0).astype(BF16) for e in range(N_EXPERTS)]
    if cap % LANES == 0:
        acc = jnp.dot(jnp.concatenate(hots, axis=1), y_ref[0].reshape(N_EXPERTS * cap, y_ref.shape[-1]),
                      preferred_element_type=F32)
    else:
        acc = sum(jnp.dot(hots[e], y_ref[0, e], preferred_element_type=F32) for e in range(N_EXPERTS))
    h = hm_ref[0] + m_ref[0, 0][5:6] * acc
    o_ref[0] = _rms(h) * fw_ref[...] if final else h


def _moe_combine(y, slot_t, hm, modsel, row0, n, is_latent, final_w=None):
    bsz, _, cap, d = y.shape
    nt = n // ROW_TILE
    blk0 = row0 // ROW_TILE
    final = final_w is not None
    fw = (final_w if final else jnp.ones((d,), F32)).reshape(1, d)
    return pl.pallas_call(
        functools.partial(_moe_combine_body, cap=cap, final=final),
        grid=(bsz, nt),
        in_specs=[pl.BlockSpec((1, N_EXPERTS, cap, d), lambda b, j: (b, 0, 0, 0)),
                  pl.BlockSpec((1, ROW_TILE, LANES), lambda b, j: (b, j, 0)),
                  pl.BlockSpec((1, ROW_TILE, d), lambda b, j: (b, blk0 + j, 0)),
                  pl.BlockSpec((1, 1, 6, d), lambda b, j: (b, is_latent, 0, 0)),
                  pl.BlockSpec((1, d), lambda b, j: (0, 0))],
        out_specs=pl.BlockSpec((1, ROW_TILE, d), lambda b, j: (b, j if final else blk0 + j, 0)),
        out_shape=jax.ShapeDtypeStruct((bsz, n, d) if final else hm.shape, F32),
        input_output_aliases={} if final else {2: 0},
        compiler_params=_cparams("arbitrary", "arbitrary"),
    )(y, slot_t, hm, modsel, fw)


def _swap_pairs(x):
    w = x.shape[-1]
    lane = lax.broadcasted_iota(jnp.int32, x.shape, x.ndim - 1)
    return jnp.where(lane % 2 == 0, pltpu.roll(x, w - 1, x.ndim - 1), pltpu.roll(x, 1, x.ndim - 1))


def _attn_body(q_ref, k_ref, v_ref, cq_ref, sq_ref, ck_ref, sk_ref, lq1_ref, lk1_ref, lq2_ref, lk2_ref, nw_ref,
               o_ref, k_scr, vt_scr, *, rope, lam_init):
    @pl.when(pl.program_id(1) == 0)
    def _():
        k = k_ref[0]
        if rope:
            k = k * ck_ref[...] + _swap_pairs(k) * sk_ref[...]
        k_scr[...] = k.astype(BF16)
        for r in range(0, k_scr.shape[0], ROW_TILE):
            vt_scr[:, r:r + ROW_TILE] = v_ref[0, r:r + ROW_TILE, :].T.astype(BF16)

    q = q_ref[0]
    if rope:
        q = q * cq_ref[...] + _swap_pairs(q) * sq_ref[...]
    q_t = (q * (DA_DQK ** -0.5)).T
    lam = (jnp.exp(jnp.sum(lq1_ref[...] * lk1_ref[...], axis=-1, keepdims=True))
           - jnp.exp(jnp.sum(lq2_ref[...] * lk2_ref[...], axis=-1, keepdims=True)) + lam_init)
    row = lax.broadcasted_iota(jnp.int32, (GROUP_W, 1), 0)
    kk = k_scr[...]
    outs = []
    for h in range(DA_HEADS):
        es, rs = [], []
        for mi in range(2):
            qm = jnp.where(row // DA_DQK == 2 * h + mi, q_t, 0.0).astype(BF16)
            s = jnp.dot(kk, qm, preferred_element_type=F32)
            e = jnp.exp(s - jnp.max(s, axis=0, keepdims=True))
            es.append(e)
            rs.append(1.0 / jnp.sum(e, axis=0, keepdims=True))
        a = (es[0] * rs[0] - es[1] * (lam * rs[1])).astype(BF16)
        oh = jnp.dot(vt_scr[h * DA_DV:(h + 1) * DA_DV, :], a, preferred_element_type=F32)
        outs.append(oh * lax.rsqrt(jnp.mean(oh * oh, axis=0, keepdims=True) + EPS))
    o_ref[0] = jnp.concatenate(outs, axis=0).T * nw_ref[...] * (1.0 - lam_init)


def _diff_attention(z, q_row0, nq, nk, rope, cos_t, sin_t, lq1, lk1, lq2, lk2, norm_w, lam_init):
    bsz = z.shape[0]
    qb = ROW_TILE
    q0 = q_row0 // qb
    col = DA_OFF // GROUP_W
    vec = lambda a: a.reshape(1, -1)
    small = pl.BlockSpec((1, DA_DQK), lambda b, j: (0, 0))
    return pl.pallas_call(
        functools.partial(_attn_body, rope=rope, lam_init=lam_init),
        grid=(bsz, nq // qb),
        in_specs=[pl.BlockSpec((1, qb, GROUP_W), lambda b, j: (b, q0 + j, col)),
                  pl.BlockSpec((1, nk, GROUP_W), lambda b, j: (b, 0, col + 1)),
                  pl.BlockSpec((1, nk, GROUP_W), lambda b, j: (b, 0, col + 2)),
                  pl.BlockSpec((qb, GROUP_W), lambda b, j: (q0 + j, 0)),
                  pl.BlockSpec((qb, GROUP_W), lambda b, j: (q0 + j, 0)),
                  pl.BlockSpec((nk, GROUP_W), lambda b, j: (0, 0)),
                  pl.BlockSpec((nk, GROUP_W), lambda b, j: (0, 0)),
                  small, small, small, small,
                  pl.BlockSpec((1, GROUP_W), lambda b, j: (0, 0))],
        out_specs=pl.BlockSpec((1, qb, GROUP_W), lambda b, j: (b, j, 0)),
        out_shape=jax.ShapeDtypeStruct((bsz, nq, GROUP_W), F32),
        scratch_shapes=[pltpu.VMEM((nk, GROUP_W), BF16), pltpu.VMEM((GROUP_W, nk), BF16)],
        compiler_params=_cparams("arbitrary", "arbitrary"),
    )(z, z, z, cos_t, sin_t, cos_t, sin_t, vec(lq1), vec(lk1), vec(lq2), vec(lk2), vec(norm_w))


def _rope_tables(t):
    n = t - CTX_LEN
    axis_dim = DA_DQK // 2
    inv = ROPE_BASE ** (-jnp.arange(0, axis_dim, 2, dtype=F32) / axis_dim)
    tok = jnp.arange(n, dtype=jnp.int32)
    row = (tok // GRID_W).astype(F32)
    colp = (tok % GRID_W).astype(F32)
    ang = jnp.concatenate([row[:, None] * inv, colp[:, None] * inv], axis=-1)
    cos = jnp.repeat(jnp.cos(ang), 2, axis=-1)
    sin = jnp.repeat(jnp.sin(ang), 2, axis=-1) * jnp.tile(jnp.array([-1.0, 1.0], F32), DA_DQK // 2)
    reps = GROUP_W // DA_DQK
    cos = jnp.concatenate([jnp.ones((CTX_LEN, GROUP_W), F32), jnp.tile(cos, (1, reps))], axis=0)
    sin = jnp.concatenate([jnp.zeros((CTX_LEN, GROUP_W), F32), jnp.tile(sin, (1, reps))], axis=0)
    return cos, sin


def _cmul(ar, ai, br, bi):
    return ar * br - ai * bi, ar * bi + ai * br


def _s5_scan_part(bufs, coef, base, seg_len, carry):
    xrf, xif, xrb, xib = bufs
    af_r, af_i, ab_r, ab_i = coef
    assert seg_len & (seg_len - 1) == 0

    def sweep(store, init):
        def step(i, st):
            fr, fi, br, bi = st
            rf = pl.ds(pl.multiple_of(base + S5_SEGS * i, S5_SEGS), S5_SEGS)
            rb = pl.ds(pl.multiple_of(base + S5_SEGS * (seg_len - 1 - i), S5_SEGS), S5_SEGS)
            nfr = af_r * fr - af_i * fi + xrf[rf, :]
            nfi = af_r * fi + af_i * fr + xif[rf, :]
            nbr = ab_r * br - ab_i * bi + xrb[rb, :]
            nbi = ab_r * bi + ab_i * br + xib[rb, :]
            if store:
                xrf[rf, :] = nfr
                xif[rf, :] = nfi
                xrb[rb, :] = nbr
                xib[rb, :] = nbi
            return nfr, nfi, nbr, nbi
        return lax.fori_loop(0, seg_len, step, init, unroll=4)

    zero = jnp.zeros(af_r.shape, F32)
    ef_r, ef_i, eb_r, eb_i = sweep(False, (zero,) * 4)
    pf = (af_r[0:1], af_i[0:1])
    pb = (ab_r[0:1], ab_i[0:1])
    for _ in range(seg_len.bit_length() - 1):
        pf = _cmul(*pf, *pf)
        pb = _cmul(*pb, *pb)
    cf_r, cf_i, cb_r, cb_i = carry
    ins_f, ins_b = [], []
    for j in range(S5_SEGS):
        ins_f.append((cf_r, cf_i))
        gr, gi = _cmul(*pf, cf_r, cf_i)
        cf_r, cf_i = ef_r[j:j + 1] + gr, ef_i[j:j + 1] + gi
    for j in range(S5_SEGS - 1, -1, -1):
        ins_b.append((cb_r, cb_i))
        gr, gi = _cmul(*pb, cb_r, cb_i)
        cb_r, cb_i = eb_r[j:j + 1] + gr, eb_i[j:j + 1] + gi
    ins_b.reverse()
    stack = lambda rows: jnp.concatenate(rows, axis=0)
    sweep(True, (stack([a for a, _ in ins_f]), stack([b for _, b in ins_f]),
                 stack([a for a, _ in ins_b]), stack([b for _, b in ins_b])))
    return cf_r, cf_i, cb_r, cb_i


def _seg_interleave(x):
    b, r, c = x.shape
    return x.reshape(b, S5_SEGS, r // S5_SEGS, c).transpose(0, 2, 1, 3).reshape(b, r, c)


def _seg_deinterleave(x):
    b, r, c = x.shape
    return x.reshape(b, r // S5_SEGS, S5_SEGS, c).transpose(0, 2, 1, 3).reshape(b, r, c)


def _s5_body(u_ref, bre_ref, bim_ref, cre_ref, cim_ref, disc_ref, dsk_ref, gw_ref, gb_ref, o_ref,
             xrf, xif, xrb, xib, y_scr, *, t_len):
    u = u_ref[0]
    ub = u.astype(BF16)
    y_scr[...] = u * dsk_ref[...]
    n_lat = t_len - CTX_LEN
    for half in range(2):
        ls = slice(half * S5_HALF, (half + 1) * S5_HALF)
        bu_r = jnp.dot(ub, bre_ref[:, ls], preferred_element_type=F32)
        bu_i = jnp.dot(ub, bim_ref[:, ls], preferred_element_type=F32)
        disc = disc_ref[:, ls]
        xrf[...] = disc[2:3] * bu_r - disc[3:4] * bu_i
        xif[...] = disc[2:3] * bu_i + disc[3:4] * bu_r
        xrb[...] = disc[6:7] * bu_r - disc[7:8] * bu_i
        xib[...] = disc[6:7] * bu_i + disc[7:8] * bu_r
        coef = tuple(jnp.broadcast_to(disc[r:r + 1], (S5_SEGS, S5_HALF)) for r in (0, 1, 4, 5))
        zero = jnp.zeros((1, S5_HALF), F32)
        carry = _s5_scan_part((xrf, xif, xrb, xib), coef, 0, CTX_LEN // S5_SEGS, (zero,) * 4)
        _s5_scan_part((xrf, xif, xrb, xib), coef, CTX_LEN, n_lat // S5_SEGS, carry)
        cre = cre_ref[ls, :]
        cim = cim_ref[ls, :]
        y_scr[...] += (jnp.dot(xrf[...].astype(BF16), cre, preferred_element_type=F32)
                       - jnp.dot(xif[...].astype(BF16), cim, preferred_element_type=F32)
                       + jnp.dot(xrb[...].astype(BF16), cre, preferred_element_type=F32)
                       - jnp.dot(xib[...].astype(BF16), cim, preferred_element_type=F32))
    y = jax.nn.gelu(y_scr[...])
    gate = jax.nn.sigmoid(jnp.dot(y.astype(BF16), gw_ref[...], preferred_element_type=F32) + gb_ref[...])
    o_ref[0] = y * gate


def _s5_mixer(z, lam_re, lam_im, log_dt, b_re, b_im, c_re, c_im, d_skip, glu_w, glu_b):
    bsz, t, _ = z.shape
    eye = jnp.eye(S5_GROUPS, dtype=F32)
    bbd = lambda w: jnp.einsum('gpc,gh->gchp', w.astype(F32), eye).reshape(GROUP_W, S5_LANES).astype(BF16)
    cbd = lambda w: jnp.einsum('gcp,gh->gphc', w.astype(F32), eye).reshape(S5_LANES, GROUP_W).astype(BF16)
    rows = []
    for d in range(2):
        lr, li = lam_re[d].astype(F32), lam_im[d].astype(F32)
        dt = jnp.exp(log_dt[d].astype(F32))[:, None]
        mag = jnp.exp(lr * dt)
        ab_re, ab_im = mag * jnp.cos(li * dt), mag * jnp.sin(li * dt)
        den = lr * lr + li * li
        co_re = ((ab_re - 1.0) * lr + ab_im * li) / den
        co_im = (ab_im * lr - (ab_re - 1.0) * li) / den
        rows += [ab_re, ab_im, co_re, co_im]
    disc = jnp.stack([r.reshape(S5_LANES) for r in rows], axis=0)
    full = lambda shape: pl.BlockSpec(shape, lambda b: (0,) * len(shape))
    u = z[:, :, S5_OFF:S5_OFF + GROUP_W]
    u = jnp.concatenate([_seg_interleave(u[:, :CTX_LEN]), _seg_interleave(u[:, CTX_LEN:])], axis=1)
    out = pl.pallas_call(
        functools.partial(_s5_body, t_len=t),
        grid=(bsz,),
        in_specs=[pl.BlockSpec((1, t, GROUP_W), lambda b: (b, 0, 0)),
                  full((GROUP_W, S5_LANES)), full((GROUP_W, S5_LANES)),
                  full((S5_LANES, GROUP_W)), full((S5_LANES, GROUP_W)),
                  full((8, S5_LANES)), full((1, GROUP_W)), full((GROUP_W, GROUP_W)), full((1, GROUP_W))],
        out_specs=pl.BlockSpec((1, t, GROUP_W), lambda b: (b, 0, 0)),
        out_shape=jax.ShapeDtypeStruct((bsz, t, GROUP_W), F32),
        scratch_shapes=[pltpu.VMEM((t, S5_HALF), F32)] * 4 + [pltpu.VMEM((t, GROUP_W), F32)],
        compiler_params=_cparams("arbitrary"),
    )(u, bbd(b_re), bbd(b_im), cbd(c_re), cbd(c_im), disc, d_skip.astype(F32).reshape(1, GROUP_W),
      glu_w.astype(BF16), glu_b.astype(F32).reshape(1, GROUP_W))
    return jnp.concatenate([_seg_deinterleave(out[:, :CTX_LEN]), _seg_deinterleave(out[:, CTX_LEN:])], axis=1)


def rms_norm(x, w):
    xf = x.astype(F32)
    y = xf * lax.rsqrt(jnp.mean(xf * xf, axis=-1, keepdims=True) + EPS)
    return (y * w.astype(F32)).astype(x.dtype)


def head_rms_norm(x, w, n_heads):
    shp = x.shape
    xh = x.reshape(shp[:-1] + (n_heads, -1))
    return rms_norm(xh, w.reshape(n_heads, -1)).reshape(shp)


def flip_seq(a, reverse):
    return a[:, ::-1] if reverse else a


def gla_chunked(q, k, v, log_f, s0, with_out=True):
    bsz, t_len, nh, _ = q.shape
    nc = t_len // HG_CHUNK
    def chunks(a):
        return a.astype(F32).reshape(bsz, nc, HG_CHUNK, nh, a.shape[-1])
    q, k, v, log_f = chunks(q), chunks(k), chunks(v), chunks(log_f)
    b = jnp.cumsum(log_f, axis=2)
    b_end = b[:, :, -1]
    ds = jnp.einsum('bcshd,bcshv->bchdv', k * jnp.exp(b_end[:, :, None] - b), v)
    def step(s, inp):
        g_c, ds_c = inp
        return jnp.exp(g_c)[..., None] * s + ds_c, s
    s_fin, s_start = lax.scan(step, s0, (jnp.moveaxis(b_end, 1, 0), jnp.moveaxis(ds, 1, 0)))
    if not with_out:
        return None, s_fin
    s_start = jnp.moveaxis(s_start, 0, 1)
    tri = jnp.tril(jnp.ones((HG_CHUNK, HG_CHUNK), bool))[None, None, :, :, None, None]
    decay = jnp.exp(jnp.where(tri, b[:, :, :, None] - b[:, :, None], -jnp.inf))
    scores = jnp.sum(q[:, :, :, None] * decay * k[:, :, None], axis=-1)
    o = (jnp.einsum('bctsh,bcshv->bcthv', scores, v)
         + jnp.einsum('bcthd,bchdv->bcthv', q * jnp.exp(b), s_start))
    return o.reshape(bsz, t_len, nh, -1), s_fin


def hgrn2_mixer(zc, zl, lb, norm_w, with_ctx_out):
    def heads(a):
        return a.reshape(a.shape[:2] + (HG_HEADS, -1))
    def split(z):
        q, i, ff, fb, g = jnp.split(z, 5, axis=-1)
        return heads(q) * HG_DK ** -0.5, heads(i), (heads(ff), heads(fb)), g
    qc, ic, fc, gc = split(zc)
    ql, il, fl, gl = split(zl)
    oc, ol = 0.0, 0.0
    for d in range(2):
        lbd = lb[d].reshape(HG_HEADS, HG_DK)
        f_c = lbd + (1.0 - lbd) * jax.nn.sigmoid(fc[d].astype(F32))
        f_l = lbd + (1.0 - lbd) * jax.nn.sigmoid(fl[d].astype(F32))
        s0 = jnp.zeros((zc.shape[0], HG_HEADS, HG_DK, HG_DV), F32)
        o_c, s_c = gla_chunked(flip_seq(qc, d), flip_seq(1.0 - f_c, d), flip_seq(ic, d),
                               flip_seq(jnp.log(f_c), d), s0, with_ctx_out)
        o_l, _ = gla_chunked(flip_seq(ql, d), flip_seq(1.0 - f_l, d), flip_seq(il, d),
                             flip_seq(jnp.log(f_l), d), s_c)
        ol = ol + flip_seq(o_l, d)
        if with_ctx_out:
            oc = oc + flip_seq(o_c, d)
    def readout(o, g):
        o = o.reshape(o.shape[:2] + (-1,)).astype(g.dtype)
        return head_rms_norm(o, norm_w, HG_HEADS) * jax.nn.silu(g)
    return (readout(oc, gc) if with_ctx_out else None), readout(ol, gl)


def mlstm_chunked(q, k, v, log_i, log_f, state, with_out=True):
    bsz, t_len, nh, _ = q.shape
    nc = t_len // ML_CHUNK
    def chunks(a):
        return a.astype(F32).reshape((bsz, nc, ML_CHUNK) + a.shape[2:])
    q, k, v, log_i, log_f = chunks(q), chunks(k), chunks(v), chunks(log_i), chunks(log_f)
    b = jnp.cumsum(log_f, axis=2)
    b_end = b[:, :, -1]
    w_end = b_end[:, :, None] - b + log_i
    m_loc = jnp.max(w_end, axis=2)
    e_end = jnp.exp(w_end - m_loc[:, :, None])
    c_loc = jnp.einsum('bcshv,bcshd->bchvd', e_end[..., None] * v, k)
    n_loc = jnp.einsum('bcsh,bcshd->bchd', e_end, k)
    def step(carry, inp):
        c_st, n_st, m_st = carry
        be, ml, cl, nl = inp
        m_new = jnp.maximum(be + m_st, ml)
        a_old, a_loc = jnp.exp(be + m_st - m_new), jnp.exp(ml - m_new)
        c_new = a_old[..., None, None] * c_st + a_loc[..., None, None] * cl
        n_new = a_old[..., None] * n_st + a_loc[..., None] * nl
        return (c_new, n_new, m_new), carry
    mv = lambda a: jnp.moveaxis(a, 1, 0)
    final, starts = lax.scan(step, state, (mv(b_end), mv(m_loc), mv(c_loc), mv(n_loc)))
    if not with_out:
        return None, final
    c_s, n_s, m_s = (jnp.moveaxis(a, 0, 1) for a in starts)
    tri = jnp.tril(jnp.ones((ML_CHUNK, ML_CHUNK), bool))[None, None, :, :, None]
    d_log = jnp.where(tri, b[:, :, :, None] - b[:, :, None] + log_i[:, :, None], -jnp.inf)
    w_inter = b + m_s[:, :, None]
    m_t = jnp.maximum(w_inter, jnp.max(d_log, axis=3))
    e_inter = jnp.exp(w_inter - m_t)
    wts = jnp.exp(d_log - m_t[:, :, :, None]) * jnp.einsum('bcthd,bcshd->bctsh', q, k)
    num = (jnp.einsum('bctsh,bcshv->bcthv', wts, v)
           + e_inter[..., None] * jnp.einsum('bcthd,bchvd->bcthv', q, c_s))
    den = jnp.sum(wts, axis=3) + e_inter * jnp.einsum('bcthd,bchd->bcth', q, n_s)
    h = num / jnp.maximum(jnp.abs(den), jnp.exp(-m_t))[..., None]
    return h.reshape(bsz, t_len, nh, -1), final


def mlstm_mixer(zc, zl, norm_w, with_ctx_out):
    def split(z):
        shp = z.shape[:2]
        q, k, v, o = (z[..., j * GROUP_W:(j + 1) * GROUP_W] for j in range(4))
        gates = z[..., 4 * GROUP_W:].astype(F32).reshape(shp + (4, ML_HEADS))
        hd = lambda a: a.reshape(shp + (ML_HEADS, -1))
        return hd(q), hd(k) * ML_DK ** -0.5, hd(v), o, gates
    qc, kc, vc, oc, gtc = split(zc)
    ql, kl, vl, ol, gtl = split(zl)
    bsz = zc.shape[0]
    hc, hl = 0.0, 0.0
    for d in range(2):
        state0 = (jnp.zeros((bsz, ML_HEADS, ML_DV, ML_DK), F32), jnp.zeros((bsz, ML_HEADS, ML_DK), F32),
                  jnp.zeros((bsz, ML_HEADS), F32))
        h_c, st_c = mlstm_chunked(flip_seq(qc, d), flip_seq(kc, d), flip_seq(vc, d), flip_seq(gtc[:, :, d], d),
                                  flip_seq(jax.nn.log_sigmoid(gtc[:, :, 2 + d]), d), state0, with_ctx_out)
        h_l, _ = mlstm_chunked(flip_seq(ql, d), flip_seq(kl, d), flip_seq(vl, d), flip_seq(gtl[:, :, d], d),
                               flip_seq(jax.nn.log_sigmoid(gtl[:, :, 2 + d]), d), st_c)
        hl = hl + flip_seq(h_l, d)
        if with_ctx_out:
            hc = hc + flip_seq(h_c, d)
    def readout(h, o):
        h = h.reshape(h.shape[:2] + (-1,)).astype(o.dtype)
        return head_rms_norm(h, norm_w, ML_HEADS) * jax.nn.sigmoid(o)
    return (readout(hc, oc) if with_ctx_out else None), readout(hl, ol)


def _split3(x):
    hi = x.astype(BF16)
    r = x - hi.astype(F32)
    mid = r.astype(BF16)
    return hi, mid, (r - mid.astype(F32)).astype(BF16)


def _dot3(x, sel):
    return sum(jnp.dot(p, sel, preferred_element_type=F32) for p in _split3(x))


def _mdot3(sel, x):
    return sum(jnp.dot(sel, p, preferred_element_type=F32) for p in _split3(x))


def _dot2(x, sel):
    hi = x.astype(BF16)
    lo = (x - hi.astype(F32)).astype(BF16)
    return jnp.dot(hi, sel, preferred_element_type=F32) + jnp.dot(lo, sel, preferred_element_type=F32)


def _head_blocks(n, seg):
    ri = lax.broadcasted_iota(jnp.int32, (n, n), 0)
    ci = lax.broadcasted_iota(jnp.int32, (n, n), 1)
    return ri // seg == ci // seg, ri, ci


def _chunk_cumsum_cols(x, chunk, reverse):
    same, ri, ci = _head_blocks(ROW_TILE, chunk)
    tri = jnp.where(same & ((ci >= ri) if reverse else (ci <= ri)), 1.0, 0.0).astype(BF16)
    return [_mdot3(tri, x[r:r + ROW_TILE]) for r in range(0, x.shape[0], ROW_TILE)]


def _repeat_row(row, n):
    return pl.ds(row, n, stride=0)


def _bwd_chunk(i, n_ctx, n_all):
    return jnp.where(i < n_ctx, n_ctx - 1 - i, n_all + n_ctx - 1 - i)


def _head_rms(x, ebd, seg):
    return x * lax.rsqrt(_dot2(x * x, ebd) * (1.0 / seg) + EPS)


def _hgrn2_body(z_ref, lb_ref, nw_ref, out_ref, bsc, ksc, vsc, osc, st, *, t_len):
    n_half = GROUP_W // LANES
    halves = lambda ref, idx, rows: jnp.concatenate([ref[idx + (h, rows)] for h in range(n_half)], axis=1)
    for h in range(n_half):
        vsc[h] = z_ref[0, :, GROUP_W + h * LANES:GROUP_W + (h + 1) * LANES]
    for d in range(2):
        lb = lb_ref[d:d + 1, :]
        f = lb + (1.0 - lb) * jax.nn.sigmoid(z_ref[0, :, (2 + d) * GROUP_W:(3 + d) * GROUP_W])
        k_all = 1.0 - f
        for h in range(n_half):
            ksc[d, h] = k_all[:, h * LANES:(h + 1) * LANES]
        for j, blk in enumerate(_chunk_cumsum_cols(jnp.log(f), HG_CHUNK, d == 1)):
            for h in range(n_half):
                bsc[d, h, j * ROW_TILE:(j + 1) * ROW_TILE, :] = blk[:, h * LANES:(h + 1) * LANES] * LOG2_E
    st[...] = jnp.zeros(st.shape, F32)
    bd, _, _ = _head_blocks(GROUP_W, HG_DK)
    ebd = jnp.where(bd, 1.0, 0.0).astype(BF16)
    tt = lax.broadcasted_iota(jnp.int32, (HG_CHUNK, GROUP_W), 0)
    n_all = t_len // HG_CHUNK
    n_ctx = CTX_LEN // HG_CHUNK

    def chunk(d, c):
        r0 = pl.multiple_of(c * HG_CHUNK, HG_CHUNK)
        rows = pl.ds(r0, HG_CHUNK)
        q = z_ref[0, rows, 0:GROUP_W] * (HG_DK ** -0.5)
        v = halves(vsc, (), rows)
        k = halves(ksc, (d,), rows)
        b = halves(bsc, (d,), rows)
        b_end = b[0:1] if d else b[HG_CHUNK - 1:HG_CHUNK]
        slabs = []
        for s in range(HG_CHUNK):
            row_s = _repeat_row(r0 + s, HG_CHUNK)
            valid = (tt <= s) if d else (tt >= s)
            slabs.append(jnp.where(valid, jnp.exp2(b - halves(bsc, (d,), row_s)), 0.0) * (q * halves(ksc, (d,), row_s)))
        a = jnp.concatenate(slabs, axis=0).astype(BF16)
        sc = jnp.dot(a, ebd, preferred_element_type=F32)
        o_intra = jnp.zeros((HG_CHUNK, GROUP_W), F32)
        for s in range(HG_CHUNK):
            o_intra = o_intra + (sc[s * HG_CHUNK:(s + 1) * HG_CHUNK]
                                 * halves(vsc, (), _repeat_row(r0 + s, HG_CHUNK)))
        s_prev = st[d]
        o_inter = lax.dot_general((q * jnp.exp2(b)).astype(BF16), s_prev.astype(BF16), (((1,), (1,)), ((), ())),
                                  preferred_element_type=F32)
        osc[d, rows, :] = o_intra + o_inter
        kd = (k * jnp.exp2(b_end - b)).astype(BF16)
        upd = lax.dot_general(v.astype(BF16), kd, (((0,), (0,)), ((), ())), preferred_element_type=F32)
        st[d] = s_prev * jnp.exp2(b_end) + jnp.where(bd, upd, 0.0)

    def step(i, carry):
        chunk(0, i)
        chunk(1, _bwd_chunk(i, n_ctx, n_all))
        return carry

    lax.fori_loop(0, n_all, step, 0)
    o = osc[0] + osc[1]
    g = z_ref[0, :, 4 * GROUP_W:5 * GROUP_W]
    out_ref[0] = _head_rms(o, ebd, HG_DV) * nw_ref[...] * (g * jax.nn.sigmoid(g))


def _hgrn2_mixer(z, lb, norm_w):
    bsz, t, _ = z.shape
    return pl.pallas_call(
        functools.partial(_hgrn2_body, t_len=t),
        grid=(bsz,),
        in_specs=[pl.BlockSpec((1, t, 5 * GROUP_W), lambda b: (b, 0, 0)),
                  pl.BlockSpec((2, GROUP_W), lambda b: (0, 0)),
                  pl.BlockSpec((1, GROUP_W), lambda b: (0, 0))],
        out_specs=pl.BlockSpec((1, t, GROUP_W), lambda b: (b, 0, 0)),
        out_shape=jax.ShapeDtypeStruct((bsz, t, GROUP_W), F32),
        scratch_shapes=[pltpu.VMEM((2, GROUP_W // LANES, t, LANES), F32), pltpu.VMEM((2, GROUP_W // LANES, t, LANES), F32),
                        pltpu.VMEM((GROUP_W // LANES, t, LANES), F32),
                        pltpu.VMEM((2, t, GROUP_W), F32), pltpu.VMEM((2, GROUP_W, GROUP_W), F32)],
        compiler_params=_cparams("arbitrary"),
    )(z, lb.astype(F32), norm_w.astype(F32).reshape(1, GROUP_W))


def _lane_expand(cols, seg):
    lane = lax.broadcasted_iota(jnp.int32, (1, len(cols) * seg), 1)
    out = cols[-1]
    for h in range(len(cols) - 2, -1, -1):
        out = jnp.where(lane < (h + 1) * seg, cols[h], out)
    return out


def _mlstm_body(q_ref, k_ref, v_ref, o_ref, zg_ref, gr_ref, nw_ref, out_ref,
                bce, lice, brow, hsc, cst, nst, *, t_len):
    n_all = t_len // ML_CHUNK
    n_ctx = CTX_LEN // ML_CHUNK
    g = zg_ref[0]
    lf = jax.nn.log_sigmoid(g)
    src = lax.broadcasted_iota(jnp.int32, (LANES, GROUP_W), 0)
    dst_head = lax.broadcasted_iota(jnp.int32, (LANES, GROUP_W), 1) // ML_DK
    bd, _, _ = _head_blocks(GROUP_W, ML_DK)
    ebd = jnp.where(bd, 1.0, 0.0).astype(BF16)
    for d in range(2):
        lice[d] = _dot3(g, jnp.where(src == ML_HEADS * d + dst_head, 1.0, 0.0).astype(BF16))
        lf_exp = _dot3(lf, jnp.where(src == ML_HEADS * (2 + d) + dst_head, 1.0, 0.0).astype(BF16))
        for j, blk in enumerate(_chunk_cumsum_cols(lf_exp, ML_CHUNK, d == 1)):
            bce[d, j * ROW_TILE:(j + 1) * ROW_TILE, :] = blk
        _, ri, ci = _head_blocks(GROUP_W, ML_CHUNK)
        tri = jnp.where(bd & ((ri >= ci) if d else (ri <= ci)), 1.0, 0.0).astype(BF16)
        brow[d] = _dot3(jax.nn.log_sigmoid(gr_ref[0, 2 + d]), tri)
    cst[...] = jnp.zeros(cst.shape, F32)
    nst[...] = jnp.zeros(nst.shape, F32)
    tt = lax.broadcasted_iota(jnp.int32, (ML_CHUNK, GROUP_W), 0)
    ss = lax.broadcasted_iota(jnp.int32, (ML_CHUNK, GROUP_W), 1) % ML_CHUNK
    lane_h = lax.broadcasted_iota(jnp.int32, (1, LANES), 1) < ML_CHUNK

    def chunk(d, c, m_prev):
        rows = pl.ds(pl.multiple_of(c * ML_CHUNK, ML_CHUNK), ML_CHUNK)
        q = q_ref[0, rows, :]
        k = k_ref[0, rows, :] * (ML_DK ** -0.5)
        v = v_ref[0, rows, :]
        bc = bce[d, rows, :]
        lic = lice[d, rows, :]
        br = brow[d, pl.ds(c, 1), :]
        lir = gr_ref[0, d, pl.ds(c, 1), :]
        b_end = bc[0:1] if d else bc[ML_CHUNK - 1:ML_CHUNK]
        valid = (ss >= tt) if d else (ss <= tt)
        d_log = jnp.where(valid, bc - br + lir, -jnp.inf)
        cols = []
        for hp in range(GROUP_W // LANES):
            xs = d_log[:, hp * LANES:(hp + 1) * LANES]
            cols.append(jnp.max(jnp.where(lane_h, xs, -jnp.inf), axis=-1, keepdims=True))
            cols.append(jnp.max(jnp.where(lane_h, -jnp.inf, xs), axis=-1, keepdims=True))
        w_inter = bc + m_prev
        m_t = jnp.maximum(w_inter, _lane_expand(cols, ML_CHUNK))
        e_inter = jnp.exp(w_inter - m_t)
        qb = q.astype(BF16)
        kbd = jnp.where(bd, jnp.concatenate([k] * ML_HEADS, axis=0).T, 0.0).astype(BF16)
        vbd = jnp.where(bd, jnp.concatenate([v] * ML_HEADS, axis=0), 0.0).astype(BF16)
        wts = jnp.exp(d_log - m_t) * jnp.dot(qb, kbd, preferred_element_type=F32)
        w_hi = wts.astype(BF16)
        w_lo = (wts - w_hi.astype(F32)).astype(BF16)
        c_prev, n_prev = cst[d], nst[d]
        num = (jnp.dot(w_hi, vbd, preferred_element_type=F32)
               + e_inter * jnp.dot(qb, c_prev.astype(BF16), preferred_element_type=F32))
        den = (jnp.dot(w_hi, ebd, preferred_element_type=F32) + jnp.dot(w_lo, ebd, preferred_element_type=F32)
               + e_inter * jnp.dot(qb, n_prev.astype(BF16), preferred_element_type=F32))
        hsc[d, rows, :] = num / jnp.maximum(jnp.abs(den), jnp.exp(-m_t))
        w_end = b_end - bc + lic
        m_loc = jnp.max(w_end, axis=0, keepdims=True)
        e_end = jnp.exp(w_end - m_loc)
        m_new = jnp.maximum(b_end + m_prev, m_loc)
        a_old = jnp.exp(b_end + m_prev - m_new)
        a_loc = jnp.exp(m_loc - m_new)
        evbd = jnp.where(bd, jnp.concatenate([e_end * v] * ML_HEADS, axis=0), 0.0).astype(BF16)
        eebd = jnp.where(bd, jnp.concatenate([e_end] * ML_HEADS, axis=0), 0.0).astype(BF16)
        cst[d] = a_old * c_prev + a_loc * jnp.dot(kbd, evbd, preferred_element_type=F32)
        nst[d] = a_old * n_prev + a_loc * jnp.dot(kbd, eebd, preferred_element_type=F32)
        return m_new

    def step(i, carry):
        return chunk(0, i, carry[0]), chunk(1, _bwd_chunk(i, n_ctx, n_all), carry[1])

    zero = jnp.zeros((1, GROUP_W), F32)
    lax.fori_loop(0, n_all, step, (zero, zero))
    out_ref[0] = _head_rms(hsc[0] + hsc[1], ebd, ML_DV) * nw_ref[...] * jax.nn.sigmoid(o_ref[0])


def _mlstm_mixer(z, gates_t, norm_w):
    bsz, t, _ = z.shape
    nc = t // ML_CHUNK
    ncp = -(-nc // 8) * 8
    gr = gates_t.reshape(bsz, 4, ML_HEADS, nc, ML_CHUNK).transpose(0, 1, 3, 2, 4).reshape(bsz, 4, nc, GROUP_W)
    gr = jnp.pad(gr, ((0, 0), (0, 0), (0, ncp - nc), (0, 0)))
    col = ML_OFF // GROUP_W
    zspec = lambda j: pl.BlockSpec((1, t, GROUP_W), lambda b: (b, 0, col + j))
    return pl.pallas_call(
        functools.partial(_mlstm_body, t_len=t),
        grid=(bsz,),
        in_specs=[zspec(0), zspec(1), zspec(2), zspec(3),
                  pl.BlockSpec((1, t, LANES), lambda b: (b, 0, ML_GATE_OFF // LANES)),
                  pl.BlockSpec((1, 4, ncp, GROUP_W), lambda b: (b, 0, 0, 0)),
                  pl.BlockSpec((1, GROUP_W), lambda b: (0, 0))],
        out_specs=pl.BlockSpec((1, t, GROUP_W), lambda b: (b, 0, 0)),
        out_shape=jax.ShapeDtypeStruct((bsz, t, GROUP_W), F32),
        scratch_shapes=[pltpu.VMEM((2, t, GROUP_W), F32), pltpu.VMEM((2, t, GROUP_W), F32),
                        pltpu.VMEM((2, ncp, GROUP_W), F32), pltpu.VMEM((2, t, GROUP_W), F32),
                        pltpu.VMEM((2, GROUP_W, GROUP_W), F32), pltpu.VMEM((2, GROUP_W, GROUP_W), F32)],
        compiler_params=_cparams("arbitrary"),
    )(z, z, z, z, z, gr, norm_w.astype(F32).reshape(1, GROUP_W))


def _final_norm_body(x_ref, w_ref, o_ref):
    o_ref[0] = _rms(x_ref[0]) * w_ref[...]


def _final_norm(h, w):
    bsz, t, d = h.shape
    n = t - CTX_LEN
    blk0 = CTX_LEN // ROW_TILE
    return pl.pallas_call(
        _final_norm_body,
        grid=(bsz, n // ROW_TILE),
        in_specs=[pl.BlockSpec((1, ROW_TILE, d), lambda b, j: (b, blk0 + j, 0)),
                  pl.BlockSpec((1, d), lambda b, j: (0, 0))],
        out_specs=pl.BlockSpec((1, ROW_TILE, d), lambda b, j: (b, j, 0)),
        out_shape=jax.ShapeDtypeStruct((bsz, n, d), F32),
        compiler_params=_cparams("arbitrary", "arbitrary"),
    )(h, w.reshape(1, d))


def kernel(x, c, ctx, c_ctx, mod_w, mod_b, norm1_w, norm2_w, w_in, b_in, hg_lb_logits, hg_norm_w,
           s5_lam_re, s5_lam_im, s5_log_dt, s5_b_re, s5_b_im, s5_c_re, s5_c_im, s5_d, s5_glu_w, s5_glu_b,
           da_lq1, da_lk1, da_lq2, da_lk2, da_norm_w, ml_norm_w, w_out, router_w,
           exp_w1, exp_w3, exp_w2, final_norm_w):
    bsz, seq, d = x.shape
    assert ctx.shape[1] == CTX_LEN == ROW_TILE and seq % ROW_TILE == 0 and d == D_MODEL
    t = CTX_LEN + seq
    lb_all = jnp.cumsum(jax.nn.softmax(hg_lb_logits.astype(F32), axis=0), axis=0)
    lb_all = lb_all - lb_all[0]

    n_rows = -(-(bsz + 1) // 8) * 8
    cv = jnp.zeros((n_rows, d), F32).at[:bsz].set(c).at[bsz].set(c_ctx)
    mod_all = _modulation(cv, mod_w, mod_b)
    cos_t, sin_t = _rope_tables(t)
    h = jnp.concatenate([ctx, x], axis=1)

    for li in range(DEPTH):
        ctx_out = li < DEPTH - 1
        m6 = mod_all[li].reshape(n_rows, 6, d)
        modsel = jnp.stack([jnp.broadcast_to(m6[bsz], (bsz, 6, d)), m6[:bsz]], axis=1)
        w_pad = jnp.pad(w_in[li], ((0, 0), (0, IN_COLS_PAD - IN_COLS))).astype(BF16)
        b_pad = jnp.pad(b_in[li], (0, IN_COLS_PAD - IN_COLS)).reshape(1, IN_COLS_PAD)
        wg_t = w_in[li][:, ML_GATE_OFF:].T.astype(BF16)
        bg_col = b_in[li][ML_GATE_OFF:].reshape(N_GATES, 1)
        z, gates_t = _in_projection(h, modsel, norm1_w[li], w_pad, b_pad, wg_t, bg_col)

        mix_a = _hgrn2_mixer(z, lb_all[li], hg_norm_w[li])
        mix_b = _s5_mixer(z, s5_lam_re[li], s5_lam_im[li], s5_log_dt[li], s5_b_re[li], s5_b_im[li],
                          s5_c_re[li], s5_c_im[li], s5_d[li], s5_glu_w[li], s5_glu_b[li])
        lam_init = 0.8 - 0.6 * math.exp(-0.3 * li)
        att_args = (da_lq1[li], da_lk1[li], da_lq2[li], da_lk2[li], da_norm_w[li], lam_init)
        c_l = _diff_attention(z, CTX_LEN, seq, t, True, cos_t, sin_t, *att_args)
        if ctx_out:
            c_c = _diff_attention(z, 0, CTX_LEN, CTX_LEN, False, cos_t, sin_t, *att_args)
        else:
            c_c = jnp.zeros((bsz, CTX_LEN, GROUP_W), F32)
        mix_c = jnp.concatenate([c_c, c_l], axis=1)
        mix_d = _mlstm_mixer(z, gates_t, ml_norm_w[li])

        hm, xm, aff_t = _out_projection((mix_a, mix_b, mix_c, mix_d), h, modsel, norm2_w[li],
                                        w_out[li].astype(BF16), router_w[li].T.astype(BF16))
        w1, w3, w2 = exp_w1[li].astype(BF16), exp_w3[li].astype(BF16), exp_w2[li].astype(BF16)
        slot, slot_t = _route(aff_t, CTX_LEN, seq)
        y = _moe_ffn(xm, slot, aff_t, CTX_LEN, seq, 1, w1, w3, w2)
        if not ctx_out:
            return _moe_combine(y, slot_t, hm, modsel, CTX_LEN, seq, 1, final_w=final_norm_w)
        h = _moe_combine(y, slot_t, hm, modsel, CTX_LEN, seq, 1)
        slot, slot_t = _route(aff_t, 0, CTX_LEN)
        y = _moe_ffn(xm, slot, aff_t, 0, CTX_LEN, bsz, w1, w3, w2)
        h = _moe_combine(y, slot_t, h, modsel, 0, CTX_LEN, 0)
```

```python
import functools
import math
import jax
import jax.numpy as jnp
from jax import lax
from jax.experimental import pallas as pl
from jax.experimental.pallas import tpu as pltpu

D_MODEL = 1024
DEPTH = 2
CTX_LEN = 256
GRID_W = 64
N_MIXERS = 4
GROUP_W = D_MODEL // N_MIXERS
MIX_W = N_MIXERS * GROUP_W
EPS = 1e-6
F32 = jnp.float32
BF16 = jnp.bfloat16
HG_HEADS = 4
HG_DK = GROUP_W // HG_HEADS
HG_DV = GROUP_W // HG_HEADS
HG_CHUNK = 16
S5_CH = 16
S5_GROUPS = GROUP_W // S5_CH
S5_STATE = 64
DA_HEADS = 4
DA_DQK = GROUP_W // (2 * DA_HEADS)
DA_DV = GROUP_W // DA_HEADS
ROPE_BASE = 10000.0
ML_HEADS = 4
ML_DK = GROUP_W // ML_HEADS
ML_DV = GROUP_W // ML_HEADS
ML_CHUNK = 64
N_EXPERTS = 16
EC_CAPACITY = 2
D_EXPERT = 2 * D_MODEL
HG_OFF = 0
S5_OFF = HG_OFF + 5 * GROUP_W
DA_OFF = S5_OFF + GROUP_W
ML_OFF = DA_OFF + 3 * GROUP_W
ML_GATE_OFF = ML_OFF + 4 * GROUP_W
IN_COLS = ML_GATE_OFF + 4 * ML_HEADS

LANES = 128
ROW_TILE = 256
IN_COLS_PAD = ML_GATE_OFF + LANES
N_GATES = 4 * ML_HEADS
S5_LANES = S5_GROUPS * S5_STATE
S5_HALF = S5_LANES // 2
S5_SEGS = 8
VMEM_LIMIT = 56 * 1024 * 1024
LOG2_E = math.log2(math.e)


def _cparams(*sem):
    return pltpu.CompilerParams(dimension_semantics=sem, vmem_limit_bytes=VMEM_LIMIT)


def _rms(x):
    return x * lax.rsqrt(jnp.mean(x * x, axis=-1, keepdims=True) + EPS)


def _mod_body(cv_ref, w_ref, b_ref, o_ref):
    cv = cv_ref[...]
    s = cv * jax.nn.sigmoid(cv)
    o_ref[0] = jnp.dot(s.astype(BF16), w_ref[0].astype(BF16), preferred_element_type=F32) + b_ref[0]


def _modulation(cv, mod_w, mod_b):
    n_l, d, n6 = mod_w.shape
    r = cv.shape[0]
    tn = n6 // 4
    return pl.pallas_call(
        _mod_body,
        grid=(n_l, 4),
        in_specs=[pl.BlockSpec((r, d), lambda l, j: (0, 0)),
                  pl.BlockSpec((1, d, tn), lambda l, j: (l, 0, j)),
                  pl.BlockSpec((1, 1, tn), lambda l, j: (l, 0, j))],
        out_specs=pl.BlockSpec((1, r, tn), lambda l, j: (l, 0, j)),
        out_shape=jax.ShapeDtypeStruct((n_l, r, n6), F32),
        compiler_params=_cparams("arbitrary", "arbitrary"),
    )(cv, mod_w, mod_b.reshape(n_l, 1, n6))


def _inproj_body(h_ref, m_ref, nw_ref, w_ref, b_ref, wg_ref, bg_ref, z_ref, gt_ref):
    m = m_ref[0, 0]
    xn = _rms(h_ref[0]) * nw_ref[...] * (1.0 + m[1:2]) + m[0:1]
    xb = xn.astype(BF16)
    z_ref[0] = jnp.dot(xb, w_ref[...], preferred_element_type=F32) + b_ref[...]
    gt_ref[0] = lax.dot_general(wg_ref[...], xb, (((1,), (1,)), ((), ())),
                                preferred_element_type=F32) + bg_ref[...]


def _in_projection(h, modsel, norm_w, w_pad, b_pad, wg_t, bg_col):
    bsz, t, d = h.shape
    nt = t // ROW_TILE
    return pl.pallas_call(
        _inproj_body,
        grid=(bsz, nt),
        in_specs=[pl.BlockSpec((1, ROW_TILE, d), lambda b, j: (b, j, 0)),
                  pl.BlockSpec((1, 1, 6, d), lambda b, j: (b, jnp.minimum(j, 1), 0, 0)),
                  pl.BlockSpec((1, d), lambda b, j: (0, 0)),
                  pl.BlockSpec((d, IN_COLS_PAD), lambda b, j: (0, 0)),
                  pl.BlockSpec((1, IN_COLS_PAD), lambda b, j: (0, 0)),
                  pl.BlockSpec((N_GATES, d), lambda b, j: (0, 0)),
                  pl.BlockSpec((N_GATES, 1), lambda b, j: (0, 0))],
        out_specs=[pl.BlockSpec((1, ROW_TILE, IN_COLS_PAD), lambda b, j: (b, j, 0)),
                   pl.BlockSpec((1, N_GATES, ROW_TILE), lambda b, j: (b, 0, j))],
        out_shape=[jax.ShapeDtypeStruct((bsz, t, IN_COLS_PAD), F32),
                   jax.ShapeDtypeStruct((bsz, N_GATES, t), F32)],
        compiler_params=_cparams("arbitrary", "arbitrary"),
    )(h, modsel, norm_w.reshape(1, d), w_pad, b_pad, wg_t, bg_col)


def _outproj_body(a_ref, b_ref, c_ref, d_ref, h_ref, m_ref, nw_ref, wo_ref, rw_ref, hm_ref, xm_ref, aff_ref):
    m = m_ref[0, 0]
    y = jnp.concatenate([a_ref[0], b_ref[0], c_ref[0], d_ref[0]], axis=-1).astype(BF16)
    hm = h_ref[0] + m[2:3] * jnp.dot(y, wo_ref[...], preferred_element_type=F32)
    hm_ref[0] = hm
    xm = (_rms(hm) * nw_ref[...] * (1.0 + m[4:5]) + m[3:4]).astype(BF16)
    xm_ref[0] = xm
    logit = lax.dot_general(rw_ref[...], xm, (((1,), (1,)), ((), ())), preferred_element_type=F32)
    e = jnp.exp(logit - jnp.max(logit, axis=0, keepdims=True))
    aff_ref[0] = e / jnp.sum(e, axis=0, keepdims=True)


def _out_projection(mix, h, modsel, norm_w, wo_bf, rw_t):
    bsz, t, d = h.shape
    nt = t // ROW_TILE
    mix_spec = pl.BlockSpec((1, ROW_TILE, GROUP_W), lambda b, j: (b, j, 0))
    row_spec = pl.BlockSpec((1, ROW_TILE, d), lambda b, j: (b, j, 0))
    return pl.pallas_call(
        _outproj_body,
        grid=(bsz, nt),
        in_specs=[mix_spec, mix_spec, mix_spec, mix_spec, row_spec,
                  pl.BlockSpec((1, 1, 6, d), lambda b, j: (b, jnp.minimum(j, 1), 0, 0)),
                  pl.BlockSpec((1, d), lambda b, j: (0, 0)),
                  pl.BlockSpec((MIX_W, d), lambda b, j: (0, 0)),
                  pl.BlockSpec((N_EXPERTS, d), lambda b, j: (0, 0))],
        out_specs=[row_spec, row_spec, pl.BlockSpec((1, N_EXPERTS, ROW_TILE), lambda b, j: (b, 0, j))],
        out_shape=[jax.ShapeDtypeStruct((bsz, t, d), F32), jax.ShapeDtypeStruct((bsz, t, d), BF16),
                   jax.ShapeDtypeStruct((bsz, N_EXPERTS, t), F32)],
        compiler_params=_cparams("arbitrary", "arbitrary"),
    )(*mix, h, modsel, norm_w.reshape(1, d), wo_bf, rw_t)


def _topk_body(aff_ref, slot_ref, slot_t_ref, *, cap, n, row0):
    aff = aff_ref[0, :, row0:row0 + n]
    bits = pltpu.bitcast(aff, jnp.int32)
    thr = jnp.zeros((N_EXPERTS, 1), jnp.int32)
    for bit in range(30, -1, -1):
        cand = thr | (1 << bit)
        cnt = jnp.sum(jnp.where(bits >= cand, 1.0, 0.0), axis=-1, keepdims=True)
        thr = jnp.where(cnt >= cap, cand, thr)
    room = cap - jnp.sum(jnp.where(bits > thr, 1.0, 0.0), axis=-1, keepdims=True)
    ri = lax.broadcasted_iota(jnp.int32, (LANES, LANES), 0)
    ci = lax.broadcasted_iota(jnp.int32, (LANES, LANES), 1)
    incl = jnp.where(ri <= ci, 1.0, 0.0).astype(BF16)
    off_eq = jnp.zeros((N_EXPERTS, 1), F32)
    off_sel = jnp.zeros((N_EXPERTS, 1), F32)
    pieces = []
    for j in range(n // LANES):
        sl = slice(j * LANES, (j + 1) * LANES)
        bits_b = bits[:, sl]
        eq_b = jnp.where(bits_b == thr, 1.0, 0.0)
        pos_eq = jnp.dot(eq_b.astype(BF16), incl, preferred_element_type=F32) - eq_b + off_eq
        sel = jnp.where(bits_b > thr, 1.0, jnp.where(pos_eq < room, eq_b, 0.0))
        pos_sel = jnp.dot(sel.astype(BF16), incl, preferred_element_type=F32) - sel + off_sel
        piece = jnp.where(sel > 0.5, pos_sel, -1.0)
        slot_ref[0, :, sl] = piece.astype(jnp.int32)
        pieces.append(piece)
        off_eq = off_eq + jnp.sum(eq_b, axis=-1, keepdims=True)
        off_sel = off_sel + jnp.sum(sel, axis=-1, keepdims=True)
    pad = jnp.full((LANES - N_EXPERTS, LANES), -1.0, F32)
    for j in range(n // LANES):
        tile = jnp.concatenate([pieces[j], pad], axis=0)
        slot_t_ref[0, j * LANES:(j + 1) * LANES, :] = tile.T.astype(jnp.int32)


def _route(aff_t, row0, n):
    bsz, _, t = aff_t.shape
    cap = EC_CAPACITY * n // N_EXPERTS
    return pl.pallas_call(
        functools.partial(_topk_body, cap=cap, n=n, row0=row0),
        grid=(bsz,),
        in_specs=[pl.BlockSpec((1, N_EXPERTS, t), lambda b: (b, 0, 0))],
        out_specs=[pl.BlockSpec((1, N_EXPERTS, n), lambda b: (b, 0, 0)),
                   pl.BlockSpec((1, n, LANES), lambda b: (b, 0, 0))],
        out_shape=[jax.ShapeDtypeStruct((bsz, N_EXPERTS, n), jnp.int32),
                   jax.ShapeDtypeStruct((bsz, n, LANES), jnp.int32)],
        compiler_params=_cparams("arbitrary"),
    )(aff_t)


def _moe_ffn_body(x_ref, slot_ref, aff_ref, w1_ref, w3_ref, w2_ref, y_ref, xe_scr, g_scr, *, nb, cap, n, off):
    for i in range(nb):
        hit = lax.broadcasted_iota(jnp.int32, (cap, n), 0) == slot_ref[i, 0]
        onehot = jnp.where(hit, 1.0, 0.0).astype(BF16)
        xe_scr[i * cap:(i + 1) * cap, :] = jnp.dot(onehot, x_ref[i, off:off + n, :],
                                                   preferred_element_type=F32).astype(BF16)
        g_scr[i * cap:(i + 1) * cap, :] = jnp.sum(jnp.where(hit, aff_ref[i, 0, :, off:off + n], 0.0),
                                                  axis=-1, keepdims=True)
    xe = xe_scr[...]
    h1 = jnp.dot(xe, w1_ref[0], preferred_element_type=F32)
    h3 = jnp.dot(xe, w3_ref[0], preferred_element_type=F32)
    act = (h1 * jax.nn.sigmoid(h1) * h3).astype(BF16)
    y = (jnp.dot(act, w2_ref[0], preferred_element_type=F32) * g_scr[...]).astype(BF16)
    for i in range(nb):
        y_ref[i, 0] = y[i * cap:(i + 1) * cap]


def _moe_ffn(xm, slot, aff_t, row0, n, nb, w1, w3, w2):
    bsz, t, d = xm.shape
    cap = EC_CAPACITY * n // N_EXPERTS
    rows, off = (n, 0) if row0 % n == 0 else (t, row0)
    blk = row0 // n if off == 0 else 0
    slot4 = slot.reshape(bsz, N_EXPERTS, 1, n)
    aff4 = aff_t.reshape(bsz, N_EXPERTS, 1, t)
    y_spec = pl.BlockSpec((nb, 1, cap, d), lambda e, b: (b, e, 0, 0))
    y_shape = jax.ShapeDtypeStruct((bsz, N_EXPERTS, cap, d), BF16)
    return pl.pallas_call(
        functools.partial(_moe_ffn_body, nb=nb, cap=cap, n=n, off=off),
        grid=(N_EXPERTS, bsz // nb),
        in_specs=[pl.BlockSpec((nb, rows, d), lambda e, b: (b, blk, 0)),
                  pl.BlockSpec((nb, 1, 1, n), lambda e, b: (b, e, 0, 0)),
                  pl.BlockSpec((nb, 1, 1, rows), lambda e, b: (b, e, 0, blk)),
                  pl.BlockSpec((1, d, D_EXPERT), lambda e, b: (e, 0, 0)),
                  pl.BlockSpec((1, d, D_EXPERT), lambda e, b: (e, 0, 0)),
                  pl.BlockSpec((1, D_EXPERT, d), lambda e, b: (e, 0, 0))],
        out_specs=y_spec,
        out_shape=y_shape,
        scratch_shapes=[pltpu.VMEM((nb * cap, d), BF16), pltpu.VMEM((nb * cap, 1), F32)],
        compiler_params=_cparams("arbitrary", "arbitrary"),
    )(xm, slot4, aff4, w1, w3, w2)


def _moe_combine_body(y_ref, st_ref, hm_ref, m_ref, fw_ref, o_ref, *, cap, final):
    st = st_ref[0]
    lane = lax.broadcasted_iota(jnp.int32, (st.shape[0], cap), 1)
    hots = [jnp.where(lane == st[:, e:e + 1], 1.0, 0.0).astype(BF16) for e in range(N_EXPERTS)]
    if cap % LANES == 0:
        acc = jnp.dot(jnp.concatenate(hots, axis=1), y_ref[0].reshape(N_EXPERTS * cap, y_ref.shape[-1]),
                      preferred_element_type=F32)
    else:
        acc = sum(jnp.dot(hots[e], y_ref[0, e], preferred_element_type=F32) for e in range(N_EXPERTS))
    h = hm_ref[0] + m_ref[0, 0][5:6] * acc
    o_ref[0] = _rms(h) * fw_ref[...] if final else h


def _moe_combine(y, slot_t, hm, modsel, row0, n, is_latent, final_w=None):
    bsz, _, cap, d = y.shape
    nt = n // ROW_TILE
    blk0 = row0 // ROW_TILE
    final = final_w is not None
    fw = (final_w if final else jnp.ones((d,), F32)).reshape(1, d)
    return pl.pallas_call(
        functools.partial(_moe_combine_body, cap=cap, final=final),
        grid=(bsz, nt),
        in_specs=[pl.BlockSpec((1, N_EXPERTS, cap, d), lambda b, j: (b, 0, 0, 0)),
                  pl.BlockSpec((1, ROW_TILE, LANES), lambda b, j: (b, j, 0)),
                  pl.BlockSpec((1, ROW_TILE, d), lambda b, j: (b, blk0 + j, 0)),
                  pl.BlockSpec((1, 1, 6, d), lambda b, j: (b, is_latent, 0, 0)),
                  pl.BlockSpec((1, d), lambda b, j: (0, 0))],
        out_specs=pl.BlockSpec((1, ROW_TILE, d), lambda b, j: (b, j if final else blk0 + j, 0)),
        out_shape=jax.ShapeDtypeStruct((bsz, n, d) if final else hm.shape, F32),
        input_output_aliases={} if final else {2: 0},
        compiler_params=_cparams("arbitrary", "arbitrary"),
    )(y, slot_t, hm, modsel, fw)


def _swap_pairs(x):
    w = x.shape[-1]
    lane = lax.broadcasted_iota(jnp.int32, x.shape, x.ndim - 1)
    return jnp.where(lane % 2 == 0, pltpu.roll(x, w - 1, x.ndim - 1), pltpu.roll(x, 1, x.ndim - 1))


def _attn_body(q_ref, k_ref, v_ref, cq_ref, sq_ref, ck_ref, sk_ref, lq1_ref, lk1_ref, lq2_ref, lk2_ref, nw_ref,
               o_ref, k_scr, vt_scr, *, rope, lam_init):
    @pl.when(pl.program_id(1) == 0)
    def _():
        k = k_ref[0]
        if rope:
            k = k * ck_ref[...] + _swap_pairs(k) * sk_ref[...]
        k_scr[...] = k.astype(BF16)
        for r in range(0, k_scr.shape[0], ROW_TILE):
            vt_scr[:, r:r + ROW_TILE] = v_ref[0, r:r + ROW_TILE, :].T.astype(BF16)

    q = q_ref[0]
    if rope:
        q = q * cq_ref[...] + _swap_pairs(q) * sq_ref[...]
    q_t = (q * (DA_DQK ** -0.5 * LOG2_E)).T
    lam = (jnp.exp(jnp.sum(lq1_ref[...] * lk1_ref[...], axis=-1, keepdims=True))
           - jnp.exp(jnp.sum(lq2_ref[...] * lk2_ref[...], axis=-1, keepdims=True)) + lam_init)
    row = lax.broadcasted_iota(jnp.int32, (GROUP_W, 1), 0)
    kk = k_scr[...]
    outs = []
    for h in range(DA_HEADS):
        qm = jnp.concatenate([jnp.where(row // DA_DQK == 2 * h + mi, q_t, 0.0) for mi in range(2)], axis=1)
        s = jnp.dot(kk, qm.astype(BF16), preferred_element_type=F32)
        e = jnp.exp2(s - jnp.max(s, axis=0, keepdims=True))
        pv = (jnp.dot(vt_scr[h * DA_DV:(h + 1) * DA_DV, :], e.astype(BF16), preferred_element_type=F32)
              / jnp.sum(e, axis=0, keepdims=True))
        nq = q_t.shape[1]
        oh = pv[:, :nq] - lam * pv[:, nq:]
        outs.append(oh * lax.rsqrt(jnp.mean(oh * oh, axis=0, keepdims=True) + EPS))
    o_ref[0] = jnp.concatenate(outs, axis=0).T * nw_ref[...] * (1.0 - lam_init)


def _diff_attention(z, q_row0, nq, nk, rope, cos_t, sin_t, lq1, lk1, lq2, lk2, norm_w, lam_init):
    bsz = z.shape[0]
    qb = ROW_TILE
    q0 = q_row0 // qb
    col = DA_OFF // GROUP_W
    vec = lambda a: a.reshape(1, -1)
    small = pl.BlockSpec((1, DA_DQK), lambda b, j: (0, 0))
    return pl.pallas_call(
        functools.partial(_attn_body, rope=rope, lam_init=lam_init),
        grid=(bsz, nq // qb),
        in_specs=[pl.BlockSpec((1, qb, GROUP_W), lambda b, j: (b, q0 + j, col)),
                  pl.BlockSpec((1, nk, GROUP_W), lambda b, j: (b, 0, col + 1)),
                  pl.BlockSpec((1, nk, GROUP_W), lambda b, j: (b, 0, col + 2)),
                  pl.BlockSpec((qb, GROUP_W), lambda b, j: (q0 + j, 0)),
                  pl.BlockSpec((qb, GROUP_W), lambda b, j: (q0 + j, 0)),
                  pl.BlockSpec((nk, GROUP_W), lambda b, j: (0, 0)),
                  pl.BlockSpec((nk, GROUP_W), lambda b, j: (0, 0)),
                  small, small, small, small,
                  pl.BlockSpec((1, GROUP_W), lambda b, j: (0, 0))],
        out_specs=pl.BlockSpec((1, qb, GROUP_W), lambda b, j: (b, j, 0)),
        out_shape=jax.ShapeDtypeStruct((bsz, nq, GROUP_W), F32),
        scratch_shapes=[pltpu.VMEM((nk, GROUP_W), BF16), pltpu.VMEM((GROUP_W, nk), BF16)],
        compiler_params=_cparams("arbitrary", "arbitrary"),
    )(z, z, z, cos_t, sin_t, cos_t, sin_t, vec(lq1), vec(lk1), vec(lq2), vec(lk2), vec(norm_w))


def _rope_tables(t):
    n = t - CTX_LEN
    axis_dim = DA_DQK // 2
    inv = ROPE_BASE ** (-jnp.arange(0, axis_dim, 2, dtype=F32) / axis_dim)
    tok = jnp.arange(n, dtype=jnp.int32)
    row = (tok // GRID_W).astype(F32)
    colp = (tok % GRID_W).astype(F32)
    ang = jnp.concatenate([row[:, None] * inv, colp[:, None] * inv], axis=-1)
    cos = jnp.repeat(jnp.cos(ang), 2, axis=-1)
    sin = jnp.repeat(jnp.sin(ang), 2, axis=-1) * jnp.tile(jnp.array([-1.0, 1.0], F32), DA_DQK // 2)
    reps = GROUP_W // DA_DQK
    cos = jnp.concatenate([jnp.ones((CTX_LEN, GROUP_W), F32), jnp.tile(cos, (1, reps))], axis=0)
    sin = jnp.concatenate([jnp.zeros((CTX_LEN, GROUP_W), F32), jnp.tile(sin, (1, reps))], axis=0)
    return cos, sin


def _cmul(ar, ai, br, bi):
    return ar * br - ai * bi, ar * bi + ai * br


def _s5_scan_part(bufs, coef, base, seg_len, carry):
    xrf, xif, xrb, xib = bufs
    af_r, af_i, ab_r, ab_i = coef
    assert seg_len & (seg_len - 1) == 0

    def sweep(store, init):
        def step(i, st):
            fr, fi, br, bi = st
            rf = pl.ds(pl.multiple_of(base + S5_SEGS * i, S5_SEGS), S5_SEGS)
            rb = pl.ds(pl.multiple_of(base + S5_SEGS * (seg_len - 1 - i), S5_SEGS), S5_SEGS)
            nfr = af_r * fr - af_i * fi + xrf[rf, :]
            nfi = af_r * fi + af_i * fr + xif[rf, :]
            nbr = ab_r * br - ab_i * bi + xrb[rb, :]
            nbi = ab_r * bi + ab_i * br + xib[rb, :]
            if store:
                xrf[rf, :] = nfr
                xif[rf, :] = nfi
                xrb[rb, :] = nbr
                xib[rb, :] = nbi
            return nfr, nfi, nbr, nbi
        return lax.fori_loop(0, seg_len, step, init, unroll=4)

    zero = jnp.zeros(af_r.shape, F32)
    ef_r, ef_i, eb_r, eb_i = sweep(False, (zero,) * 4)
    pf = (af_r[0:1], af_i[0:1])
    pb = (ab_r[0:1], ab_i[0:1])
    for _ in range(seg_len.bit_length() - 1):
        pf = _cmul(*pf, *pf)
        pb = _cmul(*pb, *pb)
    cf_r, cf_i, cb_r, cb_i = carry
    ins_f, ins_b = [], []
    for j in range(S5_SEGS):
        ins_f.append((cf_r, cf_i))
        gr, gi = _cmul(*pf, cf_r, cf_i)
        cf_r, cf_i = ef_r[j:j + 1] + gr, ef_i[j:j + 1] + gi
    for j in range(S5_SEGS - 1, -1, -1):
        ins_b.append((cb_r, cb_i))
        gr, gi = _cmul(*pb, cb_r, cb_i)
        cb_r, cb_i = eb_r[j:j + 1] + gr, eb_i[j:j + 1] + gi
    ins_b.reverse()
    stack = lambda rows: jnp.concatenate(rows, axis=0)
    sweep(True, (stack([a for a, _ in ins_f]), stack([b for _, b in ins_f]),
                 stack([a for a, _ in ins_b]), stack([b for _, b in ins_b])))
    return cf_r, cf_i, cb_r, cb_i


def _seg_interleave(x):
    b, r, c = x.shape
    return x.reshape(b, S5_SEGS, r // S5_SEGS, c).transpose(0, 2, 1, 3).reshape(b, r, c)


def _seg_deinterleave(x):
    b, r, c = x.shape
    return x.reshape(b, r // S5_SEGS, S5_SEGS, c).transpose(0, 2, 1, 3).reshape(b, r, c)


def _s5_body(u_ref, bre_ref, bim_ref, cre_ref, cim_ref, disc_ref, dsk_ref, gw_ref, gb_ref, o_ref,
             xrf, xif, xrb, xib, y_scr, *, t_len):
    u = u_ref[0]
    ub = u.astype(BF16)
    y_scr[...] = u * dsk_ref[...]
    n_lat = t_len - CTX_LEN
    for half in range(2):
        ls = slice(half * S5_HALF, (half + 1) * S5_HALF)
        bu_r = jnp.dot(ub, bre_ref[:, ls], preferred_element_type=F32)
        bu_i = jnp.dot(ub, bim_ref[:, ls], preferred_element_type=F32)
        disc = disc_ref[:, ls]
        xrf[...] = disc[2:3] * bu_r - disc[3:4] * bu_i
        xif[...] = disc[2:3] * bu_i + disc[3:4] * bu_r
        xrb[...] = disc[6:7] * bu_r - disc[7:8] * bu_i
        xib[...] = disc[6:7] * bu_i + disc[7:8] * bu_r
        coef = tuple(jnp.broadcast_to(disc[r:r + 1], (S5_SEGS, S5_HALF)) for r in (0, 1, 4, 5))
        zero = jnp.zeros((1, S5_HALF), F32)
        carry = _s5_scan_part((xrf, xif, xrb, xib), coef, 0, CTX_LEN // S5_SEGS, (zero,) * 4)
        _s5_scan_part((xrf, xif, xrb, xib), coef, CTX_LEN, n_lat // S5_SEGS, carry)
        cre = cre_ref[ls, :]
        cim = cim_ref[ls, :]
        y_scr[...] += (jnp.dot(xrf[...].astype(BF16), cre, preferred_element_type=F32)
                       - jnp.dot(xif[...].astype(BF16), cim, preferred_element_type=F32)
                       + jnp.dot(xrb[...].astype(BF16), cre, preferred_element_type=F32)
                       - jnp.dot(xib[...].astype(BF16), cim, preferred_element_type=F32))
    y = jax.nn.gelu(y_scr[...])
    gate = jax.nn.sigmoid(jnp.dot(y.astype(BF16), gw_ref[...], preferred_element_type=F32) + gb_ref[...])
    o_ref[0] = y * gate


def _s5_mixer(z, lam_re, lam_im, log_dt, b_re, b_im, c_re, c_im, d_skip, glu_w, glu_b):
    bsz, t, _ = z.shape
    eye = jnp.eye(S5_GROUPS, dtype=F32)
    bbd = lambda w: jnp.einsum('gpc,gh->gchp', w.astype(F32), eye).reshape(GROUP_W, S5_LANES).astype(BF16)
    cbd = lambda w: jnp.einsum('gcp,gh->gphc', w.astype(F32), eye).reshape(S5_LANES, GROUP_W).astype(BF16)
    rows = []
    for d in range(2):
        lr, li = lam_re[d].astype(F32), lam_im[d].astype(F32)
        dt = jnp.exp(log_dt[d].astype(F32))[:, None]
        mag = jnp.exp(lr * dt)
        ab_re, ab_im = mag * jnp.cos(li * dt), mag * jnp.sin(li * dt)
        den = lr * lr + li * li
        co_re = ((ab_re - 1.0) * lr + ab_im * li) / den
        co_im = (ab_im * lr - (ab_re - 1.0) * li) / den
        rows += [ab_re, ab_im, co_re, co_im]
    disc = jnp.stack([r.reshape(S5_LANES) for r in rows], axis=0)
    full = lambda shape: pl.BlockSpec(shape, lambda b: (0,) * len(shape))
    u = z[:, :, S5_OFF:S5_OFF + GROUP_W]
    u = jnp.concatenate([_seg_interleave(u[:, :CTX_LEN]), _seg_interleave(u[:, CTX_LEN:])], axis=1)
    out = pl.pallas_call(
        functools.partial(_s5_body, t_len=t),
        grid=(bsz,),
        in_specs=[pl.BlockSpec((1, t, GROUP_W), lambda b: (b, 0, 0)),
                  full((GROUP_W, S5_LANES)), full((GROUP_W, S5_LANES)),
                  full((S5_LANES, GROUP_W)), full((S5_LANES, GROUP_W)),
                  full((8, S5_LANES)), full((1, GROUP_W)), full((GROUP_W, GROUP_W)), full((1, GROUP_W))],
        out_specs=pl.BlockSpec((1, t, GROUP_W), lambda b: (b, 0, 0)),
        out_shape=jax.ShapeDtypeStruct((bsz, t, GROUP_W), F32),
        scratch_shapes=[pltpu.VMEM((t, S5_HALF), F32)] * 4 + [pltpu.VMEM((t, GROUP_W), F32)],
        compiler_params=_cparams("arbitrary"),
    )(u, bbd(b_re), bbd(b_im), cbd(c_re), cbd(c_im), disc, d_skip.astype(F32).reshape(1, GROUP_W),
      glu_w.astype(BF16), glu_b.astype(F32).reshape(1, GROUP_W))
    return jnp.concatenate([_seg_deinterleave(out[:, :CTX_LEN]), _seg_deinterleave(out[:, CTX_LEN:])], axis=1)


def rms_norm(x, w):
    xf = x.astype(F32)
    y = xf * lax.rsqrt(jnp.mean(xf * xf, axis=-1, keepdims=True) + EPS)
    return (y * w.astype(F32)).astype(x.dtype)


def head_rms_norm(x, w, n_heads):
    shp = x.shape
    xh = x.reshape(shp[:-1] + (n_heads, -1))
    return rms_norm(xh, w.reshape(n_heads, -1)).reshape(shp)


def flip_seq(a, reverse):
    return a[:, ::-1] if reverse else a


def gla_chunked(q, k, v, log_f, s0, with_out=True):
    bsz, t_len, nh, _ = q.shape
    nc = t_len // HG_CHUNK
    def chunks(a):
        return a.astype(F32).reshape(bsz, nc, HG_CHUNK, nh, a.shape[-1])
    q, k, v, log_f = chunks(q), chunks(k), chunks(v), chunks(log_f)
    b = jnp.cumsum(log_f, axis=2)
    b_end = b[:, :, -1]
    ds = jnp.einsum('bcshd,bcshv->bchdv', k * jnp.exp(b_end[:, :, None] - b), v)
    def step(s, inp):
        g_c, ds_c = inp
        return jnp.exp(g_c)[..., None] * s + ds_c, s
    s_fin, s_start = lax.scan(step, s0, (jnp.moveaxis(b_end, 1, 0), jnp.moveaxis(ds, 1, 0)))
    if not with_out:
        return None, s_fin
    s_start = jnp.moveaxis(s_start, 0, 1)
    tri = jnp.tril(jnp.ones((HG_CHUNK, HG_CHUNK), bool))[None, None, :, :, None, None]
    decay = jnp.exp(jnp.where(tri, b[:, :, :, None] - b[:, :, None], -jnp.inf))
    scores = jnp.sum(q[:, :, :, None] * decay * k[:, :, None], axis=-1)
    o = (jnp.einsum('bctsh,bcshv->bcthv', scores, v)
         + jnp.einsum('bcthd,bchdv->bcthv', q * jnp.exp(b), s_start))
    return o.reshape(bsz, t_len, nh, -1), s_fin


def hgrn2_mixer(zc, zl, lb, norm_w, with_ctx_out):
    def heads(a):
        return a.reshape(a.shape[:2] + (HG_HEADS, -1))
    def split(z):
        q, i, ff, fb, g = jnp.split(z, 5, axis=-1)
        return heads(q) * HG_DK ** -0.5, heads(i), (heads(ff), heads(fb)), g
    qc, ic, fc, gc = split(zc)
    ql, il, fl, gl = split(zl)
    oc, ol = 0.0, 0.0
    for d in range(2):
        lbd = lb[d].reshape(HG_HEADS, HG_DK)
        f_c = lbd + (1.0 - lbd) * jax.nn.sigmoid(fc[d].astype(F32))
        f_l = lbd + (1.0 - lbd) * jax.nn.sigmoid(fl[d].astype(F32))
        s0 = jnp.zeros((zc.shape[0], HG_HEADS, HG_DK, HG_DV), F32)
        o_c, s_c = gla_chunked(flip_seq(qc, d), flip_seq(1.0 - f_c, d), flip_seq(ic, d),
                               flip_seq(jnp.log(f_c), d), s0, with_ctx_out)
        o_l, _ = gla_chunked(flip_seq(ql, d), flip_seq(1.0 - f_l, d), flip_seq(il, d),
                             flip_seq(jnp.log(f_l), d), s_c)
        ol = ol + flip_seq(o_l, d)
        if with_ctx_out:
            oc = oc + flip_seq(o_c, d)
    def readout(o, g):
        o = o.reshape(o.shape[:2] + (-1,)).astype(g.dtype)
        return head_rms_norm(o, norm_w, HG_HEADS) * jax.nn.silu(g)
    return (readout(oc, gc) if with_ctx_out else None), readout(ol, gl)


def mlstm_chunked(q, k, v, log_i, log_f, state, with_out=True):
    bsz, t_len, nh, _ = q.shape
    nc = t_len // ML_CHUNK
    def chunks(a):
        return a.astype(F32).reshape((bsz, nc, ML_CHUNK) + a.shape[2:])
    q, k, v, log_i, log_f = chunks(q), chunks(k), chunks(v), chunks(log_i), chunks(log_f)
    b = jnp.cumsum(log_f, axis=2)
    b_end = b[:, :, -1]
    w_end = b_end[:, :, None] - b + log_i
    m_loc = jnp.max(w_end, axis=2)
    e_end = jnp.exp(w_end - m_loc[:, :, None])
    c_loc = jnp.einsum('bcshv,bcshd->bchvd', e_end[..., None] * v, k)
    n_loc = jnp.einsum('bcsh,bcshd->bchd', e_end, k)
    def step(carry, inp):
        c_st, n_st, m_st = carry
        be, ml, cl, nl = inp
        m_new = jnp.maximum(be + m_st, ml)
        a_old, a_loc = jnp.exp(be + m_st - m_new), jnp.exp(ml - m_new)
        c_new = a_old[..., None, None] * c_st + a_loc[..., None, None] * cl
        n_new = a_old[..., None] * n_st + a_loc[..., None] * nl
        return (c_new, n_new, m_new), carry
    mv = lambda a: jnp.moveaxis(a, 1, 0)
    final, starts = lax.scan(step, state, (mv(b_end), mv(m_loc), mv(c_loc), mv(n_loc)))
    if not with_out:
        return None, final
    c_s, n_s, m_s = (jnp.moveaxis(a, 0, 1) for a in starts)
    tri = jnp.tril(jnp.ones((ML_CHUNK, ML_CHUNK), bool))[None, None, :, :, None]
    d_log = jnp.where(tri, b[:, :, :, None] - b[:, :, None] + log_i[:, :, None], -jnp.inf)
    w_inter = b + m_s[:, :, None]
    m_t = jnp.maximum(w_inter, jnp.max(d_log, axis=3))
    e_inter = jnp.exp(w_inter - m_t)
    wts = jnp.exp(d_log - m_t[:, :, :, None]) * jnp.einsum('bcthd,bcshd->bctsh', q, k)
    num = (jnp.einsum('bctsh,bcshv->bcthv', wts, v)
           + e_inter[..., None] * jnp.einsum('bcthd,bchvd->bcthv', q, c_s))
    den = jnp.sum(wts, axis=3) + e_inter * jnp.einsum('bcthd,bchd->bcth', q, n_s)
    h = num / jnp.maximum(jnp.abs(den), jnp.exp(-m_t))[..., None]
    return h.reshape(bsz, t_len, nh, -1), final


def mlstm_mixer(zc, zl, norm_w, with_ctx_out):
    def split(z):
        shp = z.shape[:2]
        q, k, v, o = (z[..., j * GROUP_W:(j + 1) * GROUP_W] for j in range(4))
        gates = z[..., 4 * GROUP_W:].astype(F32).reshape(shp + (4, ML_HEADS))
        hd = lambda a: a.reshape(shp + (ML_HEADS, -1))
        return hd(q), hd(k) * ML_DK ** -0.5, hd(v), o, gates
    qc, kc, vc, oc, gtc = split(zc)
    ql, kl, vl, ol, gtl = split(zl)
    bsz = zc.shape[0]
    hc, hl = 0.0, 0.0
    for d in range(2):
        state0 = (jnp.zeros((bsz, ML_HEADS, ML_DV, ML_DK), F32), jnp.zeros((bsz, ML_HEADS, ML_DK), F32),
                  jnp.zeros((bsz, ML_HEADS), F32))
        h_c, st_c = mlstm_chunked(flip_seq(qc, d), flip_seq(kc, d), flip_seq(vc, d), flip_seq(gtc[:, :, d], d),
                                  flip_seq(jax.nn.log_sigmoid(gtc[:, :, 2 + d]), d), state0, with_ctx_out)
        h_l, _ = mlstm_chunked(flip_seq(ql, d), flip_seq(kl, d), flip_seq(vl, d), flip_seq(gtl[:, :, d], d),
                               flip_seq(jax.nn.log_sigmoid(gtl[:, :, 2 + d]), d), st_c)
        hl = hl + flip_seq(h_l, d)
        if with_ctx_out:
            hc = hc + flip_seq(h_c, d)
    def readout(h, o):
        h = h.reshape(h.shape[:2] + (-1,)).astype(o.dtype)
        return head_rms_norm(h, norm_w, ML_HEADS) * jax.nn.sigmoid(o)
    return (readout(hc, oc) if with_ctx_out else None), readout(hl, ol)


def _split3(x):
    hi = x.astype(BF16)
    r = x - hi.astype(F32)
    mid = r.astype(BF16)
    return hi, mid, (r - mid.astype(F32)).astype(BF16)


def _dot3(x, sel):
    return sum(jnp.dot(p, sel, preferred_element_type=F32) for p in _split3(x))


def _mdot3(sel, x):
    return sum(jnp.dot(sel, p, preferred_element_type=F32) for p in _split3(x))


def _dot2(x, sel):
    hi = x.astype(BF16)
    lo = (x - hi.astype(F32)).astype(BF16)
    return jnp.dot(hi, sel, preferred_element_type=F32) + jnp.dot(lo, sel, preferred_element_type=F32)


def _head_blocks(n, seg):
    ri = lax.broadcasted_iota(jnp.int32, (n, n), 0)
    ci = lax.broadcasted_iota(jnp.int32, (n, n), 1)
    return ri // seg == ci // seg, ri, ci


def _chunk_cumsum_cols(x, chunk, reverse):
    same, ri, ci = _head_blocks(ROW_TILE, chunk)
    tri = jnp.where(same & ((ci >= ri) if reverse else (ci <= ri)), 1.0, 0.0).astype(BF16)
    return [_mdot3(tri, x[r:r + ROW_TILE]) for r in range(0, x.shape[0], ROW_TILE)]


def _repeat_row(row, n):
    return pl.ds(row, n, stride=0)


def _bwd_chunk(i, n_ctx, n_all):
    return jnp.where(i < n_ctx, n_ctx - 1 - i, n_all + n_ctx - 1 - i)


def _head_rms(x, ebd, seg):
    return x * lax.rsqrt(_dot2(x * x, ebd) * (1.0 / seg) + EPS)


def _hgrn2_body(z_ref, lb_ref, nw_ref, out_ref, bsc, ksc, vsc, osc, st, *, t_len):
    n_half = GROUP_W // LANES
    halves = lambda ref, idx, rows: jnp.concatenate([ref[idx + (h, rows)] for h in range(n_half)], axis=1)
    for h in range(n_half):
        vsc[h] = z_ref[0, :, GROUP_W + h * LANES:GROUP_W + (h + 1) * LANES]
    for d in range(2):
        lb = lb_ref[d:d + 1, :]
        f = lb + (1.0 - lb) * jax.nn.sigmoid(z_ref[0, :, (2 + d) * GROUP_W:(3 + d) * GROUP_W])
        k_all = 1.0 - f
        for h in range(n_half):
            ksc[d, h] = k_all[:, h * LANES:(h + 1) * LANES]
        for j, blk in enumerate(_chunk_cumsum_cols(jnp.log(f), HG_CHUNK, d == 1)):
            for h in range(n_half):
                bsc[d, h, j * ROW_TILE:(j + 1) * ROW_TILE, :] = blk[:, h * LANES:(h + 1) * LANES] * LOG2_E
    st[...] = jnp.zeros(st.shape, F32)
    bd, _, _ = _head_blocks(GROUP_W, HG_DK)
    ebd = jnp.where(bd, 1.0, 0.0).astype(BF16)
    tt = lax.broadcasted_iota(jnp.int32, (HG_CHUNK, GROUP_W), 0)
    n_all = t_len // HG_CHUNK
    n_ctx = CTX_LEN // HG_CHUNK

    def chunk(d, c):
        r0 = pl.multiple_of(c * HG_CHUNK, HG_CHUNK)
        rows = pl.ds(r0, HG_CHUNK)
        q = z_ref[0, rows, 0:GROUP_W] * (HG_DK ** -0.5)
        v = halves(vsc, (), rows)
        k = halves(ksc, (d,), rows)
        b = halves(bsc, (d,), rows)
        b_end = b[0:1] if d else b[HG_CHUNK - 1:HG_CHUNK]
        slabs = []
        for s in range(HG_CHUNK):
            row_s = _repeat_row(r0 + s, HG_CHUNK)
            valid = (tt <= s) if d else (tt >= s)
            slabs.append(jnp.where(valid, jnp.exp2(b - halves(bsc, (d,), row_s)), 0.0) * (q * halves(ksc, (d,), row_s)))
        a = jnp.concatenate(slabs, axis=0).astype(BF16)
        sc = jnp.dot(a, ebd, preferred_element_type=F32)
        o_intra = jnp.zeros((HG_CHUNK, GROUP_W), F32)
        for s in range(HG_CHUNK):
            o_intra = o_intra + (sc[s * HG_CHUNK:(s + 1) * HG_CHUNK]
                                 * halves(vsc, (), _repeat_row(r0 + s, HG_CHUNK)))
        s_prev = st[d]
        o_inter = lax.dot_general((q * jnp.exp2(b)).astype(BF16), s_prev.astype(BF16), (((1,), (1,)), ((), ())),
                                  preferred_element_type=F32)
        osc[d, rows, :] = o_intra + o_inter
        kd = (k * jnp.exp2(b_end - b)).astype(BF16)
        upd = lax.dot_general(v.astype(BF16), kd, (((0,), (0,)), ((), ())), preferred_element_type=F32)
        st[d] = s_prev * jnp.exp2(b_end) + jnp.where(bd, upd, 0.0)

    def step(i, carry):
        chunk(0, i)
        chunk(1, _bwd_chunk(i, n_ctx, n_all))
        return carry

    lax.fori_loop(0, n_all, step, 0, unroll=2)
    o = osc[0] + osc[1]
    g = z_ref[0, :, 4 * GROUP_W:5 * GROUP_W]
    out_ref[0] = _head_rms(o, ebd, HG_DV) * nw_ref[...] * (g * jax.nn.sigmoid(g))


def _hgrn2_mixer(z, lb, norm_w):
    bsz, t, _ = z.shape
    return pl.pallas_call(
        functools.partial(_hgrn2_body, t_len=t),
        grid=(bsz,),
        in_specs=[pl.BlockSpec((1, t, 5 * GROUP_W), lambda b: (b, 0, 0)),
                  pl.BlockSpec((2, GROUP_W), lambda b: (0, 0)),
                  pl.BlockSpec((1, GROUP_W), lambda b: (0, 0))],
        out_specs=pl.BlockSpec((1, t, GROUP_W), lambda b: (b, 0, 0)),
        out_shape=jax.ShapeDtypeStruct((bsz, t, GROUP_W), F32),
        scratch_shapes=[pltpu.VMEM((2, GROUP_W // LANES, t, LANES), F32), pltpu.VMEM((2, GROUP_W // LANES, t, LANES), F32),
                        pltpu.VMEM((GROUP_W // LANES, t, LANES), F32),
                        pltpu.VMEM((2, t, GROUP_W), F32), pltpu.VMEM((2, GROUP_W, GROUP_W), F32)],
        compiler_params=_cparams("arbitrary"),
    )(z, lb.astype(F32), norm_w.astype(F32).reshape(1, GROUP_W))


def _lane_expand(cols, seg):
    lane = lax.broadcasted_iota(jnp.int32, (1, len(cols) * seg), 1)
    out = cols[-1]
    for h in range(len(cols) - 2, -1, -1):
        out = jnp.where(lane < (h + 1) * seg, cols[h], out)
    return out


def _mlstm_body(q_ref, k_ref, v_ref, o_ref, zg_ref, gr_ref, nw_ref, out_ref,
                bce, lice, brow, hsc, cst, nst, *, t_len):
    n_all = t_len // ML_CHUNK
    n_ctx = CTX_LEN // ML_CHUNK
    g = zg_ref[0]
    lf = jax.nn.log_sigmoid(g)
    src = lax.broadcasted_iota(jnp.int32, (LANES, GROUP_W), 0)
    dst_head = lax.broadcasted_iota(jnp.int32, (LANES, GROUP_W), 1) // ML_DK
    bd, _, _ = _head_blocks(GROUP_W, ML_DK)
    ebd = jnp.where(bd, 1.0, 0.0).astype(BF16)
    for d in range(2):
        lice[d] = _dot3(g, jnp.where(src == ML_HEADS * d + dst_head, 1.0, 0.0).astype(BF16))
        lf_exp = _dot3(lf, jnp.where(src == ML_HEADS * (2 + d) + dst_head, 1.0, 0.0).astype(BF16))
        for j, blk in enumerate(_chunk_cumsum_cols(lf_exp, ML_CHUNK, d == 1)):
            bce[d, j * ROW_TILE:(j + 1) * ROW_TILE, :] = blk
        _, ri, ci = _head_blocks(GROUP_W, ML_CHUNK)
        tri = jnp.where(bd & ((ri >= ci) if d else (ri <= ci)), 1.0, 0.0).astype(BF16)
        brow[d] = _dot3(jax.nn.log_sigmoid(gr_ref[0, 2 + d]), tri)
    cst[...] = jnp.zeros(cst.shape, F32)
    nst[...] = jnp.zeros(nst.shape, F32)
    tt = lax.broadcasted_iota(jnp.int32, (ML_CHUNK, GROUP_W), 0)
    ss = lax.broadcasted_iota(jnp.int32, (ML_CHUNK, GROUP_W), 1) % ML_CHUNK
    lane_h = lax.broadcasted_iota(jnp.int32, (1, LANES), 1) < ML_CHUNK

    def chunk(d, c, m_prev):
        rows = pl.ds(pl.multiple_of(c * ML_CHUNK, ML_CHUNK), ML_CHUNK)
        q = q_ref[0, rows, :]
        k = k_ref[0, rows, :] * (ML_DK ** -0.5)
        v = v_ref[0, rows, :]
        bc = bce[d, rows, :]
        lic = lice[d, rows, :]
        br = brow[d, pl.ds(c, 1), :]
        lir = gr_ref[0, d, pl.ds(c, 1), :]
        b_end = bc[0:1] if d else bc[ML_CHUNK - 1:ML_CHUNK]
        valid = (ss >= tt) if d else (ss <= tt)
        d_log = jnp.where(valid, bc - br + lir, -jnp.inf)
        cols = []
        for hp in range(GROUP_W // LANES):
            xs = d_log[:, hp * LANES:(hp + 1) * LANES]
            cols.append(jnp.max(jnp.where(lane_h, xs, -jnp.inf), axis=-1, keepdims=True))
            cols.append(jnp.max(jnp.where(lane_h, -jnp.inf, xs), axis=-1, keepdims=True))
        w_inter = bc + m_prev
        m_t = jnp.maximum(w_inter, _lane_expand(cols, ML_CHUNK))
        e_inter = jnp.exp(w_inter - m_t)
        qb = q.astype(BF16)
        kbd = jnp.where(bd, jnp.concatenate([k] * ML_HEADS, axis=0).T, 0.0).astype(BF16)
        vbd = jnp.where(bd, jnp.concatenate([v] * ML_HEADS, axis=0), 0.0).astype(BF16)
        wts = jnp.exp(d_log - m_t) * jnp.dot(qb, kbd, preferred_element_type=F32)
        w_hi = wts.astype(BF16)
        w_lo = (wts - w_hi.astype(F32)).astype(BF16)
        c_prev, n_prev = cst[d], nst[d]
        num = (jnp.dot(w_hi, vbd, preferred_element_type=F32)
               + e_inter * jnp.dot(qb, c_prev.astype(BF16), preferred_element_type=F32))
        den = (jnp.dot(w_hi, ebd, preferred_element_type=F32) + jnp.dot(w_lo, ebd, preferred_element_type=F32)
               + e_inter * jnp.dot(qb, n_prev.astype(BF16), preferred_element_type=F32))
        hsc[d, rows, :] = num / jnp.maximum(jnp.abs(den), jnp.exp(-m_t))
        w_end = b_end - bc + lic
        m_loc = jnp.max(w_end, axis=0, keepdims=True)
        e_end = jnp.exp(w_end - m_loc)
        m_new = jnp.maximum(b_end + m_prev, m_loc)
        a_old = jnp.exp(b_end + m_prev - m_new)
        a_loc = jnp.exp(m_loc - m_new)
        evbd = jnp.where(bd, jnp.concatenate([e_end * v] * ML_HEADS, axis=0), 0.0).astype(BF16)
        eebd = jnp.where(bd, jnp.concatenate([e_end] * ML_HEADS, axis=0), 0.0).astype(BF16)
        cst[d] = a_old * c_prev + a_loc * jnp.dot(kbd, evbd, preferred_element_type=F32)
        nst[d] = a_old * n_prev + a_loc * jnp.dot(kbd, eebd, preferred_element_type=F32)
        return m_new

    def step(i, carry):
        return chunk(0, i, carry[0]), chunk(1, _bwd_chunk(i, n_ctx, n_all), carry[1])

    zero = jnp.zeros((1, GROUP_W), F32)
    lax.fori_loop(0, n_all, step, (zero, zero), unroll=2)
    out_ref[0] = _head_rms(hsc[0] + hsc[1], ebd, ML_DV) * nw_ref[...] * jax.nn.sigmoid(o_ref[0])


def _mlstm_mixer(z, gates_t, norm_w):
    bsz, t, _ = z.shape
    nc = t // ML_CHUNK
    ncp = -(-nc // 8) * 8
    gr = gates_t.reshape(bsz, 4, ML_HEADS, nc, ML_CHUNK).transpose(0, 1, 3, 2, 4).reshape(bsz, 4, nc, GROUP_W)
    gr = jnp.pad(gr, ((0, 0), (0, 0), (0, ncp - nc), (0, 0)))
    col = ML_OFF // GROUP_W
    zspec = lambda j: pl.BlockSpec((1, t, GROUP_W), lambda b: (b, 0, col + j))
    return pl.pallas_call(
        functools.partial(_mlstm_body, t_len=t),
        grid=(bsz,),
        in_specs=[zspec(0), zspec(1), zspec(2), zspec(3),
                  pl.BlockSpec((1, t, LANES), lambda b: (b, 0, ML_GATE_OFF // LANES)),
                  pl.BlockSpec((1, 4, ncp, GROUP_W), lambda b: (b, 0, 0, 0)),
                  pl.BlockSpec((1, GROUP_W), lambda b: (0, 0))],
        out_specs=pl.BlockSpec((1, t, GROUP_W), lambda b: (b, 0, 0)),
        out_shape=jax.ShapeDtypeStruct((bsz, t, GROUP_W), F32),
        scratch_shapes=[pltpu.VMEM((2, t, GROUP_W), F32), pltpu.VMEM((2, t, GROUP_W), F32),
                        pltpu.VMEM((2, ncp, GROUP_W), F32), pltpu.VMEM((2, t, GROUP_W), F32),
                        pltpu.VMEM((2, GROUP_W, GROUP_W), F32), pltpu.VMEM((2, GROUP_W, GROUP_W), F32)],
        compiler_params=_cparams("arbitrary"),
    )(z, z, z, z, z, gr, norm_w.astype(F32).reshape(1, GROUP_W))


def _final_norm_body(x_ref, w_ref, o_ref):
    o_ref[0] = _rms(x_ref[0]) * w_ref[...]


def _final_norm(h, w):
    bsz, t, d = h.shape
    n = t - CTX_LEN
    blk0 = CTX_LEN // ROW_TILE
    return pl.pallas_call(
        _final_norm_body,
        grid=(bsz, n // ROW_TILE),
        in_specs=[pl.BlockSpec((1, ROW_TILE, d), lambda b, j: (b, blk0 + j, 0)),
                  pl.BlockSpec((1, d), lambda b, j: (0, 0))],
        out_specs=pl.BlockSpec((1, ROW_TILE, d), lambda b, j: (b, j, 0)),
        out_shape=jax.ShapeDtypeStruct((bsz, n, d), F32),
        compiler_params=_cparams("arbitrary", "arbitrary"),
    )(h, w.reshape(1, d))


def kernel(x, c, ctx, c_ctx, mod_w, mod_b, norm1_w, norm2_w, w_in, b_in, hg_lb_logits, hg_norm_w,
           s5_lam_re, s5_lam_im, s5_log_dt, s5_b_re, s5_b_im, s5_c_re, s5_c_im, s5_d, s5_glu_w, s5_glu_b,
           da_lq1, da_lk1, da_lq2, da_lk2, da_norm_w, ml_norm_w, w_out, router_w,
           exp_w1, exp_w3, exp_w2, final_norm_w):
    bsz, seq, d = x.shape
    assert ctx.shape[1] == CTX_LEN == ROW_TILE and seq % ROW_TILE == 0 and d == D_MODEL
    t = CTX_LEN + seq
    lb_all = jnp.cumsum(jax.nn.softmax(hg_lb_logits.astype(F32), axis=0), axis=0)
    lb_all = lb_all - lb_all[0]

    n_rows = -(-(bsz + 1) // 8) * 8
    cv = jnp.zeros((n_rows, d), F32).at[:bsz].set(c).at[bsz].set(c_ctx)
    mod_all = _modulation(cv, mod_w, mod_b)
    cos_t, sin_t = _rope_tables(t)
    h = jnp.concatenate([ctx, x], axis=1)

    for li in range(DEPTH):
        ctx_out = li < DEPTH - 1
        m6 = mod_all[li].reshape(n_rows, 6, d)
        modsel = jnp.stack([jnp.broadcast_to(m6[bsz], (bsz, 6, d)), m6[:bsz]], axis=1)
        w_pad = jnp.pad(w_in[li], ((0, 0), (0, IN_COLS_PAD - IN_COLS))).astype(BF16)
        b_pad = jnp.pad(b_in[li], (0, IN_COLS_PAD - IN_COLS)).reshape(1, IN_COLS_PAD)
        wg_t = w_in[li][:, ML_GATE_OFF:].T.astype(BF16)
        bg_col = b_in[li][ML_GATE_OFF:].reshape(N_GATES, 1)
        z, gates_t = _in_projection(h, modsel, norm1_w[li], w_pad, b_pad, wg_t, bg_col)

        mix_a = _hgrn2_mixer(z, lb_all[li], hg_norm_w[li])
        mix_b = _s5_mixer(z, s5_lam_re[li], s5_lam_im[li], s5_log_dt[li], s5_b_re[li], s5_b_im[li],
                          s5_c_re[li], s5_c_im[li], s5_d[li], s5_glu_w[li], s5_glu_b[li])
        lam_init = 0.8 - 0.6 * math.exp(-0.3 * li)
        att_args = (da_lq1[li], da_lk1[li], da_lq2[li], da_lk2[li], da_norm_w[li], lam_init)
        c_l = _diff_attention(z, CTX_LEN, seq, t, True, cos_t, sin_t, *att_args)
        if ctx_out:
            c_c = _diff_attention(z, 0, CTX_LEN, CTX_LEN, False, cos_t, sin_t, *att_args)
        else:
            c_c = jnp.zeros((bsz, CTX_LEN, GROUP_W), F32)
        mix_c = jnp.concatenate([c_c, c_l], axis=1)
        mix_d = _mlstm_mixer(z, gates_t, ml_norm_w[li])

        hm, xm, aff_t = _out_projection((mix_a, mix_b, mix_c, mix_d), h, modsel, norm2_w[li],
                                        w_out[li].astype(BF16), router_w[li].T.astype(BF16))
        w1, w3, w2 = exp_w1[li].astype(BF16), exp_w3[li].astype(BF16), exp_w2[li].astype(BF16)
        slot, slot_t = _route(aff_t, CTX_LEN, seq)
        y = _moe_ffn(xm, slot, aff_t, CTX_LEN, seq, 1, w1, w3, w2)
        if not ctx_out:
            return _moe_combine(y, slot_t, hm, modsel, CTX_LEN, seq, 1, final_w=final_norm_w)
        h = _moe_combine(y, slot_t, hm, modsel, CTX_LEN, seq, 1)
        slot, slot_t = _route(aff_t, 0, CTX_LEN)
        y = _moe_ffn(xm, slot, aff_t, 0, CTX_LEN, bsz, w1, w3, w2)
        h = _moe_combine(y, slot_t, h, modsel, 0, CTX_LEN, 0)
```

```python
import functools
import math
import jax
import jax.numpy as jnp
from jax import lax
from jax.experimental import pallas as pl
from jax.experimental.pallas import tpu as pltpu

D_MODEL = 1024
DEPTH = 2
CTX_LEN = 256
GRID_W = 64
N_MIXERS = 4
GROUP_W = D_MODEL // N_MIXERS
MIX_W = N_MIXERS * GROUP_W
EPS = 1e-6
F32 = jnp.float32
BF16 = jnp.bfloat16
HG_HEADS = 4
HG_DK = GROUP_W // HG_HEADS
HG_DV = GROUP_W // HG_HEADS
HG_CHUNK = 16
S5_CH = 16
S5_GROUPS = GROUP_W // S5_CH
S5_STATE = 64
DA_HEADS = 4
DA_DQK = GROUP_W // (2 * DA_HEADS)
DA_DV = GROUP_W // DA_HEADS
ROPE_BASE = 10000.0
ML_HEADS = 4
ML_DK = GROUP_W // ML_HEADS
ML_DV = GROUP_W // ML_HEADS
ML_CHUNK = 64
N_EXPERTS = 16
EC_CAPACITY = 2
D_EXPERT = 2 * D_MODEL
HG_OFF = 0
S5_OFF = HG_OFF + 5 * GROUP_W
DA_OFF = S5_OFF + GROUP_W
ML_OFF = DA_OFF + 3 * GROUP_W
ML_GATE_OFF = ML_OFF + 4 * GROUP_W
IN_COLS = ML_GATE_OFF + 4 * ML_HEADS

LANES = 128
ROW_TILE = 256
IN_COLS_PAD = ML_GATE_OFF + LANES
N_GATES = 4 * ML_HEADS
S5_LANES = S5_GROUPS * S5_STATE
S5_HALF = S5_LANES // 2
S5_SEGS = 8
VMEM_LIMIT = 56 * 1024 * 1024
LOG2_E = math.log2(math.e)
GATHER_SLOTS = 64
START_STRIDE = 16


def _cparams(*sem):
    return pltpu.CompilerParams(dimension_semantics=sem, vmem_limit_bytes=VMEM_LIMIT)


def _rms(x):
    return x * lax.rsqrt(jnp.mean(x * x, axis=-1, keepdims=True) + EPS)


def _mod_body(cv_ref, w_ref, b_ref, o_ref):
    cv = cv_ref[...]
    s = cv * jax.nn.sigmoid(cv)
    o_ref[0] = jnp.dot(s.astype(BF16), w_ref[0].astype(BF16), preferred_element_type=F32) + b_ref[0]


def _modulation(cv, mod_w, mod_b):
    n_l, d, n6 = mod_w.shape
    r = cv.shape[0]
    tn = n6 // 4
    return pl.pallas_call(
        _mod_body,
        grid=(n_l, 4),
        in_specs=[pl.BlockSpec((r, d), lambda l, j: (0, 0)),
                  pl.BlockSpec((1, d, tn), lambda l, j: (l, 0, j)),
                  pl.BlockSpec((1, 1, tn), lambda l, j: (l, 0, j))],
        out_specs=pl.BlockSpec((1, r, tn), lambda l, j: (l, 0, j)),
        out_shape=jax.ShapeDtypeStruct((n_l, r, n6), F32),
        compiler_params=_cparams("arbitrary", "arbitrary"),
    )(cv, mod_w, mod_b.reshape(n_l, 1, n6))


def _inproj_body(h_ref, m_ref, nw_ref, w_ref, b_ref, wg_ref, bg_ref, z_ref, gt_ref):
    m = m_ref[0, 0]
    xn = _rms(h_ref[0]) * nw_ref[...] * (1.0 + m[1:2]) + m[0:1]
    xb = xn.astype(BF16)
    z_ref[0] = jnp.dot(xb, w_ref[...], preferred_element_type=F32) + b_ref[...]
    gt_ref[0] = lax.dot_general(wg_ref[...], xb, (((1,), (1,)), ((), ())),
                                preferred_element_type=F32) + bg_ref[...]


def _in_projection(h, modsel, norm_w, w_pad, b_pad, wg_t, bg_col):
    bsz, t, d = h.shape
    nt = t // ROW_TILE
    return pl.pallas_call(
        _inproj_body,
        grid=(bsz, nt),
        in_specs=[pl.BlockSpec((1, ROW_TILE, d), lambda b, j: (b, j, 0)),
                  pl.BlockSpec((1, 1, 6, d), lambda b, j: (b, jnp.minimum(j, 1), 0, 0)),
                  pl.BlockSpec((1, d), lambda b, j: (0, 0)),
                  pl.BlockSpec((d, IN_COLS_PAD), lambda b, j: (0, 0)),
                  pl.BlockSpec((1, IN_COLS_PAD), lambda b, j: (0, 0)),
                  pl.BlockSpec((N_GATES, d), lambda b, j: (0, 0)),
                  pl.BlockSpec((N_GATES, 1), lambda b, j: (0, 0))],
        out_specs=[pl.BlockSpec((1, ROW_TILE, IN_COLS_PAD), lambda b, j: (b, j, 0)),
                   pl.BlockSpec((1, N_GATES, ROW_TILE), lambda b, j: (b, 0, j))],
        out_shape=[jax.ShapeDtypeStruct((bsz, t, IN_COLS_PAD), F32),
                   jax.ShapeDtypeStruct((bsz, N_GATES, t), F32)],
        compiler_params=_cparams("arbitrary", "arbitrary"),
    )(h, modsel, norm_w.reshape(1, d), w_pad, b_pad, wg_t, bg_col)


def _outproj_body(a_ref, b_ref, c_ref, d_ref, h_ref, m_ref, nw_ref, wo_ref, rw_ref, hm_ref, xm_ref, aff_ref):
    m = m_ref[0, 0]
    y = jnp.concatenate([a_ref[0], b_ref[0], c_ref[0], d_ref[0]], axis=-1).astype(BF16)
    hm = h_ref[0] + m[2:3] * jnp.dot(y, wo_ref[...], preferred_element_type=F32)
    hm_ref[0] = hm
    xm = (_rms(hm) * nw_ref[...] * (1.0 + m[4:5]) + m[3:4]).astype(BF16)
    xm_ref[0] = xm
    logit = lax.dot_general(rw_ref[...], xm, (((1,), (1,)), ((), ())), preferred_element_type=F32)
    e = jnp.exp(logit - jnp.max(logit, axis=0, keepdims=True))
    aff_ref[0] = e / jnp.sum(e, axis=0, keepdims=True)


def _out_projection(mix, h, modsel, norm_w, wo_bf, rw_t):
    bsz, t, d = h.shape
    nt = t // ROW_TILE
    mix_spec = pl.BlockSpec((1, ROW_TILE, GROUP_W), lambda b, j: (b, j, 0))
    row_spec = pl.BlockSpec((1, ROW_TILE, d), lambda b, j: (b, j, 0))
    return pl.pallas_call(
        _outproj_body,
        grid=(bsz, nt),
        in_specs=[mix_spec, mix_spec, mix_spec, mix_spec, row_spec,
                  pl.BlockSpec((1, 1, 6, d), lambda b, j: (b, jnp.minimum(j, 1), 0, 0)),
                  pl.BlockSpec((1, d), lambda b, j: (0, 0)),
                  pl.BlockSpec((MIX_W, d), lambda b, j: (0, 0)),
                  pl.BlockSpec((N_EXPERTS, d), lambda b, j: (0, 0))],
        out_specs=[row_spec, row_spec, pl.BlockSpec((1, N_EXPERTS, ROW_TILE), lambda b, j: (b, 0, j))],
        out_shape=[jax.ShapeDtypeStruct((bsz, t, d), F32), jax.ShapeDtypeStruct((bsz, t, d), BF16),
                   jax.ShapeDtypeStruct((bsz, N_EXPERTS, t), F32)],
        compiler_params=_cparams("arbitrary", "arbitrary"),
    )(*mix, h, modsel, norm_w.reshape(1, d), wo_bf, rw_t)


def _topk_body(aff_ref, slot_ref, slot_t_ref, start_ref, *, cap, n, row0):
    aff = aff_ref[0, :, row0:row0 + n]
    bits = pltpu.bitcast(aff, jnp.int32)
    thr = jnp.zeros((N_EXPERTS, 1), jnp.int32)
    for bit in range(30, -1, -1):
        cand = thr | (1 << bit)
        cnt = jnp.sum(jnp.where(bits >= cand, 1.0, 0.0), axis=-1, keepdims=True)
        thr = jnp.where(cnt >= cap, cand, thr)
    room = cap - jnp.sum(jnp.where(bits > thr, 1.0, 0.0), axis=-1, keepdims=True)
    ri = lax.broadcasted_iota(jnp.int32, (LANES, LANES), 0)
    ci = lax.broadcasted_iota(jnp.int32, (LANES, LANES), 1)
    incl = jnp.where(ri <= ci, 1.0, 0.0).astype(BF16)
    off_eq = jnp.zeros((N_EXPERTS, 1), F32)
    off_sel = jnp.zeros((N_EXPERTS, 1), F32)
    pieces = []
    lane = lax.broadcasted_iota(jnp.int32, (1, LANES), 1)
    starts = jnp.full((N_EXPERTS, LANES), float(cap), F32)
    per_blk = ROW_TILE // LANES
    for j in range(n // LANES):
        if j % per_blk == 0:
            starts = jnp.where(lane == j // per_blk, off_sel, starts)
        sl = slice(j * LANES, (j + 1) * LANES)
        bits_b = bits[:, sl]
        eq_b = jnp.where(bits_b == thr, 1.0, 0.0)
        pos_eq = jnp.dot(eq_b.astype(BF16), incl, preferred_element_type=F32) - eq_b + off_eq
        sel = jnp.where(bits_b > thr, 1.0, jnp.where(pos_eq < room, eq_b, 0.0))
        pos_sel = jnp.dot(sel.astype(BF16), incl, preferred_element_type=F32) - sel + off_sel
        piece = jnp.where(sel > 0.5, pos_sel, -1.0)
        slot_ref[0, :, sl] = piece.astype(jnp.int32)
        pieces.append(piece)
        off_eq = off_eq + jnp.sum(eq_b, axis=-1, keepdims=True)
        off_sel = off_sel + jnp.sum(sel, axis=-1, keepdims=True)
    pad = jnp.full((LANES - N_EXPERTS, LANES), -1.0, F32)
    for j in range(n // LANES):
        tile = jnp.concatenate([pieces[j], pad], axis=0)
        slot_t_ref[0, j * LANES:(j + 1) * LANES, :] = tile.T.astype(jnp.int32)
    start_ref[0] = starts.astype(jnp.int32)


def _route(aff_t, row0, n):
    bsz, _, t = aff_t.shape
    cap = EC_CAPACITY * n // N_EXPERTS
    return pl.pallas_call(
        functools.partial(_topk_body, cap=cap, n=n, row0=row0),
        grid=(bsz,),
        in_specs=[pl.BlockSpec((1, N_EXPERTS, t), lambda b: (b, 0, 0))],
        out_specs=[pl.BlockSpec((1, N_EXPERTS, n), lambda b: (b, 0, 0)),
                   pl.BlockSpec((1, n, LANES), lambda b: (b, 0, 0)),
                   pl.BlockSpec((1, N_EXPERTS, LANES), lambda b: (b, 0, 0))],
        out_shape=[jax.ShapeDtypeStruct((bsz, N_EXPERTS, n), jnp.int32),
                   jax.ShapeDtypeStruct((bsz, n, LANES), jnp.int32),
                   jax.ShapeDtypeStruct((bsz, N_EXPERTS, LANES), jnp.int32)],
        compiler_params=_cparams("arbitrary"),
    )(aff_t)


def _moe_ffn_body(st_ref, x_ref, slot_ref, aff_ref, w1_ref, w3_ref, w2_ref, y_ref, xe_scr, g_scr, *, nb, cap, n, off):
    e, bb = pl.program_id(0), pl.program_id(1)
    sb = min(cap, GATHER_SLOTS)
    xe_scr[...] = jnp.zeros(xe_scr.shape, F32)
    g_scr[...] = jnp.zeros(g_scr.shape, F32)
    for i in range(nb):
        base = ((bb * nb + i) * N_EXPERTS + e) * START_STRIDE
        for kb in range(n // ROW_TILE):
            lo, hi = st_ref[base + kb], st_ref[base + kb + 1]
            tok = slice(off + kb * ROW_TILE, off + (kb + 1) * ROW_TILE)
            for mb in range(cap // sb):
                @pl.when((lo < (mb + 1) * sb) & (hi > mb * sb))
                def _(i=i, kb=kb, mb=mb, tok=tok):
                    hit = (lax.broadcasted_iota(jnp.int32, (sb, ROW_TILE), 0) + mb * sb
                           == slot_ref[i, 0, :, kb * ROW_TILE:(kb + 1) * ROW_TILE])
                    rows = slice(i * cap + mb * sb, i * cap + (mb + 1) * sb)
                    xe_scr[rows, :] += jnp.dot(jnp.where(hit, 1.0, 0.0).astype(BF16), x_ref[i, tok, :],
                                               preferred_element_type=F32)
                    g_scr[rows, :] += jnp.sum(jnp.where(hit, aff_ref[i, 0, :, tok], 0.0), axis=-1, keepdims=True)
    xe = xe_scr[...].astype(BF16)
    h1 = jnp.dot(xe, w1_ref[0, 0], preferred_element_type=F32)
    h3 = jnp.dot(xe, w3_ref[0, 0], preferred_element_type=F32)
    act = (h1 * jax.nn.sigmoid(h1) * h3).astype(BF16)
    y = (jnp.dot(act, w2_ref[0, 0], preferred_element_type=F32) * g_scr[...]).astype(BF16)
    for i in range(nb):
        y_ref[i, 0] = y[i * cap:(i + 1) * cap]


def _moe_ffn(xm, slot, starts, aff_t, row0, n, nb, w1, w3, w2, li):
    bsz, t, d = xm.shape
    cap = EC_CAPACITY * n // N_EXPERTS
    assert n // ROW_TILE < START_STRIDE
    rows, off = (n, 0) if row0 % n == 0 else (t, row0)
    blk = row0 // n if off == 0 else 0
    slot4 = slot.reshape(bsz, N_EXPERTS, 1, n)
    aff4 = aff_t.reshape(bsz, N_EXPERTS, 1, t)
    return pl.pallas_call(
        functools.partial(_moe_ffn_body, nb=nb, cap=cap, n=n, off=off),
        grid_spec=pltpu.PrefetchScalarGridSpec(
            num_scalar_prefetch=1,
            grid=(N_EXPERTS, bsz // nb),
            in_specs=[pl.BlockSpec((nb, rows, d), lambda e, b, st: (b, blk, 0)),
                      pl.BlockSpec((nb, 1, 1, n), lambda e, b, st: (b, e, 0, 0)),
                      pl.BlockSpec((nb, 1, 1, rows), lambda e, b, st: (b, e, 0, blk)),
                      pl.BlockSpec((1, 1, d, D_EXPERT), lambda e, b, st: (li, e, 0, 0)),
                      pl.BlockSpec((1, 1, d, D_EXPERT), lambda e, b, st: (li, e, 0, 0)),
                      pl.BlockSpec((1, 1, D_EXPERT, d), lambda e, b, st: (li, e, 0, 0))],
            out_specs=pl.BlockSpec((nb, 1, cap, d), lambda e, b, st: (b, e, 0, 0)),
            scratch_shapes=[pltpu.VMEM((nb * cap, d), F32), pltpu.VMEM((nb * cap, 1), F32)]),
        out_shape=jax.ShapeDtypeStruct((bsz, N_EXPERTS, cap, d), BF16),
        compiler_params=_cparams("arbitrary", "arbitrary"),
    )(starts[:, :, :START_STRIDE].reshape(-1), xm, slot4, aff4, w1, w3, w2)


def _moe_combine_body(y_ref, st_ref, hm_ref, m_ref, fw_ref, o_ref, *, cap, final):
    st = st_ref[0]
    lane = lax.broadcasted_iota(jnp.int32, (st.shape[0], cap), 1)
    hots = [jnp.where(lane == st[:, e:e + 1], 1.0, 0.0).astype(BF16) for e in range(N_EXPERTS)]
    if cap % LANES == 0:
        acc = jnp.dot(jnp.concatenate(hots, axis=1), y_ref[0].reshape(N_EXPERTS * cap, y_ref.shape[-1]),
                      preferred_element_type=F32)
    else:
        acc = sum(jnp.dot(hots[e], y_ref[0, e], preferred_element_type=F32) for e in range(N_EXPERTS))
    h = hm_ref[0] + m_ref[0, 0][5:6] * acc
    o_ref[0] = _rms(h) * fw_ref[...] if final else h


def _moe_combine(y, slot_t, hm, modsel, row0, n, is_latent, final_w=None):
    bsz, _, cap, d = y.shape
    nt = n // ROW_TILE
    blk0 = row0 // ROW_TILE
    final = final_w is not None
    fw = (final_w if final else jnp.ones((d,), F32)).reshape(1, d)
    return pl.pallas_call(
        functools.partial(_moe_combine_body, cap=cap, final=final),
        grid=(bsz, nt),
        in_specs=[pl.BlockSpec((1, N_EXPERTS, cap, d), lambda b, j: (b, 0, 0, 0)),
                  pl.BlockSpec((1, ROW_TILE, LANES), lambda b, j: (b, j, 0)),
                  pl.BlockSpec((1, ROW_TILE, d), lambda b, j: (b, blk0 + j, 0)),
                  pl.BlockSpec((1, 1, 6, d), lambda b, j: (b, is_latent, 0, 0)),
                  pl.BlockSpec((1, d), lambda b, j: (0, 0))],
        out_specs=pl.BlockSpec((1, ROW_TILE, d), lambda b, j: (b, j if final else blk0 + j, 0)),
        out_shape=jax.ShapeDtypeStruct((bsz, n, d) if final else hm.shape, F32),
        input_output_aliases={} if final else {2: 0},
        compiler_params=_cparams("arbitrary", "arbitrary"),
    )(y, slot_t, hm, modsel, fw)


def _swap_pairs(x):
    w = x.shape[-1]
    lane = lax.broadcasted_iota(jnp.int32, x.shape, x.ndim - 1)
    return jnp.where(lane % 2 == 0, pltpu.roll(x, w - 1, x.ndim - 1), pltpu.roll(x, 1, x.ndim - 1))


def _attn_body(q_ref, k_ref, v_ref, cq_ref, sq_ref, ck_ref, sk_ref, lq1_ref, lk1_ref, lq2_ref, lk2_ref, nw_ref,
               o_ref, k_scr, vt_scr, *, rope, lam_init):
    @pl.when(pl.program_id(1) == 0)
    def _():
        k = k_ref[0]
        if rope:
            k = k * ck_ref[...] + _swap_pairs(k) * sk_ref[...]
        k_scr[...] = k.astype(BF16)
        for r in range(0, k_scr.shape[0], ROW_TILE):
            vt_scr[:, r:r + ROW_TILE] = v_ref[0, r:r + ROW_TILE, :].T.astype(BF16)

    q = q_ref[0]
    if rope:
        q = q * cq_ref[...] + _swap_pairs(q) * sq_ref[...]
    q_t = (q * (DA_DQK ** -0.5 * LOG2_E)).T
    lam = (jnp.exp(jnp.sum(lq1_ref[...] * lk1_ref[...], axis=-1, keepdims=True))
           - jnp.exp(jnp.sum(lq2_ref[...] * lk2_ref[...], axis=-1, keepdims=True)) + lam_init)
    row = lax.broadcasted_iota(jnp.int32, (GROUP_W, 1), 0)
    kk = k_scr[...]
    outs = []
    for h in range(DA_HEADS):
        qm = jnp.concatenate([jnp.where(row // DA_DQK == 2 * h + mi, q_t, 0.0) for mi in range(2)], axis=1)
        s = jnp.dot(kk, qm.astype(BF16), preferred_element_type=F32)
        e = jnp.exp2(s - jnp.max(s, axis=0, keepdims=True))
        pv = (jnp.dot(vt_scr[h * DA_DV:(h + 1) * DA_DV, :], e.astype(BF16), preferred_element_type=F32)
              / jnp.sum(e, axis=0, keepdims=True))
        nq = q_t.shape[1]
        oh = pv[:, :nq] - lam * pv[:, nq:]
        outs.append(oh * lax.rsqrt(jnp.mean(oh * oh, axis=0, keepdims=True) + EPS))
    o_ref[0] = jnp.concatenate(outs, axis=0).T * nw_ref[...] * (1.0 - lam_init)


def _diff_attention(z, q_row0, nq, nk, rope, cos_t, sin_t, lq1, lk1, lq2, lk2, norm_w, lam_init):
    bsz = z.shape[0]
    qb = ROW_TILE
    q0 = q_row0 // qb
    col = DA_OFF // GROUP_W
    vec = lambda a: a.reshape(1, -1)
    small = pl.BlockSpec((1, DA_DQK), lambda b, j: (0, 0))
    return pl.pallas_call(
        functools.partial(_attn_body, rope=rope, lam_init=lam_init),
        grid=(bsz, nq // qb),
        in_specs=[pl.BlockSpec((1, qb, GROUP_W), lambda b, j: (b, q0 + j, col)),
                  pl.BlockSpec((1, nk, GROUP_W), lambda b, j: (b, 0, col + 1)),
                  pl.BlockSpec((1, nk, GROUP_W), lambda b, j: (b, 0, col + 2)),
                  pl.BlockSpec((qb, GROUP_W), lambda b, j: (q0 + j, 0)),
                  pl.BlockSpec((qb, GROUP_W), lambda b, j: (q0 + j, 0)),
                  pl.BlockSpec((nk, GROUP_W), lambda b, j: (0, 0)),
                  pl.BlockSpec((nk, GROUP_W), lambda b, j: (0, 0)),
                  small, small, small, small,
                  pl.BlockSpec((1, GROUP_W), lambda b, j: (0, 0))],
        out_specs=pl.BlockSpec((1, qb, GROUP_W), lambda b, j: (b, j, 0)),
        out_shape=jax.ShapeDtypeStruct((bsz, nq, GROUP_W), F32),
        scratch_shapes=[pltpu.VMEM((nk, GROUP_W), BF16), pltpu.VMEM((GROUP_W, nk), BF16)],
        compiler_params=_cparams("arbitrary", "arbitrary"),
    )(z, z, z, cos_t, sin_t, cos_t, sin_t, vec(lq1), vec(lk1), vec(lq2), vec(lk2), vec(norm_w))


def _rope_tables(t):
    n = t - CTX_LEN
    axis_dim = DA_DQK // 2
    inv = ROPE_BASE ** (-jnp.arange(0, axis_dim, 2, dtype=F32) / axis_dim)
    tok = jnp.arange(n, dtype=jnp.int32)
    row = (tok // GRID_W).astype(F32)
    colp = (tok % GRID_W).astype(F32)
    ang = jnp.concatenate([row[:, None] * inv, colp[:, None] * inv], axis=-1)
    cos = jnp.repeat(jnp.cos(ang), 2, axis=-1)
    sin = jnp.repeat(jnp.sin(ang), 2, axis=-1) * jnp.tile(jnp.array([-1.0, 1.0], F32), DA_DQK // 2)
    reps = GROUP_W // DA_DQK
    cos = jnp.concatenate([jnp.ones((CTX_LEN, GROUP_W), F32), jnp.tile(cos, (1, reps))], axis=0)
    sin = jnp.concatenate([jnp.zeros((CTX_LEN, GROUP_W), F32), jnp.tile(sin, (1, reps))], axis=0)
    return cos, sin


def _cmul(ar, ai, br, bi):
    return ar * br - ai * bi, ar * bi + ai * br


def _s5_scan_part(bufs, coef, base, seg_len, carry):
    xrf, xif, xrb, xib = bufs
    af_r, af_i, ab_r, ab_i = coef
    assert seg_len & (seg_len - 1) == 0

    def sweep(store, init):
        def step(i, st):
            fr, fi, br, bi = st
            rf = pl.ds(pl.multiple_of(base + S5_SEGS * i, S5_SEGS), S5_SEGS)
            rb = pl.ds(pl.multiple_of(base + S5_SEGS * (seg_len - 1 - i), S5_SEGS), S5_SEGS)
            nfr = af_r * fr - af_i * fi + xrf[rf, :]
            nfi = af_r * fi + af_i * fr + xif[rf, :]
            nbr = ab_r * br - ab_i * bi + xrb[rb, :]
            nbi = ab_r * bi + ab_i * br + xib[rb, :]
            if store:
                xrf[rf, :] = nfr
                xif[rf, :] = nfi
                xrb[rb, :] = nbr
                xib[rb, :] = nbi
            return nfr, nfi, nbr, nbi
        return lax.fori_loop(0, seg_len, step, init, unroll=4)

    zero = jnp.zeros(af_r.shape, F32)
    ef_r, ef_i, eb_r, eb_i = sweep(False, (zero,) * 4)
    pf = (af_r[0:1], af_i[0:1])
    pb = (ab_r[0:1], ab_i[0:1])
    for _ in range(seg_len.bit_length() - 1):
        pf = _cmul(*pf, *pf)
        pb = _cmul(*pb, *pb)
    cf_r, cf_i, cb_r, cb_i = carry
    ins_f, ins_b = [], []
    for j in range(S5_SEGS):
        ins_f.append((cf_r, cf_i))
        gr, gi = _cmul(*pf, cf_r, cf_i)
        cf_r, cf_i = ef_r[j:j + 1] + gr, ef_i[j:j + 1] + gi
    for j in range(S5_SEGS - 1, -1, -1):
        ins_b.append((cb_r, cb_i))
        gr, gi = _cmul(*pb, cb_r, cb_i)
        cb_r, cb_i = eb_r[j:j + 1] + gr, eb_i[j:j + 1] + gi
    ins_b.reverse()
    stack = lambda rows: jnp.concatenate(rows, axis=0)
    sweep(True, (stack([a for a, _ in ins_f]), stack([b for _, b in ins_f]),
                 stack([a for a, _ in ins_b]), stack([b for _, b in ins_b])))
    return cf_r, cf_i, cb_r, cb_i


def _seg_interleave(x):
    b, r, c = x.shape
    return x.reshape(b, S5_SEGS, r // S5_SEGS, c).transpose(0, 2, 1, 3).reshape(b, r, c)


def _seg_deinterleave(x):
    b, r, c = x.shape
    return x.reshape(b, r // S5_SEGS, S5_SEGS, c).transpose(0, 2, 1, 3).reshape(b, r, c)


def _s5_body(u_ref, bre_ref, bim_ref, cre_ref, cim_ref, disc_ref, dsk_ref, gw_ref, gb_ref, o_ref,
             xrf, xif, xrb, xib, y_scr, *, t_len):
    u = u_ref[0]
    ub = u.astype(BF16)
    y_scr[...] = u * dsk_ref[...]
    n_lat = t_len - CTX_LEN
    for half in range(2):
        ls = slice(half * S5_HALF, (half + 1) * S5_HALF)
        bu_r = jnp.dot(ub, bre_ref[:, ls], preferred_element_type=F32)
        bu_i = jnp.dot(ub, bim_ref[:, ls], preferred_element_type=F32)
        disc = disc_ref[:, ls]
        xrf[...] = disc[2:3] * bu_r - disc[3:4] * bu_i
        xif[...] = disc[2:3] * bu_i + disc[3:4] * bu_r
        xrb[...] = disc[6:7] * bu_r - disc[7:8] * bu_i
        xib[...] = disc[6:7] * bu_i + disc[7:8] * bu_r
        coef = tuple(jnp.broadcast_to(disc[r:r + 1], (S5_SEGS, S5_HALF)) for r in (0, 1, 4, 5))
        zero = jnp.zeros((1, S5_HALF), F32)
        carry = _s5_scan_part((xrf, xif, xrb, xib), coef, 0, CTX_LEN // S5_SEGS, (zero,) * 4)
        _s5_scan_part((xrf, xif, xrb, xib), coef, CTX_LEN, n_lat // S5_SEGS, carry)
        cre = cre_ref[ls, :]
        cim = cim_ref[ls, :]
        y_scr[...] += (jnp.dot(xrf[...].astype(BF16), cre, preferred_element_type=F32)
                       - jnp.dot(xif[...].astype(BF16), cim, preferred_element_type=F32)
                       + jnp.dot(xrb[...].astype(BF16), cre, preferred_element_type=F32)
                       - jnp.dot(xib[...].astype(BF16), cim, preferred_element_type=F32))
    y = jax.nn.gelu(y_scr[...])
    gate = jax.nn.sigmoid(jnp.dot(y.astype(BF16), gw_ref[...], preferred_element_type=F32) + gb_ref[...])
    o_ref[0] = y * gate


def _s5_mixer(z, lam_re, lam_im, log_dt, b_re, b_im, c_re, c_im, d_skip, glu_w, glu_b):
    bsz, t, _ = z.shape
    eye = jnp.eye(S5_GROUPS, dtype=F32)
    bbd = lambda w: jnp.einsum('gpc,gh->gchp', w.astype(F32), eye).reshape(GROUP_W, S5_LANES).astype(BF16)
    cbd = lambda w: jnp.einsum('gcp,gh->gphc', w.astype(F32), eye).reshape(S5_LANES, GROUP_W).astype(BF16)
    rows = []
    for d in range(2):
        lr, li = lam_re[d].astype(F32), lam_im[d].astype(F32)
        dt = jnp.exp(log_dt[d].astype(F32))[:, None]
        mag = jnp.exp(lr * dt)
        ab_re, ab_im = mag * jnp.cos(li * dt), mag * jnp.sin(li * dt)
        den = lr * lr + li * li
        co_re = ((ab_re - 1.0) * lr + ab_im * li) / den
        co_im = (ab_im * lr - (ab_re - 1.0) * li) / den
        rows += [ab_re, ab_im, co_re, co_im]
    disc = jnp.stack([r.reshape(S5_LANES) for r in rows], axis=0)
    full = lambda shape: pl.BlockSpec(shape, lambda b: (0,) * len(shape))
    u = z[:, :, S5_OFF:S5_OFF + GROUP_W]
    u = jnp.concatenate([_seg_interleave(u[:, :CTX_LEN]), _seg_interleave(u[:, CTX_LEN:])], axis=1)
    out = pl.pallas_call(
        functools.partial(_s5_body, t_len=t),
        grid=(bsz,),
        in_specs=[pl.BlockSpec((1, t, GROUP_W), lambda b: (b, 0, 0)),
                  full((GROUP_W, S5_LANES)), full((GROUP_W, S5_LANES)),
                  full((S5_LANES, GROUP_W)), full((S5_LANES, GROUP_W)),
                  full((8, S5_LANES)), full((1, GROUP_W)), full((GROUP_W, GROUP_W)), full((1, GROUP_W))],
        out_specs=pl.BlockSpec((1, t, GROUP_W), lambda b: (b, 0, 0)),
        out_shape=jax.ShapeDtypeStruct((bsz, t, GROUP_W), F32),
        scratch_shapes=[pltpu.VMEM((t, S5_HALF), F32)] * 4 + [pltpu.VMEM((t, GROUP_W), F32)],
        compiler_params=_cparams("arbitrary"),
    )(u, bbd(b_re), bbd(b_im), cbd(c_re), cbd(c_im), disc, d_skip.astype(F32).reshape(1, GROUP_W),
      glu_w.astype(BF16), glu_b.astype(F32).reshape(1, GROUP_W))
    return jnp.concatenate([_seg_deinterleave(out[:, :CTX_LEN]), _seg_deinterleave(out[:, CTX_LEN:])], axis=1)


def rms_norm(x, w):
    xf = x.astype(F32)
    y = xf * lax.rsqrt(jnp.mean(xf * xf, axis=-1, keepdims=True) + EPS)
    return (y * w.astype(F32)).astype(x.dtype)


def head_rms_norm(x, w, n_heads):
    shp = x.shape
    xh = x.reshape(shp[:-1] + (n_heads, -1))
    return rms_norm(xh, w.reshape(n_heads, -1)).reshape(shp)


def flip_seq(a, reverse):
    return a[:, ::-1] if reverse else a


def gla_chunked(q, k, v, log_f, s0, with_out=True):
    bsz, t_len, nh, _ = q.shape
    nc = t_len // HG_CHUNK
    def chunks(a):
        return a.astype(F32).reshape(bsz, nc, HG_CHUNK, nh, a.shape[-1])
    q, k, v, log_f = chunks(q), chunks(k), chunks(v), chunks(log_f)
    b = jnp.cumsum(log_f, axis=2)
    b_end = b[:, :, -1]
    ds = jnp.einsum('bcshd,bcshv->bchdv', k * jnp.exp(b_end[:, :, None] - b), v)
    def step(s, inp):
        g_c, ds_c = inp
        return jnp.exp(g_c)[..., None] * s + ds_c, s
    s_fin, s_start = lax.scan(step, s0, (jnp.moveaxis(b_end, 1, 0), jnp.moveaxis(ds, 1, 0)))
    if not with_out:
        return None, s_fin
    s_start = jnp.moveaxis(s_start, 0, 1)
    tri = jnp.tril(jnp.ones((HG_CHUNK, HG_CHUNK), bool))[None, None, :, :, None, None]
    decay = jnp.exp(jnp.where(tri, b[:, :, :, None] - b[:, :, None], -jnp.inf))
    scores = jnp.sum(q[:, :, :, None] * decay * k[:, :, None], axis=-1)
    o = (jnp.einsum('bctsh,bcshv->bcthv', scores, v)
         + jnp.einsum('bcthd,bchdv->bcthv', q * jnp.exp(b), s_start))
    return o.reshape(bsz, t_len, nh, -1), s_fin


def hgrn2_mixer(zc, zl, lb, norm_w, with_ctx_out):
    def heads(a):
        return a.reshape(a.shape[:2] + (HG_HEADS, -1))
    def split(z):
        q, i, ff, fb, g = jnp.split(z, 5, axis=-1)
        return heads(q) * HG_DK ** -0.5, heads(i), (heads(ff), heads(fb)), g
    qc, ic, fc, gc = split(zc)
    ql, il, fl, gl = split(zl)
    oc, ol = 0.0, 0.0
    for d in range(2):
        lbd = lb[d].reshape(HG_HEADS, HG_DK)
        f_c = lbd + (1.0 - lbd) * jax.nn.sigmoid(fc[d].astype(F32))
        f_l = lbd + (1.0 - lbd) * jax.nn.sigmoid(fl[d].astype(F32))
        s0 = jnp.zeros((zc.shape[0], HG_HEADS, HG_DK, HG_DV), F32)
        o_c, s_c = gla_chunked(flip_seq(qc, d), flip_seq(1.0 - f_c, d), flip_seq(ic, d),
                               flip_seq(jnp.log(f_c), d), s0, with_ctx_out)
        o_l, _ = gla_chunked(flip_seq(ql, d), flip_seq(1.0 - f_l, d), flip_seq(il, d),
                             flip_seq(jnp.log(f_l), d), s_c)
        ol = ol + flip_seq(o_l, d)
        if with_ctx_out:
            oc = oc + flip_seq(o_c, d)
    def readout(o, g):
        o = o.reshape(o.shape[:2] + (-1,)).astype(g.dtype)
        return head_rms_norm(o, norm_w, HG_HEADS) * jax.nn.silu(g)
    return (readout(oc, gc) if with_ctx_out else None), readout(ol, gl)


def mlstm_chunked(q, k, v, log_i, log_f, state, with_out=True):
    bsz, t_len, nh, _ = q.shape
    nc = t_len // ML_CHUNK
    def chunks(a):
        return a.astype(F32).reshape((bsz, nc, ML_CHUNK) + a.shape[2:])
    q, k, v, log_i, log_f = chunks(q), chunks(k), chunks(v), chunks(log_i), chunks(log_f)
    b = jnp.cumsum(log_f, axis=2)
    b_end = b[:, :, -1]
    w_end = b_end[:, :, None] - b + log_i
    m_loc = jnp.max(w_end, axis=2)
    e_end = jnp.exp(w_end - m_loc[:, :, None])
    c_loc = jnp.einsum('bcshv,bcshd->bchvd', e_end[..., None] * v, k)
    n_loc = jnp.einsum('bcsh,bcshd->bchd', e_end, k)
    def step(carry, inp):
        c_st, n_st, m_st = carry
        be, ml, cl, nl = inp
        m_new = jnp.maximum(be + m_st, ml)
        a_old, a_loc = jnp.exp(be + m_st - m_new), jnp.exp(ml - m_new)
        c_new = a_old[..., None, None] * c_st + a_loc[..., None, None] * cl
        n_new = a_old[..., None] * n_st + a_loc[..., None] * nl
        return (c_new, n_new, m_new), carry
    mv = lambda a: jnp.moveaxis(a, 1, 0)
    final, starts = lax.scan(step, state, (mv(b_end), mv(m_loc), mv(c_loc), mv(n_loc)))
    if not with_out:
        return None, final
    c_s, n_s, m_s = (jnp.moveaxis(a, 0, 1) for a in starts)
    tri = jnp.tril(jnp.ones((ML_CHUNK, ML_CHUNK), bool))[None, None, :, :, None]
    d_log = jnp.where(tri, b[:, :, :, None] - b[:, :, None] + log_i[:, :, None], -jnp.inf)
    w_inter = b + m_s[:, :, None]
    m_t = jnp.maximum(w_inter, jnp.max(d_log, axis=3))
    e_inter = jnp.exp(w_inter - m_t)
    wts = jnp.exp(d_log - m_t[:, :, :, None]) * jnp.einsum('bcthd,bcshd->bctsh', q, k)
    num = (jnp.einsum('bctsh,bcshv->bcthv', wts, v)
           + e_inter[..., None] * jnp.einsum('bcthd,bchvd->bcthv', q, c_s))
    den = jnp.sum(wts, axis=3) + e_inter * jnp.einsum('bcthd,bchd->bcth', q, n_s)
    h = num / jnp.maximum(jnp.abs(den), jnp.exp(-m_t))[..., None]
    return h.reshape(bsz, t_len, nh, -1), final


def mlstm_mixer(zc, zl, norm_w, with_ctx_out):
    def split(z):
        shp = z.shape[:2]
        q, k, v, o = (z[..., j * GROUP_W:(j + 1) * GROUP_W] for j in range(4))
        gates = z[..., 4 * GROUP_W:].astype(F32).reshape(shp + (4, ML_HEADS))
        hd = lambda a: a.reshape(shp + (ML_HEADS, -1))
        return hd(q), hd(k) * ML_DK ** -0.5, hd(v), o, gates
    qc, kc, vc, oc, gtc = split(zc)
    ql, kl, vl, ol, gtl = split(zl)
    bsz = zc.shape[0]
    hc, hl = 0.0, 0.0
    for d in range(2):
        state0 = (jnp.zeros((bsz, ML_HEADS, ML_DV, ML_DK), F32), jnp.zeros((bsz, ML_HEADS, ML_DK), F32),
                  jnp.zeros((bsz, ML_HEADS), F32))
        h_c, st_c = mlstm_chunked(flip_seq(qc, d), flip_seq(kc, d), flip_seq(vc, d), flip_seq(gtc[:, :, d], d),
                                  flip_seq(jax.nn.log_sigmoid(gtc[:, :, 2 + d]), d), state0, with_ctx_out)
        h_l, _ = mlstm_chunked(flip_seq(ql, d), flip_seq(kl, d), flip_seq(vl, d), flip_seq(gtl[:, :, d], d),
                               flip_seq(jax.nn.log_sigmoid(gtl[:, :, 2 + d]), d), st_c)
        hl = hl + flip_seq(h_l, d)
        if with_ctx_out:
            hc = hc + flip_seq(h_c, d)
    def readout(h, o):
        h = h.reshape(h.shape[:2] + (-1,)).astype(o.dtype)
        return head_rms_norm(h, norm_w, ML_HEADS) * jax.nn.sigmoid(o)
    return (readout(hc, oc) if with_ctx_out else None), readout(hl, ol)


def _split3(x):
    hi = x.astype(BF16)
    r = x - hi.astype(F32)
    mid = r.astype(BF16)
    return hi, mid, (r - mid.astype(F32)).astype(BF16)


def _dot3(x, sel):
    return sum(jnp.dot(p, sel, preferred_element_type=F32) for p in _split3(x))


def _mdot3(sel, x):
    return sum(jnp.dot(sel, p, preferred_element_type=F32) for p in _split3(x))


def _dot2(x, sel):
    hi = x.astype(BF16)
    lo = (x - hi.astype(F32)).astype(BF16)
    return jnp.dot(hi, sel, preferred_element_type=F32) + jnp.dot(lo, sel, preferred_element_type=F32)


def _head_blocks(n, seg):
    ri = lax.broadcasted_iota(jnp.int32, (n, n), 0)
    ci = lax.broadcasted_iota(jnp.int32, (n, n), 1)
    return ri // seg == ci // seg, ri, ci


def _chunk_cumsum_cols(x, chunk, reverse):
    same, ri, ci = _head_blocks(ROW_TILE, chunk)
    tri = jnp.where(same & ((ci >= ri) if reverse else (ci <= ri)), 1.0, 0.0).astype(BF16)
    return [_mdot3(tri, x[r:r + ROW_TILE]) for r in range(0, x.shape[0], ROW_TILE)]


def _repeat_row(row, n):
    return pl.ds(row, n, stride=0)


def _bwd_chunk(i, n_ctx, n_all):
    return jnp.where(i < n_ctx, n_ctx - 1 - i, n_all + n_ctx - 1 - i)


def _head_rms(x, ebd, seg):
    return x * lax.rsqrt(_dot2(x * x, ebd) * (1.0 / seg) + EPS)


def _hgrn2_body(z_ref, lb_ref, nw_ref, out_ref, bsc, ksc, vsc, osc, st, *, t_len):
    n_half = GROUP_W // LANES
    halves = lambda ref, idx, rows: jnp.concatenate([ref[idx + (h, rows)] for h in range(n_half)], axis=1)
    for h in range(n_half):
        vsc[h] = z_ref[0, :, GROUP_W + h * LANES:GROUP_W + (h + 1) * LANES]
    for d in range(2):
        lb = lb_ref[d:d + 1, :]
        f = lb + (1.0 - lb) * jax.nn.sigmoid(z_ref[0, :, (2 + d) * GROUP_W:(3 + d) * GROUP_W])
        k_all = 1.0 - f
        for h in range(n_half):
            ksc[d, h] = k_all[:, h * LANES:(h + 1) * LANES]
        for j, blk in enumerate(_chunk_cumsum_cols(jnp.log(f), HG_CHUNK, d == 1)):
            for h in range(n_half):
                bsc[d, h, j * ROW_TILE:(j + 1) * ROW_TILE, :] = blk[:, h * LANES:(h + 1) * LANES] * LOG2_E
    st[...] = jnp.zeros(st.shape, F32)
    bd, _, _ = _head_blocks(GROUP_W, HG_DK)
    ebd = jnp.where(bd, 1.0, 0.0).astype(BF16)
    tt = lax.broadcasted_iota(jnp.int32, (HG_CHUNK, GROUP_W), 0)
    n_all = t_len // HG_CHUNK
    n_ctx = CTX_LEN // HG_CHUNK

    def chunk(d, c):
        r0 = pl.multiple_of(c * HG_CHUNK, HG_CHUNK)
        rows = pl.ds(r0, HG_CHUNK)
        q = z_ref[0, rows, 0:GROUP_W] * (HG_DK ** -0.5)
        v = halves(vsc, (), rows)
        k = halves(ksc, (d,), rows)
        b = halves(bsc, (d,), rows)
        b_end = b[0:1] if d else b[HG_CHUNK - 1:HG_CHUNK]
        slabs = []
        for s in range(HG_CHUNK):
            row_s = _repeat_row(r0 + s, HG_CHUNK)
            valid = (tt <= s) if d else (tt >= s)
            slabs.append(jnp.where(valid, jnp.exp2(b - halves(bsc, (d,), row_s)), 0.0) * (q * halves(ksc, (d,), row_s)))
        a = jnp.concatenate(slabs, axis=0).astype(BF16)
        sc = jnp.dot(a, ebd, preferred_element_type=F32)
        o_intra = jnp.zeros((HG_CHUNK, GROUP_W), F32)
        for s in range(HG_CHUNK):
            o_intra = o_intra + (sc[s * HG_CHUNK:(s + 1) * HG_CHUNK]
                                 * halves(vsc, (), _repeat_row(r0 + s, HG_CHUNK)))
        s_prev = st[d]
        o_inter = lax.dot_general((q * jnp.exp2(b)).astype(BF16), s_prev.astype(BF16), (((1,), (1,)), ((), ())),
                                  preferred_element_type=F32)
        osc[d, rows, :] = o_intra + o_inter
        kd = (k * jnp.exp2(b_end - b)).astype(BF16)
        upd = lax.dot_general(v.astype(BF16), kd, (((0,), (0,)), ((), ())), preferred_element_type=F32)
        st[d] = s_prev * jnp.exp2(b_end) + jnp.where(bd, upd, 0.0)

    def step(i, carry):
        chunk(0, i)
        chunk(1, _bwd_chunk(i, n_ctx, n_all))
        return carry

    lax.fori_loop(0, n_all, step, 0, unroll=2)
    o = osc[0] + osc[1]
    g = z_ref[0, :, 4 * GROUP_W:5 * GROUP_W]
    out_ref[0] = _head_rms(o, ebd, HG_DV) * nw_ref[...] * (g * jax.nn.sigmoid(g))


def _hgrn2_mixer(z, lb, norm_w):
    bsz, t, _ = z.shape
    return pl.pallas_call(
        functools.partial(_hgrn2_body, t_len=t),
        grid=(bsz,),
        in_specs=[pl.BlockSpec((1, t, 5 * GROUP_W), lambda b: (b, 0, 0)),
                  pl.BlockSpec((2, GROUP_W), lambda b: (0, 0)),
                  pl.BlockSpec((1, GROUP_W), lambda b: (0, 0))],
        out_specs=pl.BlockSpec((1, t, GROUP_W), lambda b: (b, 0, 0)),
        out_shape=jax.ShapeDtypeStruct((bsz, t, GROUP_W), F32),
        scratch_shapes=[pltpu.VMEM((2, GROUP_W // LANES, t, LANES), F32), pltpu.VMEM((2, GROUP_W // LANES, t, LANES), F32),
                        pltpu.VMEM((GROUP_W // LANES, t, LANES), F32),
                        pltpu.VMEM((2, t, GROUP_W), F32), pltpu.VMEM((2, GROUP_W, GROUP_W), F32)],
        compiler_params=_cparams("arbitrary"),
    )(z, lb.astype(F32), norm_w.astype(F32).reshape(1, GROUP_W))


def _lane_expand(cols, seg):
    lane = lax.broadcasted_iota(jnp.int32, (1, len(cols) * seg), 1)
    out = cols[-1]
    for h in range(len(cols) - 2, -1, -1):
        out = jnp.where(lane < (h + 1) * seg, cols[h], out)
    return out


def _mlstm_body(q_ref, k_ref, v_ref, o_ref, zg_ref, gr_ref, nw_ref, out_ref,
                bce, lice, brow, hsc, cst, nst, *, t_len):
    n_all = t_len // ML_CHUNK
    n_ctx = CTX_LEN // ML_CHUNK
    g = zg_ref[0]
    lf = jax.nn.log_sigmoid(g)
    src = lax.broadcasted_iota(jnp.int32, (LANES, GROUP_W), 0)
    dst_head = lax.broadcasted_iota(jnp.int32, (LANES, GROUP_W), 1) // ML_DK
    bd, _, _ = _head_blocks(GROUP_W, ML_DK)
    ebd = jnp.where(bd, 1.0, 0.0).astype(BF16)
    for d in range(2):
        lice[d] = _dot3(g, jnp.where(src == ML_HEADS * d + dst_head, 1.0, 0.0).astype(BF16))
        lf_exp = _dot3(lf, jnp.where(src == ML_HEADS * (2 + d) + dst_head, 1.0, 0.0).astype(BF16))
        for j, blk in enumerate(_chunk_cumsum_cols(lf_exp, ML_CHUNK, d == 1)):
            bce[d, j * ROW_TILE:(j + 1) * ROW_TILE, :] = blk
        _, ri, ci = _head_blocks(GROUP_W, ML_CHUNK)
        tri = jnp.where(bd & ((ri >= ci) if d else (ri <= ci)), 1.0, 0.0).astype(BF16)
        brow[d] = _dot3(jax.nn.log_sigmoid(gr_ref[0, 2 + d]), tri)
    cst[...] = jnp.zeros(cst.shape, F32)
    nst[...] = jnp.zeros(nst.shape, F32)
    tt = lax.broadcasted_iota(jnp.int32, (ML_CHUNK, GROUP_W), 0)
    ss = lax.broadcasted_iota(jnp.int32, (ML_CHUNK, GROUP_W), 1) % ML_CHUNK
    lane_h = lax.broadcasted_iota(jnp.int32, (1, LANES), 1) < ML_CHUNK

    def chunk(d, c, m_prev):
        rows = pl.ds(pl.multiple_of(c * ML_CHUNK, ML_CHUNK), ML_CHUNK)
        q = q_ref[0, rows, :]
        k = k_ref[0, rows, :] * (ML_DK ** -0.5)
        v = v_ref[0, rows, :]
        bc = bce[d, rows, :]
        lic = lice[d, rows, :]
        br = brow[d, pl.ds(c, 1), :]
        lir = gr_ref[0, d, pl.ds(c, 1), :]
        b_end = bc[0:1] if d else bc[ML_CHUNK - 1:ML_CHUNK]
        valid = (ss >= tt) if d else (ss <= tt)
        d_log = jnp.where(valid, bc - br + lir, -jnp.inf)
        cols = []
        for hp in range(GROUP_W // LANES):
            xs = d_log[:, hp * LANES:(hp + 1) * LANES]
            cols.append(jnp.max(jnp.where(lane_h, xs, -jnp.inf), axis=-1, keepdims=True))
            cols.append(jnp.max(jnp.where(lane_h, -jnp.inf, xs), axis=-1, keepdims=True))
        w_inter = bc + m_prev
        m_t = jnp.maximum(w_inter, _lane_expand(cols, ML_CHUNK))
        e_inter = jnp.exp(w_inter - m_t)
        qb = q.astype(BF16)
        kbd = jnp.where(bd, jnp.concatenate([k] * ML_HEADS, axis=0).T, 0.0).astype(BF16)
        vbd = jnp.where(bd, jnp.concatenate([v] * ML_HEADS, axis=0), 0.0).astype(BF16)
        wts = jnp.exp(d_log - m_t) * jnp.dot(qb, kbd, preferred_element_type=F32)
        w_hi = wts.astype(BF16)
        w_lo = (wts - w_hi.astype(F32)).astype(BF16)
        c_prev, n_prev = cst[d], nst[d]
        num = (jnp.dot(w_hi, vbd, preferred_element_type=F32)
               + e_inter * jnp.dot(qb, c_prev.astype(BF16), preferred_element_type=F32))
        den = (jnp.dot(w_hi, ebd, preferred_element_type=F32) + jnp.dot(w_lo, ebd, preferred_element_type=F32)
               + e_inter * jnp.dot(qb, n_prev.astype(BF16), preferred_element_type=F32))
        hsc[d, rows, :] = num / jnp.maximum(jnp.abs(den), jnp.exp(-m_t))
        w_end = b_end - bc + lic
        m_loc = jnp.max(w_end, axis=0, keepdims=True)
        e_end = jnp.exp(w_end - m_loc)
        m_new = jnp.maximum(b_end + m_prev, m_loc)
        a_old = jnp.exp(b_end + m_prev - m_new)
        a_loc = jnp.exp(m_loc - m_new)
        evbd = jnp.where(bd, jnp.concatenate([e_end * v] * ML_HEADS, axis=0), 0.0).astype(BF16)
        eebd = jnp.where(bd, jnp.concatenate([e_end] * ML_HEADS, axis=0), 0.0).astype(BF16)
        cst[d] = a_old * c_prev + a_loc * jnp.dot(kbd, evbd, preferred_element_type=F32)
        nst[d] = a_old * n_prev + a_loc * jnp.dot(kbd, eebd, preferred_element_type=F32)
        return m_new

    def step(i, carry):
        return chunk(0, i, carry[0]), chunk(1, _bwd_chunk(i, n_ctx, n_all), carry[1])

    zero = jnp.zeros((1, GROUP_W), F32)
    lax.fori_loop(0, n_all, step, (zero, zero), unroll=2)
    out_ref[0] = _head_rms(hsc[0] + hsc[1], ebd, ML_DV) * nw_ref[...] * jax.nn.sigmoid(o_ref[0])


def _mlstm_mixer(z, gates_t, norm_w):
    bsz, t, _ = z.shape
    nc = t // ML_CHUNK
    ncp = -(-nc // 8) * 8
    gr = gates_t.reshape(bsz, 4, ML_HEADS, nc, ML_CHUNK).transpose(0, 1, 3, 2, 4).reshape(bsz, 4, nc, GROUP_W)
    gr = jnp.pad(gr, ((0, 0), (0, 0), (0, ncp - nc), (0, 0)))
    col = ML_OFF // GROUP_W
    zspec = lambda j: pl.BlockSpec((1, t, GROUP_W), lambda b: (b, 0, col + j))
    return pl.pallas_call(
        functools.partial(_mlstm_body, t_len=t),
        grid=(bsz,),
        in_specs=[zspec(0), zspec(1), zspec(2), zspec(3),
                  pl.BlockSpec((1, t, LANES), lambda b: (b, 0, ML_GATE_OFF // LANES)),
                  pl.BlockSpec((1, 4, ncp, GROUP_W), lambda b: (b, 0, 0, 0)),
                  pl.BlockSpec((1, GROUP_W), lambda b: (0, 0))],
        out_specs=pl.BlockSpec((1, t, GROUP_W), lambda b: (b, 0, 0)),
        out_shape=jax.ShapeDtypeStruct((bsz, t, GROUP_W), F32),
        scratch_shapes=[pltpu.VMEM((2, t, GROUP_W), F32), pltpu.VMEM((2, t, GROUP_W), F32),
                        pltpu.VMEM((2, ncp, GROUP_W), F32), pltpu.VMEM((2, t, GROUP_W), F32),
                        pltpu.VMEM((2, GROUP_W, GROUP_W), F32), pltpu.VMEM((2, GROUP_W, GROUP_W), F32)],
        compiler_params=_cparams("arbitrary"),
    )(z, z, z, z, z, gr, norm_w.astype(F32).reshape(1, GROUP_W))


def _final_norm_body(x_ref, w_ref, o_ref):
    o_ref[0] = _rms(x_ref[0]) * w_ref[...]


def _final_norm(h, w):
    bsz, t, d = h.shape
    n = t - CTX_LEN
    blk0 = CTX_LEN // ROW_TILE
    return pl.pallas_call(
        _final_norm_body,
        grid=(bsz, n // ROW_TILE),
        in_specs=[pl.BlockSpec((1, ROW_TILE, d), lambda b, j: (b, blk0 + j, 0)),
                  pl.BlockSpec((1, d), lambda b, j: (0, 0))],
        out_specs=pl.BlockSpec((1, ROW_TILE, d), lambda b, j: (b, j, 0)),
        out_shape=jax.ShapeDtypeStruct((bsz, n, d), F32),
        compiler_params=_cparams("arbitrary", "arbitrary"),
    )(h, w.reshape(1, d))


def kernel(x, c, ctx, c_ctx, mod_w, mod_b, norm1_w, norm2_w, w_in, b_in, hg_lb_logits, hg_norm_w,
           s5_lam_re, s5_lam_im, s5_log_dt, s5_b_re, s5_b_im, s5_c_re, s5_c_im, s5_d, s5_glu_w, s5_glu_b,
           da_lq1, da_lk1, da_lq2, da_lk2, da_norm_w, ml_norm_w, w_out, router_w,
           exp_w1, exp_w3, exp_w2, final_norm_w):
    bsz, seq, d = x.shape
    assert ctx.shape[1] == CTX_LEN == ROW_TILE and seq % ROW_TILE == 0 and d == D_MODEL
    t = CTX_LEN + seq
    lb_all = jnp.cumsum(jax.nn.softmax(hg_lb_logits.astype(F32), axis=0), axis=0)
    lb_all = lb_all - lb_all[0]

    n_rows = -(-(bsz + 1) // 8) * 8
    cv = jnp.zeros((n_rows, d), F32).at[:bsz].set(c).at[bsz].set(c_ctx)
    mod_all = _modulation(cv, mod_w, mod_b)
    cos_t, sin_t = _rope_tables(t)
    h = jnp.concatenate([ctx, x], axis=1)
    w1, w3, w2 = exp_w1.astype(BF16), exp_w3.astype(BF16), exp_w2.astype(BF16)

    for li in range(DEPTH):
        ctx_out = li < DEPTH - 1
        m6 = mod_all[li].reshape(n_rows, 6, d)
        modsel = jnp.stack([jnp.broadcast_to(m6[bsz], (bsz, 6, d)), m6[:bsz]], axis=1)
        w_pad = jnp.pad(w_in[li], ((0, 0), (0, IN_COLS_PAD - IN_COLS))).astype(BF16)
        b_pad = jnp.pad(b_in[li], (0, IN_COLS_PAD - IN_COLS)).reshape(1, IN_COLS_PAD)
        wg_t = w_in[li][:, ML_GATE_OFF:].T.astype(BF16)
        bg_col = b_in[li][ML_GATE_OFF:].reshape(N_GATES, 1)
        z, gates_t = _in_projection(h, modsel, norm1_w[li], w_pad, b_pad, wg_t, bg_col)

        mix_a = _hgrn2_mixer(z, lb_all[li], hg_norm_w[li])
        mix_b = _s5_mixer(z, s5_lam_re[li], s5_lam_im[li], s5_log_dt[li], s5_b_re[li], s5_b_im[li],
                          s5_c_re[li], s5_c_im[li], s5_d[li], s5_glu_w[li], s5_glu_b[li])
        lam_init = 0.8 - 0.6 * math.exp(-0.3 * li)
        att_args = (da_lq1[li], da_lk1[li], da_lq2[li], da_lk2[li], da_norm_w[li], lam_init)
        c_l = _diff_attention(z, CTX_LEN, seq, t, True, cos_t, sin_t, *att_args)
        if ctx_out:
            c_c = _diff_attention(z, 0, CTX_LEN, CTX_LEN, False, cos_t, sin_t, *att_args)
        else:
            c_c = jnp.zeros((bsz, CTX_LEN, GROUP_W), F32)
        mix_c = jnp.concatenate([c_c, c_l], axis=1)
        mix_d = _mlstm_mixer(z, gates_t, ml_norm_w[li])

        hm, xm, aff_t = _out_projection((mix_a, mix_b, mix_c, mix_d), h, modsel, norm2_w[li],
                                        w_out[li].astype(BF16), router_w[li].T.astype(BF16))
        slot, slot_t, starts = _route(aff_t, CTX_LEN, seq)
        y = _moe_ffn(xm, slot, starts, aff_t, CTX_LEN, seq, 1, w1, w3, w2, li)
        if not ctx_out:
            return _moe_combine(y, slot_t, hm, modsel, CTX_LEN, seq, 1, final_w=final_norm_w)
        h = _moe_combine(y, slot_t, hm, modsel, CTX_LEN, seq, 1)
        slot, slot_t, starts = _route(aff_t, 0, CTX_LEN)
        y = _moe_ffn(xm, slot, starts, aff_t, 0, CTX_LEN, bsz, w1, w3, w2, li)
        h = _moe_combine(y, slot_t, h, modsel, 0, CTX_LEN, 0)
```

```python
import functools
import math
import jax
import jax.numpy as jnp
from jax import lax
from jax.experimental import pallas as pl
from jax.experimental.pallas import tpu as pltpu

D_MODEL = 1024
DEPTH = 2
CTX_LEN = 256
GRID_W = 64
N_MIXERS = 4
GROUP_W = D_MODEL // N_MIXERS
MIX_W = N_MIXERS * GROUP_W
EPS = 1e-6
F32 = jnp.float32
BF16 = jnp.bfloat16
HG_HEADS = 4
HG_DK = GROUP_W // HG_HEADS
HG_DV = GROUP_W // HG_HEADS
HG_CHUNK = 16
S5_CH = 16
S5_GROUPS = GROUP_W // S5_CH
S5_STATE = 64
DA_HEADS = 4
DA_DQK = GROUP_W // (2 * DA_HEADS)
DA_DV = GROUP_W // DA_HEADS
ROPE_BASE = 10000.0
ML_HEADS = 4
ML_DK = GROUP_W // ML_HEADS
ML_DV = GROUP_W // ML_HEADS
ML_CHUNK = 64
N_EXPERTS = 16
EC_CAPACITY = 2
D_EXPERT = 2 * D_MODEL
HG_OFF = 0
S5_OFF = HG_OFF + 5 * GROUP_W
DA_OFF = S5_OFF + GROUP_W
ML_OFF = DA_OFF + 3 * GROUP_W
ML_GATE_OFF = ML_OFF + 4 * GROUP_W
IN_COLS = ML_GATE_OFF + 4 * ML_HEADS

LANES = 128
ROW_TILE = 256
IN_COLS_PAD = ML_GATE_OFF + LANES
N_GATES = 4 * ML_HEADS
S5_LANES = S5_GROUPS * S5_STATE
S5_HALF = S5_LANES // 2
S5_SEGS = 8
VMEM_LIMIT = 56 * 1024 * 1024
LOG2_E = math.log2(math.e)


def _cparams(*sem):
    return pltpu.CompilerParams(dimension_semantics=sem, vmem_limit_bytes=VMEM_LIMIT)


def _rms(x):
    return x * lax.rsqrt(jnp.mean(x * x, axis=-1, keepdims=True) + EPS)


def _mod_body(cv_ref, w_ref, b_ref, o_ref):
    cv = cv_ref[...]
    s = cv * jax.nn.sigmoid(cv)
    o_ref[0] = jnp.dot(s.astype(BF16), w_ref[0].astype(BF16), preferred_element_type=F32) + b_ref[0]


def _modulation(cv, mod_w, mod_b):
    n_l, d, n6 = mod_w.shape
    r = cv.shape[0]
    tn = n6 // 4
    return pl.pallas_call(
        _mod_body,
        grid=(n_l, 4),
        in_specs=[pl.BlockSpec((r, d), lambda l, j: (0, 0)),
                  pl.BlockSpec((1, d, tn), lambda l, j: (l, 0, j)),
                  pl.BlockSpec((1, 1, tn), lambda l, j: (l, 0, j))],
        out_specs=pl.BlockSpec((1, r, tn), lambda l, j: (l, 0, j)),
        out_shape=jax.ShapeDtypeStruct((n_l, r, n6), F32),
        compiler_params=_cparams("arbitrary", "arbitrary"),
    )(cv, mod_w, mod_b.reshape(n_l, 1, n6))


def _inproj_body(h_ref, m_ref, nw_ref, w_ref, b_ref, wg_ref, bg_ref, z_ref, gt_ref):
    m = m_ref[0, 0]
    xn = _rms(h_ref[0]) * nw_ref[...] * (1.0 + m[1:2]) + m[0:1]
    xb = xn.astype(BF16)
    z_ref[0] = jnp.dot(xb, w_ref[...], preferred_element_type=F32) + b_ref[...]
    gt_ref[0] = lax.dot_general(wg_ref[...], xb, (((1,), (1,)), ((), ())),
                                preferred_element_type=F32) + bg_ref[...]


def _in_projection(h, modsel, norm_w, w_pad, b_pad, wg_t, bg_col):
    bsz, t, d = h.shape
    nt = t // ROW_TILE
    return pl.pallas_call(
        _inproj_body,
        grid=(bsz, nt),
        in_specs=[pl.BlockSpec((1, ROW_TILE, d), lambda b, j: (b, j, 0)),
                  pl.BlockSpec((1, 1, 6, d), lambda b, j: (b, jnp.minimum(j, 1), 0, 0)),
                  pl.BlockSpec((1, d), lambda b, j: (0, 0)),
                  pl.BlockSpec((d, IN_COLS_PAD), lambda b, j: (0, 0)),
                  pl.BlockSpec((1, IN_COLS_PAD), lambda b, j: (0, 0)),
                  pl.BlockSpec((N_GATES, d), lambda b, j: (0, 0)),
                  pl.BlockSpec((N_GATES, 1), lambda b, j: (0, 0))],
        out_specs=[pl.BlockSpec((1, ROW_TILE, IN_COLS_PAD), lambda b, j: (b, j, 0)),
                   pl.BlockSpec((1, N_GATES, ROW_TILE), lambda b, j: (b, 0, j))],
        out_shape=[jax.ShapeDtypeStruct((bsz, t, IN_COLS_PAD), F32),
                   jax.ShapeDtypeStruct((bsz, N_GATES, t), F32)],
        compiler_params=_cparams("arbitrary", "arbitrary"),
    )(h, modsel, norm_w.reshape(1, d), w_pad, b_pad, wg_t, bg_col)


def _outproj_body(a_ref, b_ref, c_ref, d_ref, h_ref, m_ref, nw_ref, wo_ref, rw_ref, hm_ref, xm_ref, aff_ref):
    m = m_ref[0, 0]
    y = jnp.concatenate([a_ref[0], b_ref[0], c_ref[0], d_ref[0]], axis=-1).astype(BF16)
    hm = h_ref[0] + m[2:3] * jnp.dot(y, wo_ref[...], preferred_element_type=F32)
    hm_ref[0] = hm
    xm = (_rms(hm) * nw_ref[...] * (1.0 + m[4:5]) + m[3:4]).astype(BF16)
    xm_ref[0] = xm
    logit = lax.dot_general(rw_ref[...], xm, (((1,), (1,)), ((), ())), preferred_element_type=F32)
    e = jnp.exp(logit - jnp.max(logit, axis=0, keepdims=True))
    aff_ref[0] = e / jnp.sum(e, axis=0, keepdims=True)


def _out_projection(mix, h, modsel, norm_w, wo_bf, rw_t):
    bsz, t, d = h.shape
    nt = t // ROW_TILE
    mix_spec = pl.BlockSpec((1, ROW_TILE, GROUP_W), lambda b, j: (b, j, 0))
    row_spec = pl.BlockSpec((1, ROW_TILE, d), lambda b, j: (b, j, 0))
    return pl.pallas_call(
        _outproj_body,
        grid=(bsz, nt),
        in_specs=[mix_spec, mix_spec, mix_spec, mix_spec, row_spec,
                  pl.BlockSpec((1, 1, 6, d), lambda b, j: (b, jnp.minimum(j, 1), 0, 0)),
                  pl.BlockSpec((1, d), lambda b, j: (0, 0)),
                  pl.BlockSpec((MIX_W, d), lambda b, j: (0, 0)),
                  pl.BlockSpec((N_EXPERTS, d), lambda b, j: (0, 0))],
        out_specs=[row_spec, row_spec, pl.BlockSpec((1, N_EXPERTS, ROW_TILE), lambda b, j: (b, 0, j))],
        out_shape=[jax.ShapeDtypeStruct((bsz, t, d), F32), jax.ShapeDtypeStruct((bsz, t, d), BF16),
                   jax.ShapeDtypeStruct((bsz, N_EXPERTS, t), F32)],
        compiler_params=_cparams("arbitrary", "arbitrary"),
    )(*mix, h, modsel, norm_w.reshape(1, d), wo_bf, rw_t)


def _topk_body(aff_ref, slot_ref, slot_t_ref, *, cap, n, row0):
    aff = aff_ref[0, :, row0:row0 + n]
    bits = pltpu.bitcast(aff, jnp.int32)
    thr = jnp.zeros((N_EXPERTS, 1), jnp.int32)
    for bit in range(30, -1, -1):
        cand = thr | (1 << bit)
        cnt = jnp.sum(jnp.where(bits >= cand, 1.0, 0.0), axis=-1, keepdims=True)
        thr = jnp.where(cnt >= cap, cand, thr)
    room = cap - jnp.sum(jnp.where(bits > thr, 1.0, 0.0), axis=-1, keepdims=True)
    ri = lax.broadcasted_iota(jnp.int32, (LANES, LANES), 0)
    ci = lax.broadcasted_iota(jnp.int32, (LANES, LANES), 1)
    incl = jnp.where(ri <= ci, 1.0, 0.0).astype(BF16)
    off_eq = jnp.zeros((N_EXPERTS, 1), F32)
    off_sel = jnp.zeros((N_EXPERTS, 1), F32)
    pieces = []
    for j in range(n // LANES):
        sl = slice(j * LANES, (j + 1) * LANES)
        bits_b = bits[:, sl]
        eq_b = jnp.where(bits_b == thr, 1.0, 0.0)
        pos_eq = jnp.dot(eq_b.astype(BF16), incl, preferred_element_type=F32) - eq_b + off_eq
        sel = jnp.where(bits_b > thr, 1.0, jnp.where(pos_eq < room, eq_b, 0.0))
        pos_sel = jnp.dot(sel.astype(BF16), incl, preferred_element_type=F32) - sel + off_sel
        piece = jnp.where(sel > 0.5, pos_sel, -1.0)
        slot_ref[0, :, sl] = piece.astype(jnp.int32)
        pieces.append(piece)
        off_eq = off_eq + jnp.sum(eq_b, axis=-1, keepdims=True)
        off_sel = off_sel + jnp.sum(sel, axis=-1, keepdims=True)
    pad = jnp.full((LANES - N_EXPERTS, LANES), -1.0, F32)
    for j in range(n // LANES):
        tile = jnp.concatenate([pieces[j], pad], axis=0)
        slot_t_ref[0, j * LANES:(j + 1) * LANES, :] = tile.T.astype(jnp.int32)


def _route(aff_t, row0, n):
    bsz, _, t = aff_t.shape
    cap = EC_CAPACITY * n // N_EXPERTS
    return pl.pallas_call(
        functools.partial(_topk_body, cap=cap, n=n, row0=row0),
        grid=(bsz,),
        in_specs=[pl.BlockSpec((1, N_EXPERTS, t), lambda b: (b, 0, 0))],
        out_specs=[pl.BlockSpec((1, N_EXPERTS, n), lambda b: (b, 0, 0)),
                   pl.BlockSpec((1, n, LANES), lambda b: (b, 0, 0))],
        out_shape=[jax.ShapeDtypeStruct((bsz, N_EXPERTS, n), jnp.int32),
                   jax.ShapeDtypeStruct((bsz, n, LANES), jnp.int32)],
        compiler_params=_cparams("arbitrary"),
    )(aff_t)


def _moe_ffn_body(x_ref, slot_ref, aff_ref, w1_ref, w3_ref, w2_ref, y_ref, xe_scr, g_scr, *, nb, cap, n, off):
    for i in range(nb):
        hit = lax.broadcasted_iota(jnp.int32, (cap, n), 0) == slot_ref[i, 0]
        onehot = jnp.where(hit, 1.0, 0.0).astype(BF16)
        xe_scr[i * cap:(i + 1) * cap, :] = jnp.dot(onehot, x_ref[i, off:off + n, :],
                                                   preferred_element_type=F32).astype(BF16)
        g_scr[i * cap:(i + 1) * cap, :] = jnp.sum(jnp.where(hit, aff_ref[i, 0, :, off:off + n], 0.0),
                                                  axis=-1, keepdims=True)
    xe = xe_scr[...]
    h1 = jnp.dot(xe, w1_ref[0, 0], preferred_element_type=F32)
    h3 = jnp.dot(xe, w3_ref[0, 0], preferred_element_type=F32)
    act = (h1 * jax.nn.sigmoid(h1) * h3).astype(BF16)
    y = (jnp.dot(act, w2_ref[0, 0], preferred_element_type=F32) * g_scr[...]).astype(BF16)
    for i in range(nb):
        y_ref[i, 0] = y[i * cap:(i + 1) * cap]


def _moe_ffn(xm, slot, aff_t, row0, n, nb, w1, w3, w2, li):
    bsz, t, d = xm.shape
    cap = EC_CAPACITY * n // N_EXPERTS
    rows, off = (n, 0) if row0 % n == 0 else (t, row0)
    blk = row0 // n if off == 0 else 0
    slot4 = slot.reshape(bsz, N_EXPERTS, 1, n)
    aff4 = aff_t.reshape(bsz, N_EXPERTS, 1, t)
    return pl.pallas_call(
        functools.partial(_moe_ffn_body, nb=nb, cap=cap, n=n, off=off),
        grid=(N_EXPERTS, bsz // nb),
        in_specs=[pl.BlockSpec((nb, rows, d), lambda e, b: (b, blk, 0)),
                  pl.BlockSpec((nb, 1, 1, n), lambda e, b: (b, e, 0, 0)),
                  pl.BlockSpec((nb, 1, 1, rows), lambda e, b: (b, e, 0, blk)),
                  pl.BlockSpec((1, 1, d, D_EXPERT), lambda e, b: (li, e, 0, 0)),
                  pl.BlockSpec((1, 1, d, D_EXPERT), lambda e, b: (li, e, 0, 0)),
                  pl.BlockSpec((1, 1, D_EXPERT, d), lambda e, b: (li, e, 0, 0))],
        out_specs=pl.BlockSpec((nb, 1, cap, d), lambda e, b: (b, e, 0, 0)),
        out_shape=jax.ShapeDtypeStruct((bsz, N_EXPERTS, cap, d), BF16),
        scratch_shapes=[pltpu.VMEM((nb * cap, d), BF16), pltpu.VMEM((nb * cap, 1), F32)],
        compiler_params=_cparams("arbitrary", "arbitrary"),
    )(xm, slot4, aff4, w1, w3, w2)


def _moe_combine_body(y_ref, st_ref, hm_ref, m_ref, fw_ref, o_ref, *, cap, final):
    st = st_ref[0]
    lane = lax.broadcasted_iota(jnp.int32, (st.shape[0], cap), 1)
    hots = [jnp.where(lane == st[:, e:e + 1], 1.0, 0.0).astype(BF16) for e in range(N_EXPERTS)]
    if cap % LANES == 0:
        acc = jnp.dot(jnp.concatenate(hots, axis=1), y_ref[0].reshape(N_EXPERTS * cap, y_ref.shape[-1]),
                      preferred_element_type=F32)
    else:
        acc = sum(jnp.dot(hots[e], y_ref[0, e], preferred_element_type=F32) for e in range(N_EXPERTS))
    h = hm_ref[0] + m_ref[0, 0][5:6] * acc
    o_ref[0] = _rms(h) * fw_ref[...] if final else h


def _moe_combine(y, slot_t, hm, modsel, row0, n, is_latent, final_w=None):
    bsz, _, cap, d = y.shape
    nt = n // ROW_TILE
    blk0 = row0 // ROW_TILE
    final = final_w is not None
    fw = (final_w if final else jnp.ones((d,), F32)).reshape(1, d)
    return pl.pallas_call(
        functools.partial(_moe_combine_body, cap=cap, final=final),
        grid=(bsz, nt),
        in_specs=[pl.BlockSpec((1, N_EXPERTS, cap, d), lambda b, j: (b, 0, 0, 0)),
                  pl.BlockSpec((1, ROW_TILE, LANES), lambda b, j: (b, j, 0)),
                  pl.BlockSpec((1, ROW_TILE, d), lambda b, j: (b, blk0 + j, 0)),
                  pl.BlockSpec((1, 1, 6, d), lambda b, j: (b, is_latent, 0, 0)),
                  pl.BlockSpec((1, d), lambda b, j: (0, 0))],
        out_specs=pl.BlockSpec((1, ROW_TILE, d), lambda b, j: (b, j if final else blk0 + j, 0)),
        out_shape=jax.ShapeDtypeStruct((bsz, n, d) if final else hm.shape, F32),
        input_output_aliases={} if final else {2: 0},
        compiler_params=_cparams("arbitrary", "arbitrary"),
    )(y, slot_t, hm, modsel, fw)


def _swap_pairs(x):
    w = x.shape[-1]
    lane = lax.broadcasted_iota(jnp.int32, x.shape, x.ndim - 1)
    return jnp.where(lane % 2 == 0, pltpu.roll(x, w - 1, x.ndim - 1), pltpu.roll(x, 1, x.ndim - 1))


def _attn_body(q_ref, k_ref, v_ref, cq_ref, sq_ref, ck_ref, sk_ref, lq1_ref, lk1_ref, lq2_ref, lk2_ref, nw_ref,
               o_ref, k_scr, vt_scr, *, rope, lam_init):
    @pl.when(pl.program_id(1) == 0)
    def _():
        k = k_ref[0]
        if rope:
            k = k * ck_ref[...] + _swap_pairs(k) * sk_ref[...]
        k_scr[...] = k.astype(BF16)
        for r in range(0, k_scr.shape[0], ROW_TILE):
            vt_scr[:, r:r + ROW_TILE] = v_ref[0, r:r + ROW_TILE, :].T.astype(BF16)

    q = q_ref[0]
    if rope:
        q = q * cq_ref[...] + _swap_pairs(q) * sq_ref[...]
    q_t = (q * (DA_DQK ** -0.5 * LOG2_E)).T
    lam = (jnp.exp(jnp.sum(lq1_ref[...] * lk1_ref[...], axis=-1, keepdims=True))
           - jnp.exp(jnp.sum(lq2_ref[...] * lk2_ref[...], axis=-1, keepdims=True)) + lam_init)
    row = lax.broadcasted_iota(jnp.int32, (GROUP_W, 1), 0)
    kk = k_scr[...]
    outs = []
    for h in range(DA_HEADS):
        qm = jnp.concatenate([jnp.where(row // DA_DQK == 2 * h + mi, q_t, 0.0) for mi in range(2)], axis=1)
        s = jnp.dot(kk, qm.astype(BF16), preferred_element_type=F32)
        e = jnp.exp2(s - jnp.max(s, axis=0, keepdims=True))
        pv = (jnp.dot(vt_scr[h * DA_DV:(h + 1) * DA_DV, :], e.astype(BF16), preferred_element_type=F32)
              / jnp.sum(e, axis=0, keepdims=True))
        nq = q_t.shape[1]
        oh = pv[:, :nq] - lam * pv[:, nq:]
        outs.append(oh * lax.rsqrt(jnp.mean(oh * oh, axis=0, keepdims=True) + EPS))
    o_ref[0] = jnp.concatenate(outs, axis=0).T * nw_ref[...] * (1.0 - lam_init)


def _diff_attention(z, q_row0, nq, nk, rope, cos_t, sin_t, lq1, lk1, lq2, lk2, norm_w, lam_init):
    bsz = z.shape[0]
    qb = ROW_TILE
    q0 = q_row0 // qb
    col = DA_OFF // GROUP_W
    vec = lambda a: a.reshape(1, -1)
    small = pl.BlockSpec((1, DA_DQK), lambda b, j: (0, 0))
    return pl.pallas_call(
        functools.partial(_attn_body, rope=rope, lam_init=lam_init),
        grid=(bsz, nq // qb),
        in_specs=[pl.BlockSpec((1, qb, GROUP_W), lambda b, j: (b, q0 + j, col)),
                  pl.BlockSpec((1, nk, GROUP_W), lambda b, j: (b, 0, col + 1)),
                  pl.BlockSpec((1, nk, GROUP_W), lambda b, j: (b, 0, col + 2)),
                  pl.BlockSpec((qb, GROUP_W), lambda b, j: (q0 + j, 0)),
                  pl.BlockSpec((qb, GROUP_W), lambda b, j: (q0 + j, 0)),
                  pl.BlockSpec((nk, GROUP_W), lambda b, j: (0, 0)),
                  pl.BlockSpec((nk, GROUP_W), lambda b, j: (0, 0)),
                  small, small, small, small,
                  pl.BlockSpec((1, GROUP_W), lambda b, j: (0, 0))],
        out_specs=pl.BlockSpec((1, qb, GROUP_W), lambda b, j: (b, j, 0)),
        out_shape=jax.ShapeDtypeStruct((bsz, nq, GROUP_W), F32),
        scratch_shapes=[pltpu.VMEM((nk, GROUP_W), BF16), pltpu.VMEM((GROUP_W, nk), BF16)],
        compiler_params=_cparams("arbitrary", "arbitrary"),
    )(z, z, z, cos_t, sin_t, cos_t, sin_t, vec(lq1), vec(lk1), vec(lq2), vec(lk2), vec(norm_w))


def _rope_tables(t):
    n = t - CTX_LEN
    axis_dim = DA_DQK // 2
    inv = ROPE_BASE ** (-jnp.arange(0, axis_dim, 2, dtype=F32) / axis_dim)
    tok = jnp.arange(n, dtype=jnp.int32)
    row = (tok // GRID_W).astype(F32)
    colp = (tok % GRID_W).astype(F32)
    ang = jnp.concatenate([row[:, None] * inv, colp[:, None] * inv], axis=-1)
    cos = jnp.repeat(jnp.cos(ang), 2, axis=-1)
    sin = jnp.repeat(jnp.sin(ang), 2, axis=-1) * jnp.tile(jnp.array([-1.0, 1.0], F32), DA_DQK // 2)
    reps = GROUP_W // DA_DQK
    cos = jnp.concatenate([jnp.ones((CTX_LEN, GROUP_W), F32), jnp.tile(cos, (1, reps))], axis=0)
    sin = jnp.concatenate([jnp.zeros((CTX_LEN, GROUP_W), F32), jnp.tile(sin, (1, reps))], axis=0)
    return cos, sin


def _cmul(ar, ai, br, bi):
    return ar * br - ai * bi, ar * bi + ai * br


def _s5_scan_part(bufs, coef, base, seg_len, carry):
    xrf, xif, xrb, xib = bufs
    af_r, af_i, ab_r, ab_i = coef
    assert seg_len & (seg_len - 1) == 0

    def sweep(store, init):
        def step(i, st):
            fr, fi, br, bi = st
            rf = pl.ds(pl.multiple_of(base + S5_SEGS * i, S5_SEGS), S5_SEGS)
            rb = pl.ds(pl.multiple_of(base + S5_SEGS * (seg_len - 1 - i), S5_SEGS), S5_SEGS)
            nfr = af_r * fr - af_i * fi + xrf[rf, :]
            nfi = af_r * fi + af_i * fr + xif[rf, :]
            nbr = ab_r * br - ab_i * bi + xrb[rb, :]
            nbi = ab_r * bi + ab_i * br + xib[rb, :]
            if store:
                xrf[rf, :] = nfr
                xif[rf, :] = nfi
                xrb[rb, :] = nbr
                xib[rb, :] = nbi
            return nfr, nfi, nbr, nbi
        return lax.fori_loop(0, seg_len, step, init, unroll=4)

    zero = jnp.zeros(af_r.shape, F32)
    ef_r, ef_i, eb_r, eb_i = sweep(False, (zero,) * 4)
    pf = (af_r[0:1], af_i[0:1])
    pb = (ab_r[0:1], ab_i[0:1])
    for _ in range(seg_len.bit_length() - 1):
        pf = _cmul(*pf, *pf)
        pb = _cmul(*pb, *pb)
    cf_r, cf_i, cb_r, cb_i = carry
    ins_f, ins_b = [], []
    for j in range(S5_SEGS):
        ins_f.append((cf_r, cf_i))
        gr, gi = _cmul(*pf, cf_r, cf_i)
        cf_r, cf_i = ef_r[j:j + 1] + gr, ef_i[j:j + 1] + gi
    for j in range(S5_SEGS - 1, -1, -1):
        ins_b.append((cb_r, cb_i))
        gr, gi = _cmul(*pb, cb_r, cb_i)
        cb_r, cb_i = eb_r[j:j + 1] + gr, eb_i[j:j + 1] + gi
    ins_b.reverse()
    stack = lambda rows: jnp.concatenate(rows, axis=0)
    sweep(True, (stack([a for a, _ in ins_f]), stack([b for _, b in ins_f]),
                 stack([a for a, _ in ins_b]), stack([b for _, b in ins_b])))
    return cf_r, cf_i, cb_r, cb_i


def _seg_interleave(x):
    b, r, c = x.shape
    return x.reshape(b, S5_SEGS, r // S5_SEGS, c).transpose(0, 2, 1, 3).reshape(b, r, c)


def _seg_deinterleave(x):
    b, r, c = x.shape
    return x.reshape(b, r // S5_SEGS, S5_SEGS, c).transpose(0, 2, 1, 3).reshape(b, r, c)


def _s5_body(u_ref, bre_ref, bim_ref, cre_ref, cim_ref, disc_ref, dsk_ref, gw_ref, gb_ref, o_ref,
             xrf, xif, xrb, xib, y_scr, *, t_len):
    u = u_ref[0]
    ub = u.astype(BF16)
    y_scr[...] = u * dsk_ref[...]
    n_lat = t_len - CTX_LEN
    for half in range(2):
        ls = slice(half * S5_HALF, (half + 1) * S5_HALF)
        bu_r = jnp.dot(ub, bre_ref[:, ls], preferred_element_type=F32)
        bu_i = jnp.dot(ub, bim_ref[:, ls], preferred_element_type=F32)
        disc = disc_ref[:, ls]
        xrf[...] = disc[2:3] * bu_r - disc[3:4] * bu_i
        xif[...] = disc[2:3] * bu_i + disc[3:4] * bu_r
        xrb[...] = disc[6:7] * bu_r - disc[7:8] * bu_i
        xib[...] = disc[6:7] * bu_i + disc[7:8] * bu_r
        coef = tuple(jnp.broadcast_to(disc[r:r + 1], (S5_SEGS, S5_HALF)) for r in (0, 1, 4, 5))
        zero = jnp.zeros((1, S5_HALF), F32)
        carry = _s5_scan_part((xrf, xif, xrb, xib), coef, 0, CTX_LEN // S5_SEGS, (zero,) * 4)
        _s5_scan_part((xrf, xif, xrb, xib), coef, CTX_LEN, n_lat // S5_SEGS, carry)
        cre = cre_ref[ls, :]
        cim = cim_ref[ls, :]
        y_scr[...] += (jnp.dot(xrf[...].astype(BF16), cre, preferred_element_type=F32)
                       - jnp.dot(xif[...].astype(BF16), cim, preferred_element_type=F32)
                       + jnp.dot(xrb[...].astype(BF16), cre, preferred_element_type=F32)
                       - jnp.dot(xib[...].astype(BF16), cim, preferred_element_type=F32))
    y = jax.nn.gelu(y_scr[...])
    gate = jax.nn.sigmoid(jnp.dot(y.astype(BF16), gw_ref[...], preferred_element_type=F32) + gb_ref[...])
    o_ref[0] = y * gate


def _s5_mixer(z, lam_re, lam_im, log_dt, b_re, b_im, c_re, c_im, d_skip, glu_w, glu_b):
    bsz, t, _ = z.shape
    eye = jnp.eye(S5_GROUPS, dtype=F32)
    bbd = lambda w: jnp.einsum('gpc,gh->gchp', w.astype(F32), eye).reshape(GROUP_W, S5_LANES).astype(BF16)
    cbd = lambda w: jnp.einsum('gcp,gh->gphc', w.astype(F32), eye).reshape(S5_LANES, GROUP_W).astype(BF16)
    rows = []
    for d in range(2):
        lr, li = lam_re[d].astype(F32), lam_im[d].astype(F32)
        dt = jnp.exp(log_dt[d].astype(F32))[:, None]
        mag = jnp.exp(lr * dt)
        ab_re, ab_im = mag * jnp.cos(li * dt), mag * jnp.sin(li * dt)
        den = lr * lr + li * li
        co_re = ((ab_re - 1.0) * lr + ab_im * li) / den
        co_im = (ab_im * lr - (ab_re - 1.0) * li) / den
        rows += [ab_re, ab_im, co_re, co_im]
    disc = jnp.stack([r.reshape(S5_LANES) for r in rows], axis=0)
    full = lambda shape: pl.BlockSpec(shape, lambda b: (0,) * len(shape))
    u = z[:, :, S5_OFF:S5_OFF + GROUP_W]
    u = jnp.concatenate([_seg_interleave(u[:, :CTX_LEN]), _seg_interleave(u[:, CTX_LEN:])], axis=1)
    out = pl.pallas_call(
        functools.partial(_s5_body, t_len=t),
        grid=(bsz,),
        in_specs=[pl.BlockSpec((1, t, GROUP_W), lambda b: (b, 0, 0)),
                  full((GROUP_W, S5_LANES)), full((GROUP_W, S5_LANES)),
                  full((S5_LANES, GROUP_W)), full((S5_LANES, GROUP_W)),
                  full((8, S5_LANES)), full((1, GROUP_W)), full((GROUP_W, GROUP_W)), full((1, GROUP_W))],
        out_specs=pl.BlockSpec((1, t, GROUP_W), lambda b: (b, 0, 0)),
        out_shape=jax.ShapeDtypeStruct((bsz, t, GROUP_W), F32),
        scratch_shapes=[pltpu.VMEM((t, S5_HALF), F32)] * 4 + [pltpu.VMEM((t, GROUP_W), F32)],
        compiler_params=_cparams("arbitrary"),
    )(u, bbd(b_re), bbd(b_im), cbd(c_re), cbd(c_im), disc, d_skip.astype(F32).reshape(1, GROUP_W),
      glu_w.astype(BF16), glu_b.astype(F32).reshape(1, GROUP_W))
    return jnp.concatenate([_seg_deinterleave(out[:, :CTX_LEN]), _seg_deinterleave(out[:, CTX_LEN:])], axis=1)


def rms_norm(x, w):
    xf = x.astype(F32)
    y = xf * lax.rsqrt(jnp.mean(xf * xf, axis=-1, keepdims=True) + EPS)
    return (y * w.astype(F32)).astype(x.dtype)


def head_rms_norm(x, w, n_heads):
    shp = x.shape
    xh = x.reshape(shp[:-1] + (n_heads, -1))
    return rms_norm(xh, w.reshape(n_heads, -1)).reshape(shp)


def flip_seq(a, reverse):
    return a[:, ::-1] if reverse else a


def gla_chunked(q, k, v, log_f, s0, with_out=True):
    bsz, t_len, nh, _ = q.shape
    nc = t_len // HG_CHUNK
    def chunks(a):
        return a.astype(F32).reshape(bsz, nc, HG_CHUNK, nh, a.shape[-1])
    q, k, v, log_f = chunks(q), chunks(k), chunks(v), chunks(log_f)
    b = jnp.cumsum(log_f, axis=2)
    b_end = b[:, :, -1]
    ds = jnp.einsum('bcshd,bcshv->bchdv', k * jnp.exp(b_end[:, :, None] - b), v)
    def step(s, inp):
        g_c, ds_c = inp
        return jnp.exp(g_c)[..., None] * s + ds_c, s
    s_fin, s_start = lax.scan(step, s0, (jnp.moveaxis(b_end, 1, 0), jnp.moveaxis(ds, 1, 0)))
    if not with_out:
        return None, s_fin
    s_start = jnp.moveaxis(s_start, 0, 1)
    tri = jnp.tril(jnp.ones((HG_CHUNK, HG_CHUNK), bool))[None, None, :, :, None, None]
    decay = jnp.exp(jnp.where(tri, b[:, :, :, None] - b[:, :, None], -jnp.inf))
    scores = jnp.sum(q[:, :, :, None] * decay * k[:, :, None], axis=-1)
    o = (jnp.einsum('bctsh,bcshv->bcthv', scores, v)
         + jnp.einsum('bcthd,bchdv->bcthv', q * jnp.exp(b), s_start))
    return o.reshape(bsz, t_len, nh, -1), s_fin


def hgrn2_mixer(zc, zl, lb, norm_w, with_ctx_out):
    def heads(a):
        return a.reshape(a.shape[:2] + (HG_HEADS, -1))
    def split(z):
        q, i, ff, fb, g = jnp.split(z, 5, axis=-1)
        return heads(q) * HG_DK ** -0.5, heads(i), (heads(ff), heads(fb)), g
    qc, ic, fc, gc = split(zc)
    ql, il, fl, gl = split(zl)
    oc, ol = 0.0, 0.0
    for d in range(2):
        lbd = lb[d].reshape(HG_HEADS, HG_DK)
        f_c = lbd + (1.0 - lbd) * jax.nn.sigmoid(fc[d].astype(F32))
        f_l = lbd + (1.0 - lbd) * jax.nn.sigmoid(fl[d].astype(F32))
        s0 = jnp.zeros((zc.shape[0], HG_HEADS, HG_DK, HG_DV), F32)
        o_c, s_c = gla_chunked(flip_seq(qc, d), flip_seq(1.0 - f_c, d), flip_seq(ic, d),
                               flip_seq(jnp.log(f_c), d), s0, with_ctx_out)
        o_l, _ = gla_chunked(flip_seq(ql, d), flip_seq(1.0 - f_l, d), flip_seq(il, d),
                             flip_seq(jnp.log(f_l), d), s_c)
        ol = ol + flip_seq(o_l, d)
        if with_ctx_out:
            oc = oc + flip_seq(o_c, d)
    def readout(o, g):
        o = o.reshape(o.shape[:2] + (-1,)).astype(g.dtype)
        return head_rms_norm(o, norm_w, HG_HEADS) * jax.nn.silu(g)
    return (readout(oc, gc) if with_ctx_out else None), readout(ol, gl)


def mlstm_chunked(q, k, v, log_i, log_f, state, with_out=True):
    bsz, t_len, nh, _ = q.shape
    nc = t_len // ML_CHUNK
    def chunks(a):
        return a.astype(F32).reshape((bsz, nc, ML_CHUNK) + a.shape[2:])
    q, k, v, log_i, log_f = chunks(q), chunks(k), chunks(v), chunks(log_i), chunks(log_f)
    b = jnp.cumsum(log_f, axis=2)
    b_end = b[:, :, -1]
    w_end = b_end[:, :, None] - b + log_i
    m_loc = jnp.max(w_end, axis=2)
    e_end = jnp.exp(w_end - m_loc[:, :, None])
    c_loc = jnp.einsum('bcshv,bcshd->bchvd', e_end[..., None] * v, k)
    n_loc = jnp.einsum('bcsh,bcshd->bchd', e_end, k)
    def step(carry, inp):
        c_st, n_st, m_st = carry
        be, ml, cl, nl = inp
        m_new = jnp.maximum(be + m_st, ml)
        a_old, a_loc = jnp.exp(be + m_st - m_new), jnp.exp(ml - m_new)
        c_new = a_old[..., None, None] * c_st + a_loc[..., None, None] * cl
        n_new = a_old[..., None] * n_st + a_loc[..., None] * nl
        return (c_new, n_new, m_new), carry
    mv = lambda a: jnp.moveaxis(a, 1, 0)
    final, starts = lax.scan(step, state, (mv(b_end), mv(m_loc), mv(c_loc), mv(n_loc)))
    if not with_out:
        return None, final
    c_s, n_s, m_s = (jnp.moveaxis(a, 0, 1) for a in starts)
    tri = jnp.tril(jnp.ones((ML_CHUNK, ML_CHUNK), bool))[None, None, :, :, None]
    d_log = jnp.where(tri, b[:, :, :, None] - b[:, :, None] + log_i[:, :, None], -jnp.inf)
    w_inter = b + m_s[:, :, None]
    m_t = jnp.maximum(w_inter, jnp.max(d_log, axis=3))
    e_inter = jnp.exp(w_inter - m_t)
    wts = jnp.exp(d_log - m_t[:, :, :, None]) * jnp.einsum('bcthd,bcshd->bctsh', q, k)
    num = (jnp.einsum('bctsh,bcshv->bcthv', wts, v)
           + e_inter[..., None] * jnp.einsum('bcthd,bchvd->bcthv', q, c_s))
    den = jnp.sum(wts, axis=3) + e_inter * jnp.einsum('bcthd,bchd->bcth', q, n_s)
    h = num / jnp.maximum(jnp.abs(den), jnp.exp(-m_t))[..., None]
    return h.reshape(bsz, t_len, nh, -1), final


def mlstm_mixer(zc, zl, norm_w, with_ctx_out):
    def split(z):
        shp = z.shape[:2]
        q, k, v, o = (z[..., j * GROUP_W:(j + 1) * GROUP_W] for j in range(4))
        gates = z[..., 4 * GROUP_W:].astype(F32).reshape(shp + (4, ML_HEADS))
        hd = lambda a: a.reshape(shp + (ML_HEADS, -1))
        return hd(q), hd(k) * ML_DK ** -0.5, hd(v), o, gates
    qc, kc, vc, oc, gtc = split(zc)
    ql, kl, vl, ol, gtl = split(zl)
    bsz = zc.shape[0]
    hc, hl = 0.0, 0.0
    for d in range(2):
        state0 = (jnp.zeros((bsz, ML_HEADS, ML_DV, ML_DK), F32), jnp.zeros((bsz, ML_HEADS, ML_DK), F32),
                  jnp.zeros((bsz, ML_HEADS), F32))
        h_c, st_c = mlstm_chunked(flip_seq(qc, d), flip_seq(kc, d), flip_seq(vc, d), flip_seq(gtc[:, :, d], d),
                                  flip_seq(jax.nn.log_sigmoid(gtc[:, :, 2 + d]), d), state0, with_ctx_out)
        h_l, _ = mlstm_chunked(flip_seq(ql, d), flip_seq(kl, d), flip_seq(vl, d), flip_seq(gtl[:, :, d], d),
                               flip_seq(jax.nn.log_sigmoid(gtl[:, :, 2 + d]), d), st_c)
        hl = hl + flip_seq(h_l, d)
        if with_ctx_out:
            hc = hc + flip_seq(h_c, d)
    def readout(h, o):
        h = h.reshape(h.shape[:2] + (-1,)).astype(o.dtype)
        return head_rms_norm(h, norm_w, ML_HEADS) * jax.nn.sigmoid(o)
    return (readout(hc, oc) if with_ctx_out else None), readout(hl, ol)


def _split3(x):
    hi = x.astype(BF16)
    r = x - hi.astype(F32)
    mid = r.astype(BF16)
    return hi, mid, (r - mid.astype(F32)).astype(BF16)


def _dot3(x, sel):
    return sum(jnp.dot(p, sel, preferred_element_type=F32) for p in _split3(x))


def _mdot3(sel, x):
    return sum(jnp.dot(sel, p, preferred_element_type=F32) for p in _split3(x))


def _dot2(x, sel):
    hi = x.astype(BF16)
    lo = (x - hi.astype(F32)).astype(BF16)
    return jnp.dot(hi, sel, preferred_element_type=F32) + jnp.dot(lo, sel, preferred_element_type=F32)


def _head_blocks(n, seg):
    ri = lax.broadcasted_iota(jnp.int32, (n, n), 0)
    ci = lax.broadcasted_iota(jnp.int32, (n, n), 1)
    return ri // seg == ci // seg, ri, ci


def _chunk_cumsum_cols(x, chunk, reverse):
    same, ri, ci = _head_blocks(ROW_TILE, chunk)
    tri = jnp.where(same & ((ci >= ri) if reverse else (ci <= ri)), 1.0, 0.0).astype(BF16)
    return [_mdot3(tri, x[r:r + ROW_TILE]) for r in range(0, x.shape[0], ROW_TILE)]


def _repeat_row(row, n):
    return pl.ds(row, n, stride=0)


def _bwd_chunk(i, n_ctx, n_all):
    return jnp.where(i < n_ctx, n_ctx - 1 - i, n_all + n_ctx - 1 - i)


def _head_rms(x, ebd, seg):
    return x * lax.rsqrt(_dot2(x * x, ebd) * (1.0 / seg) + EPS)


def _hgrn2_body(z_ref, lb_ref, nw_ref, out_ref, bsc, ksc, vsc, osc, st, *, t_len):
    n_half = GROUP_W // LANES
    halves = lambda ref, idx, rows: jnp.concatenate([ref[idx + (h, rows)] for h in range(n_half)], axis=1)
    for h in range(n_half):
        vsc[h] = z_ref[0, :, GROUP_W + h * LANES:GROUP_W + (h + 1) * LANES]
    for d in range(2):
        lb = lb_ref[d:d + 1, :]
        f = lb + (1.0 - lb) * jax.nn.sigmoid(z_ref[0, :, (2 + d) * GROUP_W:(3 + d) * GROUP_W])
        k_all = 1.0 - f
        for h in range(n_half):
            ksc[d, h] = k_all[:, h * LANES:(h + 1) * LANES]
        for j, blk in enumerate(_chunk_cumsum_cols(jnp.log(f), HG_CHUNK, d == 1)):
            for h in range(n_half):
                bsc[d, h, j * ROW_TILE:(j + 1) * ROW_TILE, :] = blk[:, h * LANES:(h + 1) * LANES] * LOG2_E
    st[...] = jnp.zeros(st.shape, F32)
    bd, _, _ = _head_blocks(GROUP_W, HG_DK)
    ebd = jnp.where(bd, 1.0, 0.0).astype(BF16)
    tt = lax.broadcasted_iota(jnp.int32, (HG_CHUNK, GROUP_W), 0)
    n_all = t_len // HG_CHUNK
    n_ctx = CTX_LEN // HG_CHUNK

    def chunk(d, c):
        r0 = pl.multiple_of(c * HG_CHUNK, HG_CHUNK)
        rows = pl.ds(r0, HG_CHUNK)
        q = z_ref[0, rows, 0:GROUP_W] * (HG_DK ** -0.5)
        v = halves(vsc, (), rows)
        k = halves(ksc, (d,), rows)
        b = halves(bsc, (d,), rows)
        b_end = b[0:1] if d else b[HG_CHUNK - 1:HG_CHUNK]
        slabs = []
        for s in range(HG_CHUNK):
            row_s = _repeat_row(r0 + s, HG_CHUNK)
            valid = (tt <= s) if d else (tt >= s)
            slabs.append(jnp.where(valid, jnp.exp2(b - halves(bsc, (d,), row_s)), 0.0) * (q * halves(ksc, (d,), row_s)))
        a = jnp.concatenate(slabs, axis=0).astype(BF16)
        sc = jnp.dot(a, ebd, preferred_element_type=F32)
        o_intra = jnp.zeros((HG_CHUNK, GROUP_W), F32)
        for s in range(HG_CHUNK):
            o_intra = o_intra + (sc[s * HG_CHUNK:(s + 1) * HG_CHUNK]
                                 * halves(vsc, (), _repeat_row(r0 + s, HG_CHUNK)))
        s_prev = st[d]
        o_inter = lax.dot_general((q * jnp.exp2(b)).astype(BF16), s_prev.astype(BF16), (((1,), (1,)), ((), ())),
                                  preferred_element_type=F32)
        osc[d, rows, :] = o_intra + o_inter
        kd = (k * jnp.exp2(b_end - b)).astype(BF16)
        upd = lax.dot_general(v.astype(BF16), kd, (((0,), (0,)), ((), ())), preferred_element_type=F32)
        st[d] = s_prev * jnp.exp2(b_end) + jnp.where(bd, upd, 0.0)

    def step(i, carry):
        chunk(0, i)
        chunk(1, _bwd_chunk(i, n_ctx, n_all))
        return carry

    lax.fori_loop(0, n_all, step, 0, unroll=4)
    o = osc[0] + osc[1]
    g = z_ref[0, :, 4 * GROUP_W:5 * GROUP_W]
    out_ref[0] = _head_rms(o, ebd, HG_DV) * nw_ref[...] * (g * jax.nn.sigmoid(g))


def _hgrn2_mixer(z, lb, norm_w):
    bsz, t, _ = z.shape
    return pl.pallas_call(
        functools.partial(_hgrn2_body, t_len=t),
        grid=(bsz,),
        in_specs=[pl.BlockSpec((1, t, 5 * GROUP_W), lambda b: (b, 0, 0)),
                  pl.BlockSpec((2, GROUP_W), lambda b: (0, 0)),
                  pl.BlockSpec((1, GROUP_W), lambda b: (0, 0))],
        out_specs=pl.BlockSpec((1, t, GROUP_W), lambda b: (b, 0, 0)),
        out_shape=jax.ShapeDtypeStruct((bsz, t, GROUP_W), F32),
        scratch_shapes=[pltpu.VMEM((2, GROUP_W // LANES, t, LANES), F32), pltpu.VMEM((2, GROUP_W // LANES, t, LANES), F32),
                        pltpu.VMEM((GROUP_W // LANES, t, LANES), F32),
                        pltpu.VMEM((2, t, GROUP_W), F32), pltpu.VMEM((2, GROUP_W, GROUP_W), F32)],
        compiler_params=_cparams("arbitrary"),
    )(z, lb.astype(F32), norm_w.astype(F32).reshape(1, GROUP_W))


def _lane_expand(cols, seg):
    lane = lax.broadcasted_iota(jnp.int32, (1, len(cols) * seg), 1)
    out = cols[-1]
    for h in range(len(cols) - 2, -1, -1):
        out = jnp.where(lane < (h + 1) * seg, cols[h], out)
    return out


def _mlstm_body(q_ref, k_ref, v_ref, o_ref, zg_ref, gr_ref, nw_ref, out_ref,
                bce, lice, brow, hsc, cst, nst, *, t_len):
    n_all = t_len // ML_CHUNK
    n_ctx = CTX_LEN // ML_CHUNK
    g = zg_ref[0]
    lf = jax.nn.log_sigmoid(g)
    src = lax.broadcasted_iota(jnp.int32, (LANES, GROUP_W), 0)
    dst_head = lax.broadcasted_iota(jnp.int32, (LANES, GROUP_W), 1) // ML_DK
    bd, _, _ = _head_blocks(GROUP_W, ML_DK)
    ebd = jnp.where(bd, 1.0, 0.0).astype(BF16)
    for d in range(2):
        lice[d] = _dot3(g, jnp.where(src == ML_HEADS * d + dst_head, 1.0, 0.0).astype(BF16))
        lf_exp = _dot3(lf, jnp.where(src == ML_HEADS * (2 + d) + dst_head, 1.0, 0.0).astype(BF16))
        for j, blk in enumerate(_chunk_cumsum_cols(lf_exp, ML_CHUNK, d == 1)):
            bce[d, j * ROW_TILE:(j + 1) * ROW_TILE, :] = blk
        _, ri, ci = _head_blocks(GROUP_W, ML_CHUNK)
        tri = jnp.where(bd & ((ri >= ci) if d else (ri <= ci)), 1.0, 0.0).astype(BF16)
        brow[d] = _dot3(jax.nn.log_sigmoid(gr_ref[0, 2 + d]), tri)
    cst[...] = jnp.zeros(cst.shape, F32)
    nst[...] = jnp.zeros(nst.shape, F32)
    tt = lax.broadcasted_iota(jnp.int32, (ML_CHUNK, GROUP_W), 0)
    ss = lax.broadcasted_iota(jnp.int32, (ML_CHUNK, GROUP_W), 1) % ML_CHUNK
    lane_h = lax.broadcasted_iota(jnp.int32, (1, LANES), 1) < ML_CHUNK

    def chunk(d, c, m_prev):
        rows = pl.ds(pl.multiple_of(c * ML_CHUNK, ML_CHUNK), ML_CHUNK)
        q = q_ref[0, rows, :]
        k = k_ref[0, rows, :] * (ML_DK ** -0.5)
        v = v_ref[0, rows, :]
        bc = bce[d, rows, :]
        lic = lice[d, rows, :]
        br = brow[d, pl.ds(c, 1), :]
        lir = gr_ref[0, d, pl.ds(c, 1), :]
        b_end = bc[0:1] if d else bc[ML_CHUNK - 1:ML_CHUNK]
        valid = (ss >= tt) if d else (ss <= tt)
        d_log = jnp.where(valid, bc - br + lir, -jnp.inf)
        cols = []
        for hp in range(GROUP_W // LANES):
            xs = d_log[:, hp * LANES:(hp + 1) * LANES]
            cols.append(jnp.max(jnp.where(lane_h, xs, -jnp.inf), axis=-1, keepdims=True))
            cols.append(jnp.max(jnp.where(lane_h, -jnp.inf, xs), axis=-1, keepdims=True))
        w_inter = bc + m_prev
        m_t = jnp.maximum(w_inter, _lane_expand(cols, ML_CHUNK))
        e_inter = jnp.exp(w_inter - m_t)
        qb = q.astype(BF16)
        kbd = jnp.where(bd, jnp.concatenate([k] * ML_HEADS, axis=0).T, 0.0).astype(BF16)
        vbd = jnp.where(bd, jnp.concatenate([v] * ML_HEADS, axis=0), 0.0).astype(BF16)
        wts = jnp.exp(d_log - m_t) * jnp.dot(qb, kbd, preferred_element_type=F32)
        w_hi = wts.astype(BF16)
        w_lo = (wts - w_hi.astype(F32)).astype(BF16)
        c_prev, n_prev = cst[d], nst[d]
        num = (jnp.dot(w_hi, vbd, preferred_element_type=F32)
               + e_inter * jnp.dot(qb, c_prev.astype(BF16), preferred_element_type=F32))
        den = (jnp.dot(w_hi, ebd, preferred_element_type=F32) + jnp.dot(w_lo, ebd, preferred_element_type=F32)
               + e_inter * jnp.dot(qb, n_prev.astype(BF16), preferred_element_type=F32))
        hsc[d, rows, :] = num / jnp.maximum(jnp.abs(den), jnp.exp(-m_t))
        w_end = b_end - bc + lic
        m_loc = jnp.max(w_end, axis=0, keepdims=True)
        e_end = jnp.exp(w_end - m_loc)
        m_new = jnp.maximum(b_end + m_prev, m_loc)
        a_old = jnp.exp(b_end + m_prev - m_new)
        a_loc = jnp.exp(m_loc - m_new)
        evbd = jnp.where(bd, jnp.concatenate([e_end * v] * ML_HEADS, axis=0), 0.0).astype(BF16)
        eebd = jnp.where(bd, jnp.concatenate([e_end] * ML_HEADS, axis=0), 0.0).astype(BF16)
        cst[d] = a_old * c_prev + a_loc * jnp.dot(kbd, evbd, preferred_element_type=F32)
        nst[d] = a_old * n_prev + a_loc * jnp.dot(kbd, eebd, preferred_element_type=F32)
        return m_new

    def step(i, carry):
        return chunk(0, i, carry[0]), chunk(1, _bwd_chunk(i, n_ctx, n_all), carry[1])

    zero = jnp.zeros((1, GROUP_W), F32)
    lax.fori_loop(0, n_all, step, (zero, zero), unroll=4)
    out_ref[0] = _head_rms(hsc[0] + hsc[1], ebd, ML_DV) * nw_ref[...] * jax.nn.sigmoid(o_ref[0])


def _mlstm_mixer(z, gates_t, norm_w):
    bsz, t, _ = z.shape
    nc = t // ML_CHUNK
    ncp = -(-nc // 8) * 8
    gr = gates_t.reshape(bsz, 4, ML_HEADS, nc, ML_CHUNK).transpose(0, 1, 3, 2, 4).reshape(bsz, 4, nc, GROUP_W)
    gr = jnp.pad(gr, ((0, 0), (0, 0), (0, ncp - nc), (0, 0)))
    col = ML_OFF // GROUP_W
    zspec = lambda j: pl.BlockSpec((1, t, GROUP_W), lambda b: (b, 0, col + j))
    return pl.pallas_call(
        functools.partial(_mlstm_body, t_len=t),
        grid=(bsz,),
        in_specs=[zspec(0), zspec(1), zspec(2), zspec(3),
                  pl.BlockSpec((1, t, LANES), lambda b: (b, 0, ML_GATE_OFF // LANES)),
                  pl.BlockSpec((1, 4, ncp, GROUP_W), lambda b: (b, 0, 0, 0)),
                  pl.BlockSpec((1, GROUP_W), lambda b: (0, 0))],
        out_specs=pl.BlockSpec((1, t, GROUP_W), lambda b: (b, 0, 0)),
        out_shape=jax.ShapeDtypeStruct((bsz, t, GROUP_W), F32),
        scratch_shapes=[pltpu.VMEM((2, t, GROUP_W), F32), pltpu.VMEM((2, t, GROUP_W), F32),
                        pltpu.VMEM((2, ncp, GROUP_W), F32), pltpu.VMEM((2, t, GROUP_W), F32),
                        pltpu.VMEM((2, GROUP_W, GROUP_W), F32), pltpu.VMEM((2, GROUP_W, GROUP_W), F32)],
        compiler_params=_cparams("arbitrary"),
    )(z, z, z, z, z, gr, norm_w.astype(F32).reshape(1, GROUP_W))


def _final_norm_body(x_ref, w_ref, o_ref):
    o_ref[0] = _rms(x_ref[0]) * w_ref[...]


def _final_norm(h, w):
    bsz, t, d = h.shape
    n = t - CTX_LEN
    blk0 = CTX_LEN // ROW_TILE
    return pl.pallas_call(
        _final_norm_body,
        grid=(bsz, n // ROW_TILE),
        in_specs=[pl.BlockSpec((1, ROW_TILE, d), lambda b, j: (b, blk0 + j, 0)),
                  pl.BlockSpec((1, d), lambda b, j: (0, 0))],
        out_specs=pl.BlockSpec((1, ROW_TILE, d), lambda b, j: (b, j, 0)),
        out_shape=jax.ShapeDtypeStruct((bsz, n, d), F32),
        compiler_params=_cparams("arbitrary", "arbitrary"),
    )(h, w.reshape(1, d))


def kernel(x, c, ctx, c_ctx, mod_w, mod_b, norm1_w, norm2_w, w_in, b_in, hg_lb_logits, hg_norm_w,
           s5_lam_re, s5_lam_im, s5_log_dt, s5_b_re, s5_b_im, s5_c_re, s5_c_im, s5_d, s5_glu_w, s5_glu_b,
           da_lq1, da_lk1, da_lq2, da_lk2, da_norm_w, ml_norm_w, w_out, router_w,
           exp_w1, exp_w3, exp_w2, final_norm_w):
    bsz, seq, d = x.shape
    assert ctx.shape[1] == CTX_LEN == ROW_TILE and seq % ROW_TILE == 0 and d == D_MODEL
    t = CTX_LEN + seq
    lb_all = jnp.cumsum(jax.nn.softmax(hg_lb_logits.astype(F32), axis=0), axis=0)
    lb_all = lb_all - lb_all[0]

    n_rows = -(-(bsz + 1) // 8) * 8
    cv = jnp.zeros((n_rows, d), F32).at[:bsz].set(c).at[bsz].set(c_ctx)
    mod_all = _modulation(cv, mod_w, mod_b)
    cos_t, sin_t = _rope_tables(t)
    h = jnp.concatenate([ctx, x], axis=1)
    w1, w3, w2 = exp_w1.astype(BF16), exp_w3.astype(BF16), exp_w2.astype(BF16)

    for li in range(DEPTH):
        ctx_out = li < DEPTH - 1
        m6 = mod_all[li].reshape(n_rows, 6, d)
        modsel = jnp.stack([jnp.broadcast_to(m6[bsz], (bsz, 6, d)), m6[:bsz]], axis=1)
        w_pad = jnp.pad(w_in[li], ((0, 0), (0, IN_COLS_PAD - IN_COLS))).astype(BF16)
        b_pad = jnp.pad(b_in[li], (0, IN_COLS_PAD - IN_COLS)).reshape(1, IN_COLS_PAD)
        wg_t = w_in[li][:, ML_GATE_OFF:].T.astype(BF16)
        bg_col = b_in[li][ML_GATE_OFF:].reshape(N_GATES, 1)
        z, gates_t = _in_projection(h, modsel, norm1_w[li], w_pad, b_pad, wg_t, bg_col)

        mix_a = _hgrn2_mixer(z, lb_all[li], hg_norm_w[li])
        mix_b = _s5_mixer(z, s5_lam_re[li], s5_lam_im[li], s5_log_dt[li], s5_b_re[li], s5_b_im[li],
                          s5_c_re[li], s5_c_im[li], s5_d[li], s5_glu_w[li], s5_glu_b[li])
        lam_init = 0.8 - 0.6 * math.exp(-0.3 * li)
        att_args = (da_lq1[li], da_lk1[li], da_lq2[li], da_lk2[li], da_norm_w[li], lam_init)
        c_l = _diff_attention(z, CTX_LEN, seq, t, True, cos_t, sin_t, *att_args)
        if ctx_out:
            c_c = _diff_attention(z, 0, CTX_LEN, CTX_LEN, False, cos_t, sin_t, *att_args)
        else:
            c_c = jnp.zeros((bsz, CTX_LEN, GROUP_W), F32)
        mix_c = jnp.concatenate([c_c, c_l], axis=1)
        mix_d = _mlstm_mixer(z, gates_t, ml_norm_w[li])

        hm, xm, aff_t = _out_projection((mix_a, mix_b, mix_c, mix_d), h, modsel, norm2_w[li],
                                        w_out[li].astype(BF16), router_w[li].T.astype(BF16))
        slot, slot_t = _route(aff_t, CTX_LEN, seq)
        y = _moe_ffn(xm, slot, aff_t, CTX_LEN, seq, 1, w1, w3, w2, li)
        if not ctx_out:
            return _moe_combine(y, slot_t, hm, modsel, CTX_LEN, seq, 1, final_w=final_norm_w)
        h = _moe_combine(y, slot_t, hm, modsel, CTX_LEN, seq, 1)
        slot, slot_t = _route(aff_t, 0, CTX_LEN)
        y = _moe_ffn(xm, slot, aff_t, 0, CTX_LEN, bsz, w1, w3, w2, li)
        h = _moe_combine(y, slot_t, h, modsel, 0, CTX_LEN, 0)
```

```python
import functools
import math
import jax
import jax.numpy as jnp
from jax import lax
from jax.experimental import pallas as pl
from jax.experimental.pallas import tpu as pltpu

D_MODEL = 1024
DEPTH = 2
CTX_LEN = 256
GRID_W = 64
N_MIXERS = 4
GROUP_W = D_MODEL // N_MIXERS
MIX_W = N_MIXERS * GROUP_W
EPS = 1e-6
F32 = jnp.float32
BF16 = jnp.bfloat16
HG_HEADS = 4
HG_DK = GROUP_W // HG_HEADS
HG_DV = GROUP_W // HG_HEADS
HG_CHUNK = 16
S5_CH = 16
S5_GROUPS = GROUP_W // S5_CH
S5_STATE = 64
DA_HEADS = 4
DA_DQK = GROUP_W // (2 * DA_HEADS)
DA_DV = GROUP_W // DA_HEADS
ROPE_BASE = 10000.0
ML_HEADS = 4
ML_DK = GROUP_W // ML_HEADS
ML_DV = GROUP_W // ML_HEADS
ML_CHUNK = 64
N_EXPERTS = 16
EC_CAPACITY = 2
D_EXPERT = 2 * D_MODEL
HG_OFF = 0
S5_OFF = HG_OFF + 5 * GROUP_W
DA_OFF = S5_OFF + GROUP_W
ML_OFF = DA_OFF + 3 * GROUP_W
ML_GATE_OFF = ML_OFF + 4 * GROUP_W
IN_COLS = ML_GATE_OFF + 4 * ML_HEADS

LANES = 128
ROW_TILE = 256
IN_COLS_PAD = ML_GATE_OFF + LANES
N_GATES = 4 * ML_HEADS
S5_LANES = S5_GROUPS * S5_STATE
S5_HALF = S5_LANES // 2
S5_SEGS = 8
VMEM_LIMIT = 56 * 1024 * 1024
LOG2_E = math.log2(math.e)


def _cparams(*sem):
    return pltpu.CompilerParams(dimension_semantics=sem, vmem_limit_bytes=VMEM_LIMIT)


def _rms(x):
    return x * lax.rsqrt(jnp.mean(x * x, axis=-1, keepdims=True) + EPS)


def _mod_body(cv_ref, w_ref, b_ref, o_ref):
    cv = cv_ref[...]
    s = cv * jax.nn.sigmoid(cv)
    o_ref[0] = jnp.dot(s.astype(BF16), w_ref[0].astype(BF16), preferred_element_type=F32) + b_ref[0]


def _modulation(cv, mod_w, mod_b):
    n_l, d, n6 = mod_w.shape
    r = cv.shape[0]
    tn = n6 // 4
    return pl.pallas_call(
        _mod_body,
        grid=(n_l, 4),
        in_specs=[pl.BlockSpec((r, d), lambda l, j: (0, 0)),
                  pl.BlockSpec((1, d, tn), lambda l, j: (l, 0, j)),
                  pl.BlockSpec((1, 1, tn), lambda l, j: (l, 0, j))],
        out_specs=pl.BlockSpec((1, r, tn), lambda l, j: (l, 0, j)),
        out_shape=jax.ShapeDtypeStruct((n_l, r, n6), F32),
        compiler_params=_cparams("arbitrary", "arbitrary"),
    )(cv, mod_w, mod_b.reshape(n_l, 1, n6))


def _inproj_body(h_ref, m_ref, nw_ref, w_ref, b_ref, wg_ref, bg_ref, z_ref, gt_ref):
    m = m_ref[0, 0]
    xn = _rms(h_ref[0]) * nw_ref[...] * (1.0 + m[1:2]) + m[0:1]
    xb = xn.astype(BF16)
    z_ref[0] = jnp.dot(xb, w_ref[...], preferred_element_type=F32) + b_ref[...]
    gt_ref[0] = lax.dot_general(wg_ref[...], xb, (((1,), (1,)), ((), ())),
                                preferred_element_type=F32) + bg_ref[...]


def _in_projection(h, modsel, norm_w, w_pad, b_pad, wg_t, bg_col):
    bsz, t, d = h.shape
    nt = t // ROW_TILE
    return pl.pallas_call(
        _inproj_body,
        grid=(bsz, nt),
        in_specs=[pl.BlockSpec((1, ROW_TILE, d), lambda b, j: (b, j, 0)),
                  pl.BlockSpec((1, 1, 6, d), lambda b, j: (b, jnp.minimum(j, 1), 0, 0)),
                  pl.BlockSpec((1, d), lambda b, j: (0, 0)),
                  pl.BlockSpec((d, IN_COLS_PAD), lambda b, j: (0, 0)),
                  pl.BlockSpec((1, IN_COLS_PAD), lambda b, j: (0, 0)),
                  pl.BlockSpec((N_GATES, d), lambda b, j: (0, 0)),
                  pl.BlockSpec((N_GATES, 1), lambda b, j: (0, 0))],
        out_specs=[pl.BlockSpec((1, ROW_TILE, IN_COLS_PAD), lambda b, j: (b, j, 0)),
                   pl.BlockSpec((1, N_GATES, ROW_TILE), lambda b, j: (b, 0, j))],
        out_shape=[jax.ShapeDtypeStruct((bsz, t, IN_COLS_PAD), F32),
                   jax.ShapeDtypeStruct((bsz, N_GATES, t), F32)],
        compiler_params=_cparams("arbitrary", "arbitrary"),
    )(h, modsel, norm_w.reshape(1, d), w_pad, b_pad, wg_t, bg_col)


def _outproj_body(a_ref, b_ref, c_ref, d_ref, h_ref, m_ref, nw_ref, wo_ref, rw_ref, hm_ref, xm_ref, aff_ref):
    m = m_ref[0, 0]
    y = jnp.concatenate([a_ref[0], b_ref[0], c_ref[0], d_ref[0]], axis=-1).astype(BF16)
    hm = h_ref[0] + m[2:3] * jnp.dot(y, wo_ref[...], preferred_element_type=F32)
    hm_ref[0] = hm
    xm = (_rms(hm) * nw_ref[...] * (1.0 + m[4:5]) + m[3:4]).astype(BF16)
    xm_ref[0] = xm
    logit = lax.dot_general(rw_ref[...], xm, (((1,), (1,)), ((), ())), preferred_element_type=F32)
    e = jnp.exp(logit - jnp.max(logit, axis=0, keepdims=True))
    aff_ref[0] = e / jnp.sum(e, axis=0, keepdims=True)


def _out_projection(mix, h, modsel, norm_w, wo_bf, rw_t):
    bsz, t, d = h.shape
    nt = t // ROW_TILE
    mix_spec = pl.BlockSpec((1, ROW_TILE, GROUP_W), lambda b, j: (b, j, 0))
    row_spec = pl.BlockSpec((1, ROW_TILE, d), lambda b, j: (b, j, 0))
    return pl.pallas_call(
        _outproj_body,
        grid=(bsz, nt),
        in_specs=[mix_spec, mix_spec, mix_spec, mix_spec, row_spec,
                  pl.BlockSpec((1, 1, 6, d), lambda b, j: (b, jnp.minimum(j, 1), 0, 0)),
                  pl.BlockSpec((1, d), lambda b, j: (0, 0)),
                  pl.BlockSpec((MIX_W, d), lambda b, j: (0, 0)),
                  pl.BlockSpec((N_EXPERTS, d), lambda b, j: (0, 0))],
        out_specs=[row_spec, row_spec, pl.BlockSpec((1, N_EXPERTS, ROW_TILE), lambda b, j: (b, 0, j))],
        out_shape=[jax.ShapeDtypeStruct((bsz, t, d), F32), jax.ShapeDtypeStruct((bsz, t, d), BF16),
                   jax.ShapeDtypeStruct((bsz, N_EXPERTS, t), F32)],
        compiler_params=_cparams("arbitrary", "arbitrary"),
    )(*mix, h, modsel, norm_w.reshape(1, d), wo_bf, rw_t)


def _topk_body(aff_ref, slot_ref, slot_t_ref, *, cap, n, row0):
    aff = aff_ref[0, :, row0:row0 + n]
    bits = pltpu.bitcast(aff, jnp.int32)
    thr = jnp.zeros((N_EXPERTS, 1), jnp.int32)
    for bit in range(30, -1, -1):
        cand = thr | (1 << bit)
        cnt = jnp.sum(jnp.where(bits >= cand, 1.0, 0.0), axis=-1, keepdims=True)
        thr = jnp.where(cnt >= cap, cand, thr)
    room = cap - jnp.sum(jnp.where(bits > thr, 1.0, 0.0), axis=-1, keepdims=True)
    ri = lax.broadcasted_iota(jnp.int32, (LANES, LANES), 0)
    ci = lax.broadcasted_iota(jnp.int32, (LANES, LANES), 1)
    incl = jnp.where(ri <= ci, 1.0, 0.0).astype(BF16)
    off_eq = jnp.zeros((N_EXPERTS, 1), F32)
    off_sel = jnp.zeros((N_EXPERTS, 1), F32)
    pieces = []
    for j in range(n // LANES):
        sl = slice(j * LANES, (j + 1) * LANES)
        bits_b = bits[:, sl]
        eq_b = jnp.where(bits_b == thr, 1.0, 0.0)
        pos_eq = jnp.dot(eq_b.astype(BF16), incl, preferred_element_type=F32) - eq_b + off_eq
        sel = jnp.where(bits_b > thr, 1.0, jnp.where(pos_eq < room, eq_b, 0.0))
        pos_sel = jnp.dot(sel.astype(BF16), incl, preferred_element_type=F32) - sel + off_sel
        piece = jnp.where(sel > 0.5, pos_sel, -1.0)
        slot_ref[0, :, sl] = piece.astype(jnp.int32)
        pieces.append(piece)
        off_eq = off_eq + jnp.sum(eq_b, axis=-1, keepdims=True)
        off_sel = off_sel + jnp.sum(sel, axis=-1, keepdims=True)
    pad = jnp.full((LANES - N_EXPERTS, LANES), -1.0, F32)
    for j in range(n // LANES):
        tile = jnp.concatenate([pieces[j], pad], axis=0)
        slot_t_ref[0, j * LANES:(j + 1) * LANES, :] = tile.T.astype(jnp.int32)


def _route(aff_t, row0, n):
    bsz, _, t = aff_t.shape
    cap = EC_CAPACITY * n // N_EXPERTS
    return pl.pallas_call(
        functools.partial(_topk_body, cap=cap, n=n, row0=row0),
        grid=(bsz,),
        in_specs=[pl.BlockSpec((1, N_EXPERTS, t), lambda b: (b, 0, 0))],
        out_specs=[pl.BlockSpec((1, N_EXPERTS, n), lambda b: (b, 0, 0)),
                   pl.BlockSpec((1, n, LANES), lambda b: (b, 0, 0))],
        out_shape=[jax.ShapeDtypeStruct((bsz, N_EXPERTS, n), jnp.int32),
                   jax.ShapeDtypeStruct((bsz, n, LANES), jnp.int32)],
        compiler_params=_cparams("arbitrary"),
    )(aff_t)


def _moe_ffn_body(x_ref, slot_ref, aff_ref, w1_ref, w3_ref, w2_ref, y_ref, xe_scr, g_scr, *, nb, cap, n, off):
    for i in range(nb):
        hit = lax.broadcasted_iota(jnp.int32, (cap, n), 0) == slot_ref[i, 0]
        onehot = jnp.where(hit, 1.0, 0.0).astype(BF16)
        xe_scr[i * cap:(i + 1) * cap, :] = jnp.dot(onehot, x_ref[i, off:off + n, :],
                                                   preferred_element_type=F32).astype(BF16)
        g_scr[i * cap:(i + 1) * cap, :] = jnp.sum(jnp.where(hit, aff_ref[i, 0, :, off:off + n], 0.0),
                                                  axis=-1, keepdims=True)
    xe = xe_scr[...]
    h1 = jnp.dot(xe, w1_ref[0, 0], preferred_element_type=F32)
    h3 = jnp.dot(xe, w3_ref[0, 0], preferred_element_type=F32)
    act = (h1 * jax.nn.sigmoid(h1) * h3).astype(BF16)
    y = (jnp.dot(act, w2_ref[0, 0], preferred_element_type=F32) * g_scr[...]).astype(BF16)
    for i in range(nb):
        y_ref[i, 0] = y[i * cap:(i + 1) * cap]


def _moe_ffn(xm, slot, aff_t, row0, n, nb, w1, w3, w2, li):
    bsz, t, d = xm.shape
    cap = EC_CAPACITY * n // N_EXPERTS
    rows, off = (n, 0) if row0 % n == 0 else (t, row0)
    blk = row0 // n if off == 0 else 0
    slot4 = slot.reshape(bsz, N_EXPERTS, 1, n)
    aff4 = aff_t.reshape(bsz, N_EXPERTS, 1, t)
    return pl.pallas_call(
        functools.partial(_moe_ffn_body, nb=nb, cap=cap, n=n, off=off),
        grid=(N_EXPERTS, bsz // nb),
        in_specs=[pl.BlockSpec((nb, rows, d), lambda e, b: (b, blk, 0)),
                  pl.BlockSpec((nb, 1, 1, n), lambda e, b: (b, e, 0, 0)),
                  pl.BlockSpec((nb, 1, 1, rows), lambda e, b: (b, e, 0, blk)),
                  pl.BlockSpec((1, 1, d, D_EXPERT), lambda e, b: (li, e, 0, 0)),
                  pl.BlockSpec((1, 1, d, D_EXPERT), lambda e, b: (li, e, 0, 0)),
                  pl.BlockSpec((1, 1, D_EXPERT, d), lambda e, b: (li, e, 0, 0))],
        out_specs=pl.BlockSpec((nb, 1, cap, d), lambda e, b: (b, e, 0, 0)),
        out_shape=jax.ShapeDtypeStruct((bsz, N_EXPERTS, cap, d), BF16),
        scratch_shapes=[pltpu.VMEM((nb * cap, d), BF16), pltpu.VMEM((nb * cap, 1), F32)],
        compiler_params=_cparams("arbitrary", "arbitrary"),
    )(xm, slot4, aff4, w1, w3, w2)


def _moe_combine_body(y_ref, st_ref, hm_ref, m_ref, fw_ref, o_ref, *, cap, final):
    st = st_ref[0]
    lane = lax.broadcasted_iota(jnp.int32, (st.shape[0], cap), 1)
    hots = [jnp.where(lane == st[:, e:e + 1], 1.0, 0.0).astype(BF16) for e in range(N_EXPERTS)]
    if cap % LANES == 0:
        acc = jnp.dot(jnp.concatenate(hots, axis=1), y_ref[0].reshape(N_EXPERTS * cap, y_ref.shape[-1]),
                      preferred_element_type=F32)
    else:
        acc = sum(jnp.dot(hots[e], y_ref[0, e], preferred_element_type=F32) for e in range(N_EXPERTS))
    h = hm_ref[0] + m_ref[0, 0][5:6] * acc
    o_ref[0] = _rms(h) * fw_ref[...] if final else h


def _moe_combine(y, slot_t, hm, modsel, row0, n, is_latent, final_w=None):
    bsz, _, cap, d = y.shape
    nt = n // ROW_TILE
    blk0 = row0 // ROW_TILE
    final = final_w is not None
    fw = (final_w if final else jnp.ones((d,), F32)).reshape(1, d)
    return pl.pallas_call(
        functools.partial(_moe_combine_body, cap=cap, final=final),
        grid=(bsz, nt),
        in_specs=[pl.BlockSpec((1, N_EXPERTS, cap, d), lambda b, j: (b, 0, 0, 0)),
                  pl.BlockSpec((1, ROW_TILE, LANES), lambda b, j: (b, j, 0)),
                  pl.BlockSpec((1, ROW_TILE, d), lambda b, j: (b, blk0 + j, 0)),
                  pl.BlockSpec((1, 1, 6, d), lambda b, j: (b, is_latent, 0, 0)),
                  pl.BlockSpec((1, d), lambda b, j: (0, 0))],
        out_specs=pl.BlockSpec((1, ROW_TILE, d), lambda b, j: (b, j if final else blk0 + j, 0)),
        out_shape=jax.ShapeDtypeStruct((bsz, n, d) if final else hm.shape, F32),
        input_output_aliases={} if final else {2: 0},
        compiler_params=_cparams("arbitrary", "arbitrary"),
    )(y, slot_t, hm, modsel, fw)


def _swap_pairs(x):
    w = x.shape[-1]
    lane = lax.broadcasted_iota(jnp.int32, x.shape, x.ndim - 1)
    return jnp.where(lane % 2 == 0, pltpu.roll(x, w - 1, x.ndim - 1), pltpu.roll(x, 1, x.ndim - 1))


def _attn_body(q_ref, k_ref, v_ref, cq_ref, sq_ref, ck_ref, sk_ref, lq1_ref, lk1_ref, lq2_ref, lk2_ref, nw_ref,
               o_ref, k_scr, vt_scr, *, rope, lam_init):
    @pl.when(pl.program_id(1) == 0)
    def _():
        k = k_ref[0]
        if rope:
            k = k * ck_ref[...] + _swap_pairs(k) * sk_ref[...]
        k_scr[...] = k.astype(BF16)
        for r in range(0, k_scr.shape[0], ROW_TILE):
            vt_scr[:, r:r + ROW_TILE] = v_ref[0, r:r + ROW_TILE, :].T.astype(BF16)

    q = q_ref[0]
    if rope:
        q = q * cq_ref[...] + _swap_pairs(q) * sq_ref[...]
    q_t = (q * (DA_DQK ** -0.5 * LOG2_E)).T
    lam = (jnp.exp(jnp.sum(lq1_ref[...] * lk1_ref[...], axis=-1, keepdims=True))
           - jnp.exp(jnp.sum(lq2_ref[...] * lk2_ref[...], axis=-1, keepdims=True)) + lam_init)
    row = lax.broadcasted_iota(jnp.int32, (GROUP_W, 1), 0)
    kk = k_scr[...]
    outs = []
    for h in range(DA_HEADS):
        qm = jnp.concatenate([jnp.where(row // DA_DQK == 2 * h + mi, q_t, 0.0) for mi in range(2)], axis=1)
        s = jnp.dot(kk, qm.astype(BF16), preferred_element_type=F32)
        e = jnp.exp2(s - jnp.max(s, axis=0, keepdims=True))
        pv = (jnp.dot(vt_scr[h * DA_DV:(h + 1) * DA_DV, :], e.astype(BF16), preferred_element_type=F32)
              / jnp.sum(e, axis=0, keepdims=True))
        nq = q_t.shape[1]
        oh = pv[:, :nq] - lam * pv[:, nq:]
        outs.append(oh * lax.rsqrt(jnp.mean(oh * oh, axis=0, keepdims=True) + EPS))
    o_ref[0] = jnp.concatenate(outs, axis=0).T * nw_ref[...] * (1.0 - lam_init)


def _diff_attention(z, q_row0, nq, nk, rope, cos_t, sin_t, lq1, lk1, lq2, lk2, norm_w, lam_init):
    bsz = z.shape[0]
    qb = ROW_TILE
    q0 = q_row0 // qb
    col = DA_OFF // GROUP_W
    vec = lambda a: a.reshape(1, -1)
    small = pl.BlockSpec((1, DA_DQK), lambda b, j: (0, 0))
    return pl.pallas_call(
        functools.partial(_attn_body, rope=rope, lam_init=lam_init),
        grid=(bsz, nq // qb),
        in_specs=[pl.BlockSpec((1, qb, GROUP_W), lambda b, j: (b, q0 + j, col)),
                  pl.BlockSpec((1, nk, GROUP_W), lambda b, j: (b, 0, col + 1)),
                  pl.BlockSpec((1, nk, GROUP_W), lambda b, j: (b, 0, col + 2)),
                  pl.BlockSpec((qb, GROUP_W), lambda b, j: (q0 + j, 0)),
                  pl.BlockSpec((qb, GROUP_W), lambda b, j: (q0 + j, 0)),
                  pl.BlockSpec((nk, GROUP_W), lambda b, j: (0, 0)),
                  pl.BlockSpec((nk, GROUP_W), lambda b, j: (0, 0)),
                  small, small, small, small,
                  pl.BlockSpec((1, GROUP_W), lambda b, j: (0, 0))],
        out_specs=pl.BlockSpec((1, qb, GROUP_W), lambda b, j: (b, j, 0)),
        out_shape=jax.ShapeDtypeStruct((bsz, nq, GROUP_W), F32),
        scratch_shapes=[pltpu.VMEM((nk, GROUP_W), BF16), pltpu.VMEM((GROUP_W, nk), BF16)],
        compiler_params=_cparams("arbitrary", "arbitrary"),
    )(z, z, z, cos_t, sin_t, cos_t, sin_t, vec(lq1), vec(lk1), vec(lq2), vec(lk2), vec(norm_w))


def _rope_tables(t):
    n = t - CTX_LEN
    axis_dim = DA_DQK // 2
    inv = ROPE_BASE ** (-jnp.arange(0, axis_dim, 2, dtype=F32) / axis_dim)
    tok = jnp.arange(n, dtype=jnp.int32)
    row = (tok // GRID_W).astype(F32)
    colp = (tok % GRID_W).astype(F32)
    ang = jnp.concatenate([row[:, None] * inv, colp[:, None] * inv], axis=-1)
    cos = jnp.repeat(jnp.cos(ang), 2, axis=-1)
    sin = jnp.repeat(jnp.sin(ang), 2, axis=-1) * jnp.tile(jnp.array([-1.0, 1.0], F32), DA_DQK // 2)
    reps = GROUP_W // DA_DQK
    cos = jnp.concatenate([jnp.ones((CTX_LEN, GROUP_W), F32), jnp.tile(cos, (1, reps))], axis=0)
    sin = jnp.concatenate([jnp.zeros((CTX_LEN, GROUP_W), F32), jnp.tile(sin, (1, reps))], axis=0)
    return cos, sin


def _cmul(ar, ai, br, bi):
    return ar * br - ai * bi, ar * bi + ai * br


def _s5_scan_part(bufs, coef, base, seg_len, carry):
    xrf, xif, xrb, xib = bufs
    af_r, af_i, ab_r, ab_i = coef
    assert seg_len & (seg_len - 1) == 0

    def sweep(store, init):
        def step(i, st):
            fr, fi, br, bi = st
            rf = pl.ds(pl.multiple_of(base + S5_SEGS * i, S5_SEGS), S5_SEGS)
            rb = pl.ds(pl.multiple_of(base + S5_SEGS * (seg_len - 1 - i), S5_SEGS), S5_SEGS)
            nfr = af_r * fr - af_i * fi + xrf[rf, :]
            nfi = af_r * fi + af_i * fr + xif[rf, :]
            nbr = ab_r * br - ab_i * bi + xrb[rb, :]
            nbi = ab_r * bi + ab_i * br + xib[rb, :]
            if store:
                xrf[rf, :] = nfr
                xif[rf, :] = nfi
                xrb[rb, :] = nbr
                xib[rb, :] = nbi
            return nfr, nfi, nbr, nbi
        return lax.fori_loop(0, seg_len, step, init, unroll=4)

    zero = jnp.zeros(af_r.shape, F32)
    ef_r, ef_i, eb_r, eb_i = sweep(False, (zero,) * 4)
    pf = (af_r[0:1], af_i[0:1])
    pb = (ab_r[0:1], ab_i[0:1])
    for _ in range(seg_len.bit_length() - 1):
        pf = _cmul(*pf, *pf)
        pb = _cmul(*pb, *pb)
    cf_r, cf_i, cb_r, cb_i = carry
    ins_f, ins_b = [], []
    for j in range(S5_SEGS):
        ins_f.append((cf_r, cf_i))
        gr, gi = _cmul(*pf, cf_r, cf_i)
        cf_r, cf_i = ef_r[j:j + 1] + gr, ef_i[j:j + 1] + gi
    for j in range(S5_SEGS - 1, -1, -1):
        ins_b.append((cb_r, cb_i))
        gr, gi = _cmul(*pb, cb_r, cb_i)
        cb_r, cb_i = eb_r[j:j + 1] + gr, eb_i[j:j + 1] + gi
    ins_b.reverse()
    stack = lambda rows: jnp.concatenate(rows, axis=0)
    sweep(True, (stack([a for a, _ in ins_f]), stack([b for _, b in ins_f]),
                 stack([a for a, _ in ins_b]), stack([b for _, b in ins_b])))
    return cf_r, cf_i, cb_r, cb_i


def _seg_interleave(x):
    b, r, c = x.shape
    return x.reshape(b, S5_SEGS, r // S5_SEGS, c).transpose(0, 2, 1, 3).reshape(b, r, c)


def _seg_deinterleave(x):
    b, r, c = x.shape
    return x.reshape(b, r // S5_SEGS, S5_SEGS, c).transpose(0, 2, 1, 3).reshape(b, r, c)


def _s5_body(u_ref, bre_ref, bim_ref, cre_ref, cim_ref, disc_ref, dsk_ref, gw_ref, gb_ref, o_ref,
             xrf, xif, xrb, xib, y_scr, *, t_len):
    u = u_ref[0]
    ub = u.astype(BF16)
    y_scr[...] = u * dsk_ref[...]
    n_lat = t_len - CTX_LEN
    for half in range(2):
        ls = slice(half * S5_HALF, (half + 1) * S5_HALF)
        bu_r = jnp.dot(ub, bre_ref[:, ls], preferred_element_type=F32)
        bu_i = jnp.dot(ub, bim_ref[:, ls], preferred_element_type=F32)
        disc = disc_ref[:, ls]
        xrf[...] = disc[2:3] * bu_r - disc[3:4] * bu_i
        xif[...] = disc[2:3] * bu_i + disc[3:4] * bu_r
        xrb[...] = disc[6:7] * bu_r - disc[7:8] * bu_i
        xib[...] = disc[6:7] * bu_i + disc[7:8] * bu_r
        coef = tuple(jnp.broadcast_to(disc[r:r + 1], (S5_SEGS, S5_HALF)) for r in (0, 1, 4, 5))
        zero = jnp.zeros((1, S5_HALF), F32)
        carry = _s5_scan_part((xrf, xif, xrb, xib), coef, 0, CTX_LEN // S5_SEGS, (zero,) * 4)
        _s5_scan_part((xrf, xif, xrb, xib), coef, CTX_LEN, n_lat // S5_SEGS, carry)
        cre = cre_ref[ls, :]
        cim = cim_ref[ls, :]
        y_scr[...] += (jnp.dot((xrf[...] + xrb[...]).astype(BF16), cre, preferred_element_type=F32)
                       - jnp.dot((xif[...] + xib[...]).astype(BF16), cim, preferred_element_type=F32))
    y = jax.nn.gelu(y_scr[...])
    gate = jax.nn.sigmoid(jnp.dot(y.astype(BF16), gw_ref[...], preferred_element_type=F32) + gb_ref[...])
    o_ref[0] = y * gate


def _s5_mixer(z, lam_re, lam_im, log_dt, b_re, b_im, c_re, c_im, d_skip, glu_w, glu_b):
    bsz, t, _ = z.shape
    eye = jnp.eye(S5_GROUPS, dtype=F32)
    bbd = lambda w: jnp.einsum('gpc,gh->gchp', w.astype(F32), eye).reshape(GROUP_W, S5_LANES).astype(BF16)
    cbd = lambda w: jnp.einsum('gcp,gh->gphc', w.astype(F32), eye).reshape(S5_LANES, GROUP_W).astype(BF16)
    rows = []
    for d in range(2):
        lr, li = lam_re[d].astype(F32), lam_im[d].astype(F32)
        dt = jnp.exp(log_dt[d].astype(F32))[:, None]
        mag = jnp.exp(lr * dt)
        ab_re, ab_im = mag * jnp.cos(li * dt), mag * jnp.sin(li * dt)
        den = lr * lr + li * li
        co_re = ((ab_re - 1.0) * lr + ab_im * li) / den
        co_im = (ab_im * lr - (ab_re - 1.0) * li) / den
        rows += [ab_re, ab_im, co_re, co_im]
    disc = jnp.stack([r.reshape(S5_LANES) for r in rows], axis=0)
    full = lambda shape: pl.BlockSpec(shape, lambda b: (0,) * len(shape))
    u = z[:, :, S5_OFF:S5_OFF + GROUP_W]
    u = jnp.concatenate([_seg_interleave(u[:, :CTX_LEN]), _seg_interleave(u[:, CTX_LEN:])], axis=1)
    out = pl.pallas_call(
        functools.partial(_s5_body, t_len=t),
        grid=(bsz,),
        in_specs=[pl.BlockSpec((1, t, GROUP_W), lambda b: (b, 0, 0)),
                  full((GROUP_W, S5_LANES)), full((GROUP_W, S5_LANES)),
                  full((S5_LANES, GROUP_W)), full((S5_LANES, GROUP_W)),
                  full((8, S5_LANES)), full((1, GROUP_W)), full((GROUP_W, GROUP_W)), full((1, GROUP_W))],
        out_specs=pl.BlockSpec((1, t, GROUP_W), lambda b: (b, 0, 0)),
        out_shape=jax.ShapeDtypeStruct((bsz, t, GROUP_W), F32),
        scratch_shapes=[pltpu.VMEM((t, S5_HALF), F32)] * 4 + [pltpu.VMEM((t, GROUP_W), F32)],
        compiler_params=_cparams("arbitrary"),
    )(u, bbd(b_re), bbd(b_im), cbd(c_re), cbd(c_im), disc, d_skip.astype(F32).reshape(1, GROUP_W),
      glu_w.astype(BF16), glu_b.astype(F32).reshape(1, GROUP_W))
    return jnp.concatenate([_seg_deinterleave(out[:, :CTX_LEN]), _seg_deinterleave(out[:, CTX_LEN:])], axis=1)


def _split3(x):
    hi = x.astype(BF16)
    r = x - hi.astype(F32)
    mid = r.astype(BF16)
    return hi, mid, (r - mid.astype(F32)).astype(BF16)


def _dot3(x, sel):
    return sum(jnp.dot(p, sel, preferred_element_type=F32) for p in _split3(x))


def _mdot3(sel, x):
    return sum(jnp.dot(sel, p, preferred_element_type=F32) for p in _split3(x))


def _dot2(x, sel):
    hi = x.astype(BF16)
    lo = (x - hi.astype(F32)).astype(BF16)
    return jnp.dot(hi, sel, preferred_element_type=F32) + jnp.dot(lo, sel, preferred_element_type=F32)


def _head_blocks(n, seg):
    ri = lax.broadcasted_iota(jnp.int32, (n, n), 0)
    ci = lax.broadcasted_iota(jnp.int32, (n, n), 1)
    return ri // seg == ci // seg, ri, ci


def _chunk_cumsum_cols(x, chunk, reverse):
    same, ri, ci = _head_blocks(ROW_TILE, chunk)
    tri = jnp.where(same & ((ci >= ri) if reverse else (ci <= ri)), 1.0, 0.0).astype(BF16)
    return [_mdot3(tri, x[r:r + ROW_TILE]) for r in range(0, x.shape[0], ROW_TILE)]


def _repeat_row(row, n):
    return pl.ds(row, n, stride=0)


def _bwd_chunk(i, n_ctx, n_all):
    return jnp.where(i < n_ctx, n_ctx - 1 - i, n_all + n_ctx - 1 - i)


def _head_rms(x, ebd, seg):
    return x * lax.rsqrt(_dot2(x * x, ebd) * (1.0 / seg) + EPS)


def _hgrn2_body(z_ref, lb_ref, nw_ref, out_ref, bsc, ksc, vsc, osc, st, *, t_len):
    n_half = GROUP_W // LANES
    halves = lambda ref, idx, rows: jnp.concatenate([ref[idx + (h, rows)] for h in range(n_half)], axis=1)
    for h in range(n_half):
        vsc[h] = z_ref[0, :, GROUP_W + h * LANES:GROUP_W + (h + 1) * LANES]
    for d in range(2):
        lb = lb_ref[d:d + 1, :]
        f = lb + (1.0 - lb) * jax.nn.sigmoid(z_ref[0, :, (2 + d) * GROUP_W:(3 + d) * GROUP_W])
        k_all = 1.0 - f
        for h in range(n_half):
            ksc[d, h] = k_all[:, h * LANES:(h + 1) * LANES]
        for j, blk in enumerate(_chunk_cumsum_cols(jnp.log(f), HG_CHUNK, d == 1)):
            for h in range(n_half):
                bsc[d, h, j * ROW_TILE:(j + 1) * ROW_TILE, :] = blk[:, h * LANES:(h + 1) * LANES] * LOG2_E
    st[...] = jnp.zeros(st.shape, F32)
    bd, _, _ = _head_blocks(GROUP_W, HG_DK)
    ebd = jnp.where(bd, 1.0, 0.0).astype(BF16)
    tt = lax.broadcasted_iota(jnp.int32, (HG_CHUNK, GROUP_W), 0)
    n_all = t_len // HG_CHUNK
    n_ctx = CTX_LEN // HG_CHUNK

    def chunk(d, c):
        r0 = pl.multiple_of(c * HG_CHUNK, HG_CHUNK)
        rows = pl.ds(r0, HG_CHUNK)
        q = z_ref[0, rows, 0:GROUP_W] * (HG_DK ** -0.5)
        v = halves(vsc, (), rows)
        k = halves(ksc, (d,), rows)
        b = halves(bsc, (d,), rows)
        b_end = b[0:1] if d else b[HG_CHUNK - 1:HG_CHUNK]
        slabs = []
        for s in range(HG_CHUNK):
            row_s = _repeat_row(r0 + s, HG_CHUNK)
            valid = (tt <= s) if d else (tt >= s)
            slabs.append(jnp.where(valid, jnp.exp2(b - halves(bsc, (d,), row_s)), 0.0) * (q * halves(ksc, (d,), row_s)))
        a = jnp.concatenate(slabs, axis=0).astype(BF16)
        sc = jnp.dot(a, ebd, preferred_element_type=F32)
        o_intra = jnp.zeros((HG_CHUNK, GROUP_W), F32)
        for s in range(HG_CHUNK):
            o_intra = o_intra + (sc[s * HG_CHUNK:(s + 1) * HG_CHUNK]
                                 * halves(vsc, (), _repeat_row(r0 + s, HG_CHUNK)))
        s_prev = st[d]
        o_inter = lax.dot_general((q * jnp.exp2(b)).astype(BF16), s_prev.astype(BF16), (((1,), (1,)), ((), ())),
                                  preferred_element_type=F32)
        osc[d, rows, :] = o_intra + o_inter
        kd = (k * jnp.exp2(b_end - b)).astype(BF16)
        upd = lax.dot_general(v.astype(BF16), kd, (((0,), (0,)), ((), ())), preferred_element_type=F32)
        st[d] = s_prev * jnp.exp2(b_end) + jnp.where(bd, upd, 0.0)

    def step(i, carry):
        chunk(0, i)
        chunk(1, _bwd_chunk(i, n_ctx, n_all))
        return carry

    lax.fori_loop(0, n_all, step, 0, unroll=4)
    o = osc[0] + osc[1]
    g = z_ref[0, :, 4 * GROUP_W:5 * GROUP_W]
    out_ref[0] = _head_rms(o, ebd, HG_DV) * nw_ref[...] * (g * jax.nn.sigmoid(g))


def _hgrn2_mixer(z, lb, norm_w):
    bsz, t, _ = z.shape
    return pl.pallas_call(
        functools.partial(_hgrn2_body, t_len=t),
        grid=(bsz,),
        in_specs=[pl.BlockSpec((1, t, 5 * GROUP_W), lambda b: (b, 0, 0)),
                  pl.BlockSpec((2, GROUP_W), lambda b: (0, 0)),
                  pl.BlockSpec((1, GROUP_W), lambda b: (0, 0))],
        out_specs=pl.BlockSpec((1, t, GROUP_W), lambda b: (b, 0, 0)),
        out_shape=jax.ShapeDtypeStruct((bsz, t, GROUP_W), F32),
        scratch_shapes=[pltpu.VMEM((2, GROUP_W // LANES, t, LANES), F32), pltpu.VMEM((2, GROUP_W // LANES, t, LANES), F32),
                        pltpu.VMEM((GROUP_W // LANES, t, LANES), F32),
                        pltpu.VMEM((2, t, GROUP_W), F32), pltpu.VMEM((2, GROUP_W, GROUP_W), F32)],
        compiler_params=_cparams("arbitrary"),
    )(z, lb.astype(F32), norm_w.astype(F32).reshape(1, GROUP_W))


def _lane_expand(cols, seg):
    lane = lax.broadcasted_iota(jnp.int32, (1, len(cols) * seg), 1)
    out = cols[-1]
    for h in range(len(cols) - 2, -1, -1):
        out = jnp.where(lane < (h + 1) * seg, cols[h], out)
    return out


def _mlstm_body(q_ref, k_ref, v_ref, o_ref, zg_ref, gr_ref, nw_ref, out_ref,
                bce, lice, brow, hsc, cst, nst, *, t_len):
    n_all = t_len // ML_CHUNK
    n_ctx = CTX_LEN // ML_CHUNK
    g = zg_ref[0]
    lf = jax.nn.log_sigmoid(g)
    src = lax.broadcasted_iota(jnp.int32, (LANES, GROUP_W), 0)
    dst_head = lax.broadcasted_iota(jnp.int32, (LANES, GROUP_W), 1) // ML_DK
    bd, _, _ = _head_blocks(GROUP_W, ML_DK)
    ebd = jnp.where(bd, 1.0, 0.0).astype(BF16)
    for d in range(2):
        lice[d] = _dot3(g, jnp.where(src == ML_HEADS * d + dst_head, 1.0, 0.0).astype(BF16))
        lf_exp = _dot3(lf, jnp.where(src == ML_HEADS * (2 + d) + dst_head, 1.0, 0.0).astype(BF16))
        for j, blk in enumerate(_chunk_cumsum_cols(lf_exp, ML_CHUNK, d == 1)):
            bce[d, j * ROW_TILE:(j + 1) * ROW_TILE, :] = blk
        _, ri, ci = _head_blocks(GROUP_W, ML_CHUNK)
        tri = jnp.where(bd & ((ri >= ci) if d else (ri <= ci)), 1.0, 0.0).astype(BF16)
        brow[d] = _dot3(jax.nn.log_sigmoid(gr_ref[0, 2 + d]), tri)
    cst[...] = jnp.zeros(cst.shape, F32)
    nst[...] = jnp.zeros(nst.shape, F32)
    tt = lax.broadcasted_iota(jnp.int32, (ML_CHUNK, GROUP_W), 0)
    ss = lax.broadcasted_iota(jnp.int32, (ML_CHUNK, GROUP_W), 1) % ML_CHUNK
    lane_h = lax.broadcasted_iota(jnp.int32, (1, LANES), 1) < ML_CHUNK

    def chunk(d, c, m_prev):
        rows = pl.ds(pl.multiple_of(c * ML_CHUNK, ML_CHUNK), ML_CHUNK)
        q = q_ref[0, rows, :]
        k = k_ref[0, rows, :] * (ML_DK ** -0.5)
        v = v_ref[0, rows, :]
        bc = bce[d, rows, :]
        lic = lice[d, rows, :]
        br = brow[d, pl.ds(c, 1), :]
        lir = gr_ref[0, d, pl.ds(c, 1), :]
        b_end = bc[0:1] if d else bc[ML_CHUNK - 1:ML_CHUNK]
        valid = (ss >= tt) if d else (ss <= tt)
        d_log = jnp.where(valid, bc - br + lir, -jnp.inf)
        cols = []
        for hp in range(GROUP_W // LANES):
            xs = d_log[:, hp * LANES:(hp + 1) * LANES]
            cols.append(jnp.max(jnp.where(lane_h, xs, -jnp.inf), axis=-1, keepdims=True))
            cols.append(jnp.max(jnp.where(lane_h, -jnp.inf, xs), axis=-1, keepdims=True))
        w_inter = bc + m_prev
        m_t = jnp.maximum(w_inter, _lane_expand(cols, ML_CHUNK))
        e_inter = jnp.exp(w_inter - m_t)
        qb = q.astype(BF16)
        kbd = jnp.where(bd, jnp.concatenate([k] * ML_HEADS, axis=0).T, 0.0).astype(BF16)
        vbd = jnp.where(bd, jnp.concatenate([v] * ML_HEADS, axis=0), 0.0).astype(BF16)
        wts = jnp.exp(d_log - m_t) * jnp.dot(qb, kbd, preferred_element_type=F32)
        w_b = wts.astype(BF16)
        c_prev, n_prev = cst[d], nst[d]
        num = (jnp.dot(w_b, vbd, preferred_element_type=F32)
               + e_inter * jnp.dot(qb, c_prev.astype(BF16), preferred_element_type=F32))
        den = (jnp.dot(w_b, ebd, preferred_element_type=F32)
               + e_inter * jnp.dot(qb, n_prev.astype(BF16), preferred_element_type=F32))
        hsc[d, rows, :] = num / jnp.maximum(jnp.abs(den), jnp.exp(-m_t))
        w_end = b_end - bc + lic
        m_loc = jnp.max(w_end, axis=0, keepdims=True)
        e_end = jnp.exp(w_end - m_loc)
        m_new = jnp.maximum(b_end + m_prev, m_loc)
        a_old = jnp.exp(b_end + m_prev - m_new)
        a_loc = jnp.exp(m_loc - m_new)
        ev2 = jnp.concatenate([e_end * v, e_end], axis=1).astype(BF16)
        upd = lax.dot_general(k.astype(BF16), ev2, (((0,), (0,)), ((), ())), preferred_element_type=F32)
        cst[d] = a_old * c_prev + a_loc * jnp.where(bd, upd[:, :GROUP_W], 0.0)
        nst[d] = a_old * n_prev + a_loc * jnp.where(bd, upd[:, GROUP_W:], 0.0)
        return m_new

    def step(i, carry):
        return chunk(0, i, carry[0]), chunk(1, _bwd_chunk(i, n_ctx, n_all), carry[1])

    zero = jnp.zeros((1, GROUP_W), F32)
    lax.fori_loop(0, n_all, step, (zero, zero), unroll=4)
    out_ref[0] = _head_rms(hsc[0] + hsc[1], ebd, ML_DV) * nw_ref[...] * jax.nn.sigmoid(o_ref[0])


def _mlstm_mixer(z, gates_t, norm_w):
    bsz, t, _ = z.shape
    nc = t // ML_CHUNK
    ncp = -(-nc // 8) * 8
    gr = gates_t.reshape(bsz, 4, ML_HEADS, nc, ML_CHUNK).transpose(0, 1, 3, 2, 4).reshape(bsz, 4, nc, GROUP_W)
    gr = jnp.pad(gr, ((0, 0), (0, 0), (0, ncp - nc), (0, 0)))
    col = ML_OFF // GROUP_W
    zspec = lambda j: pl.BlockSpec((1, t, GROUP_W), lambda b: (b, 0, col + j))
    return pl.pallas_call(
        functools.partial(_mlstm_body, t_len=t),
        grid=(bsz,),
        in_specs=[zspec(0), zspec(1), zspec(2), zspec(3),
                  pl.BlockSpec((1, t, LANES), lambda b: (b, 0, ML_GATE_OFF // LANES)),
                  pl.BlockSpec((1, 4, ncp, GROUP_W), lambda b: (b, 0, 0, 0)),
                  pl.BlockSpec((1, GROUP_W), lambda b: (0, 0))],
        out_specs=pl.BlockSpec((1, t, GROUP_W), lambda b: (b, 0, 0)),
        out_shape=jax.ShapeDtypeStruct((bsz, t, GROUP_W), F32),
        scratch_shapes=[pltpu.VMEM((2, t, GROUP_W), F32), pltpu.VMEM((2, t, GROUP_W), F32),
                        pltpu.VMEM((2, ncp, GROUP_W), F32), pltpu.VMEM((2, t, GROUP_W), F32),
                        pltpu.VMEM((2, GROUP_W, GROUP_W), F32), pltpu.VMEM((2, GROUP_W, GROUP_W), F32)],
        compiler_params=_cparams("arbitrary"),
    )(z, z, z, z, z, gr, norm_w.astype(F32).reshape(1, GROUP_W))


def kernel(x, c, ctx, c_ctx, mod_w, mod_b, norm1_w, norm2_w, w_in, b_in, hg_lb_logits, hg_norm_w,
           s5_lam_re, s5_lam_im, s5_log_dt, s5_b_re, s5_b_im, s5_c_re, s5_c_im, s5_d, s5_glu_w, s5_glu_b,
           da_lq1, da_lk1, da_lq2, da_lk2, da_norm_w, ml_norm_w, w_out, router_w,
           exp_w1, exp_w3, exp_w2, final_norm_w):
    bsz, seq, d = x.shape
    assert ctx.shape[1] == CTX_LEN == ROW_TILE and seq % ROW_TILE == 0 and d == D_MODEL
    t = CTX_LEN + seq
    lb_all = jnp.cumsum(jax.nn.softmax(hg_lb_logits.astype(F32), axis=0), axis=0)
    lb_all = lb_all - lb_all[0]

    n_rows = -(-(bsz + 1) // 8) * 8
    cv = jnp.zeros((n_rows, d), F32).at[:bsz].set(c).at[bsz].set(c_ctx)
    mod_all = _modulation(cv, mod_w, mod_b)
    cos_t, sin_t = _rope_tables(t)
    h = jnp.concatenate([ctx, x], axis=1)
    w1, w3, w2 = exp_w1.astype(BF16), exp_w3.astype(BF16), exp_w2.astype(BF16)

    for li in range(DEPTH):
        ctx_out = li < DEPTH - 1
        m6 = mod_all[li].reshape(n_rows, 6, d)
        modsel = jnp.stack([jnp.broadcast_to(m6[bsz], (bsz, 6, d)), m6[:bsz]], axis=1)
        w_pad = jnp.pad(w_in[li], ((0, 0), (0, IN_COLS_PAD - IN_COLS))).astype(BF16)
        b_pad = jnp.pad(b_in[li], (0, IN_COLS_PAD - IN_COLS)).reshape(1, IN_COLS_PAD)
        wg_t = w_in[li][:, ML_GATE_OFF:].T.astype(BF16)
        bg_col = b_in[li][ML_GATE_OFF:].reshape(N_GATES, 1)
        z, gates_t = _in_projection(h, modsel, norm1_w[li], w_pad, b_pad, wg_t, bg_col)

        mix_a = _hgrn2_mixer(z, lb_all[li], hg_norm_w[li])
        mix_b = _s5_mixer(z, s5_lam_re[li], s5_lam_im[li], s5_log_dt[li], s5_b_re[li], s5_b_im[li],
                          s5_c_re[li], s5_c_im[li], s5_d[li], s5_glu_w[li], s5_glu_b[li])
        lam_init = 0.8 - 0.6 * math.exp(-0.3 * li)
        att_args = (da_lq1[li], da_lk1[li], da_lq2[li], da_lk2[li], da_norm_w[li], lam_init)
        c_l = _diff_attention(z, CTX_LEN, seq, t, True, cos_t, sin_t, *att_args)
        if ctx_out:
            c_c = _diff_attention(z, 0, CTX_LEN, CTX_LEN, False, cos_t, sin_t, *att_args)
        else:
            c_c = jnp.zeros((bsz, CTX_LEN, GROUP_W), F32)
        mix_c = jnp.concatenate([c_c, c_l], axis=1)
        mix_d = _mlstm_mixer(z, gates_t, ml_norm_w[li])

        hm, xm, aff_t = _out_projection((mix_a, mix_b, mix_c, mix_d), h, modsel, norm2_w[li],
                                        w_out[li].astype(BF16), router_w[li].T.astype(BF16))
        slot, slot_t = _route(aff_t, CTX_LEN, seq)
        y = _moe_ffn(xm, slot, aff_t, CTX_LEN, seq, 1, w1, w3, w2, li)
        if not ctx_out:
            return _moe_combine(y, slot_t, hm, modsel, CTX_LEN, seq, 1, final_w=final_norm_w)
        h = _moe_combine(y, slot_t, hm, modsel, CTX_LEN, seq, 1)
        slot, slot_t = _route(aff_t, 0, CTX_LEN)
        y = _moe_ffn(xm, slot, aff_t, 0, CTX_LEN, bsz, w1, w3, w2, li)
        h = _moe_combine(y, slot_t, h, modsel, 0, CTX_LEN, 0)
```

```python
import functools
import math
import jax
import jax.numpy as jnp
from jax import lax
from jax.experimental import pallas as pl
from jax.experimental.pallas import tpu as pltpu

D_MODEL = 1024
DEPTH = 2
CTX_LEN = 256
GRID_W = 64
N_MIXERS = 4
GROUP_W = D_MODEL // N_MIXERS
MIX_W = N_MIXERS * GROUP_W
EPS = 1e-6
F32 = jnp.float32
BF16 = jnp.bfloat16
HG_HEADS = 4
HG_DK = GROUP_W // HG_HEADS
HG_DV = GROUP_W // HG_HEADS
HG_CHUNK = 16
S5_CH = 16
S5_GROUPS = GROUP_W // S5_CH
S5_STATE = 64
DA_HEADS = 4
DA_DQK = GROUP_W // (2 * DA_HEADS)
DA_DV = GROUP_W // DA_HEADS
ROPE_BASE = 10000.0
ML_HEADS = 4
ML_DK = GROUP_W // ML_HEADS
ML_DV = GROUP_W // ML_HEADS
ML_CHUNK = 64
N_EXPERTS = 16
EC_CAPACITY = 2
D_EXPERT = 2 * D_MODEL
HG_OFF = 0
S5_OFF = HG_OFF + 5 * GROUP_W
DA_OFF = S5_OFF + GROUP_W
ML_OFF = DA_OFF + 3 * GROUP_W
ML_GATE_OFF = ML_OFF + 4 * GROUP_W
IN_COLS = ML_GATE_OFF + 4 * ML_HEADS

LANES = 128
ROW_TILE = 256
PROJ_TILE = 3 * ROW_TILE
IN_COLS_PAD = ML_GATE_OFF + LANES
N_GATES = 4 * ML_HEADS
S5_LANES = S5_GROUPS * S5_STATE
S5_HALF = S5_LANES // 2
SUBLANES = 8
S5_SEGS = SUBLANES
VMEM_LIMIT = 56 * 1024 * 1024
LOG2_E = math.log2(math.e)


def _cparams(*sem):
    return pltpu.CompilerParams(dimension_semantics=sem, vmem_limit_bytes=VMEM_LIMIT)


def _rms(x):
    return x * lax.rsqrt(jnp.mean(x * x, axis=-1, keepdims=True) + EPS)


def _mod_body(cv_ref, w_ref, b_ref, o_ref):
    cv = cv_ref[...]
    s = cv * jax.nn.sigmoid(cv)
    o_ref[0] = jnp.dot(s.astype(BF16), w_ref[0].astype(BF16), preferred_element_type=F32) + b_ref[0]


def _modulation(cv, mod_w, mod_b):
    n_l, d, n6 = mod_w.shape
    r = cv.shape[0]
    tn = n6 // 4
    return pl.pallas_call(
        _mod_body,
        grid=(n_l, 4),
        in_specs=[pl.BlockSpec((r, d), lambda l, j: (0, 0)),
                  pl.BlockSpec((1, d, tn), lambda l, j: (l, 0, j)),
                  pl.BlockSpec((1, 1, tn), lambda l, j: (l, 0, j))],
        out_specs=pl.BlockSpec((1, r, tn), lambda l, j: (l, 0, j)),
        out_shape=jax.ShapeDtypeStruct((n_l, r, n6), F32),
        compiler_params=_cparams("arbitrary", "arbitrary"),
    )(cv, mod_w, mod_b.reshape(n_l, 1, n6))


def _mod_rows(m_ref, rows, which):
    r = pl.program_id(1) * rows + lax.broadcasted_iota(jnp.int32, (rows, 1), 0)
    return [jnp.where(r < CTX_LEN, m_ref[0, 0, k:k + 1, :], m_ref[0, 1, k:k + 1, :]) for k in which]


def _proj_tile(t):
    return PROJ_TILE if t % PROJ_TILE == 0 else ROW_TILE


def _inproj_body(h_ref, m_ref, nw_ref, w_ref, b_ref, wg_ref, bg_ref, z_ref, gt_ref):
    shift, scale = _mod_rows(m_ref, h_ref.shape[1], (0, 1))
    xn = _rms(h_ref[0]) * nw_ref[...] * (1.0 + scale) + shift
    xb = xn.astype(BF16)
    z_ref[0] = jnp.dot(xb, w_ref[...], preferred_element_type=F32) + b_ref[...]
    gt_ref[0] = lax.dot_general(wg_ref[...], xb, (((1,), (1,)), ((), ())),
                                preferred_element_type=F32) + bg_ref[...]


def _in_projection(h, modsel, norm_w, w_pad, b_pad, wg_t, bg_col):
    bsz, t, d = h.shape
    tile = _proj_tile(t)
    return pl.pallas_call(
        _inproj_body,
        grid=(bsz, t // tile),
        in_specs=[pl.BlockSpec((1, tile, d), lambda b, j: (b, j, 0)),
                  pl.BlockSpec((1, 2, 6, d), lambda b, j: (b, 0, 0, 0)),
                  pl.BlockSpec((1, d), lambda b, j: (0, 0)),
                  pl.BlockSpec((d, IN_COLS_PAD), lambda b, j: (0, 0)),
                  pl.BlockSpec((1, IN_COLS_PAD), lambda b, j: (0, 0)),
                  pl.BlockSpec((N_GATES, d), lambda b, j: (0, 0)),
                  pl.BlockSpec((N_GATES, 1), lambda b, j: (0, 0))],
        out_specs=[pl.BlockSpec((1, tile, IN_COLS_PAD), lambda b, j: (b, j, 0)),
                   pl.BlockSpec((1, N_GATES, tile), lambda b, j: (b, 0, j))],
        out_shape=[jax.ShapeDtypeStruct((bsz, t, IN_COLS_PAD), F32),
                   jax.ShapeDtypeStruct((bsz, N_GATES, t), F32)],
        compiler_params=_cparams("arbitrary", "arbitrary"),
    )(h, modsel, norm_w.reshape(1, d), w_pad, b_pad, wg_t, bg_col)


def _outproj_body(a_ref, b_ref, c_ref, d_ref, h_ref, m_ref, nw_ref, wo_ref, rw_ref, hm_ref, xm_ref, aff_ref):
    gate, shift, scale = _mod_rows(m_ref, h_ref.shape[1], (2, 3, 4))
    y = jnp.concatenate([a_ref[0], b_ref[0], c_ref[0], d_ref[0]], axis=-1).astype(BF16)
    hm = h_ref[0] + gate * jnp.dot(y, wo_ref[...], preferred_element_type=F32)
    hm_ref[0] = hm
    xm = (_rms(hm) * nw_ref[...] * (1.0 + scale) + shift).astype(BF16)
    xm_ref[0] = xm
    logit = lax.dot_general(rw_ref[...], xm, (((1,), (1,)), ((), ())), preferred_element_type=F32)
    e = jnp.exp(logit - jnp.max(logit, axis=0, keepdims=True))
    aff_ref[0] = e / jnp.sum(e, axis=0, keepdims=True)


def _out_projection(mix, h, modsel, norm_w, wo_bf, rw_t):
    bsz, t, d = h.shape
    tile = _proj_tile(t)
    mix_spec = pl.BlockSpec((1, tile, GROUP_W), lambda b, j: (b, j, 0))
    row_spec = pl.BlockSpec((1, tile, d), lambda b, j: (b, j, 0))
    return pl.pallas_call(
        _outproj_body,
        grid=(bsz, t // tile),
        in_specs=[mix_spec, mix_spec, mix_spec, mix_spec, row_spec,
                  pl.BlockSpec((1, 2, 6, d), lambda b, j: (b, 0, 0, 0)),
                  pl.BlockSpec((1, d), lambda b, j: (0, 0)),
                  pl.BlockSpec((MIX_W, d), lambda b, j: (0, 0)),
                  pl.BlockSpec((N_EXPERTS, d), lambda b, j: (0, 0))],
        out_specs=[row_spec, row_spec, pl.BlockSpec((1, N_EXPERTS, tile), lambda b, j: (b, 0, j))],
        out_shape=[jax.ShapeDtypeStruct((bsz, t, d), F32), jax.ShapeDtypeStruct((bsz, t, d), BF16),
                   jax.ShapeDtypeStruct((bsz, N_EXPERTS, t), F32)],
        compiler_params=_cparams("arbitrary", "arbitrary"),
    )(*mix, h, modsel, norm_w.reshape(1, d), wo_bf, rw_t)


def _topk_body(aff_ref, slot_ref, slot_t_ref, *, cap, n, row0):
    aff = aff_ref[0, :, row0:row0 + n]
    bits = pltpu.bitcast(aff, jnp.int32)
    thr = jnp.zeros((N_EXPERTS, 1), jnp.int32)
    for bit in range(30, -1, -1):
        cand = thr | (1 << bit)
        cnt = jnp.sum(jnp.where(bits >= cand, 1.0, 0.0), axis=-1, keepdims=True)
        thr = jnp.where(cnt >= cap, cand, thr)
    room = cap - jnp.sum(jnp.where(bits > thr, 1.0, 0.0), axis=-1, keepdims=True)
    ri = lax.broadcasted_iota(jnp.int32, (LANES, LANES), 0)
    ci = lax.broadcasted_iota(jnp.int32, (LANES, LANES), 1)
    incl = jnp.where(ri <= ci, 1.0, 0.0).astype(BF16)
    off_eq = jnp.zeros((N_EXPERTS, 1), F32)
    off_sel = jnp.zeros((N_EXPERTS, 1), F32)
    pieces = []
    for j in range(n // LANES):
        sl = slice(j * LANES, (j + 1) * LANES)
        bits_b = bits[:, sl]
        eq_b = jnp.where(bits_b == thr, 1.0, 0.0)
        pos_eq = jnp.dot(eq_b.astype(BF16), incl, preferred_element_type=F32) - eq_b + off_eq
        sel = jnp.where(bits_b > thr, 1.0, jnp.where(pos_eq < room, eq_b, 0.0))
        pos_sel = jnp.dot(sel.astype(BF16), incl, preferred_element_type=F32) - sel + off_sel
        piece = jnp.where(sel > 0.5, pos_sel, -1.0)
        slot_ref[0, :, sl] = piece.astype(jnp.int32)
        pieces.append(piece)
        off_eq = off_eq + jnp.sum(eq_b, axis=-1, keepdims=True)
        off_sel = off_sel + jnp.sum(sel, axis=-1, keepdims=True)
    pad = jnp.full((LANES - N_EXPERTS, LANES), -1.0, F32)
    for j in range(n // LANES):
        tile = jnp.concatenate([pieces[j], pad], axis=0)
        slot_t_ref[0, j * LANES:(j + 1) * LANES, :] = tile.T.astype(jnp.int32)


def _route(aff_t, row0, n):
    bsz, _, t = aff_t.shape
    cap = EC_CAPACITY * n // N_EXPERTS
    return pl.pallas_call(
        functools.partial(_topk_body, cap=cap, n=n, row0=row0),
        grid=(bsz,),
        in_specs=[pl.BlockSpec((1, N_EXPERTS, t), lambda b: (b, 0, 0))],
        out_specs=[pl.BlockSpec((1, N_EXPERTS, n), lambda b: (b, 0, 0)),
                   pl.BlockSpec((1, n, LANES), lambda b: (b, 0, 0))],
        out_shape=[jax.ShapeDtypeStruct((bsz, N_EXPERTS, n), jnp.int32),
                   jax.ShapeDtypeStruct((bsz, n, LANES), jnp.int32)],
        compiler_params=_cparams("arbitrary"),
    )(aff_t)


def _moe_ffn_body(x_ref, slot_ref, aff_ref, w1_ref, w3_ref, w2_ref, y_ref, xe_scr, g_scr, *, nb, cap, n, off):
    for i in range(nb):
        hit = lax.broadcasted_iota(jnp.int32, (cap, n), 0) == slot_ref[i, 0]
        onehot = jnp.where(hit, 1.0, 0.0).astype(BF16)
        xe_scr[i * cap:(i + 1) * cap, :] = jnp.dot(onehot, x_ref[i, off:off + n, :],
                                                   preferred_element_type=F32).astype(BF16)
        g_scr[i * cap:(i + 1) * cap, :] = jnp.sum(jnp.where(hit, aff_ref[i, 0, :, off:off + n], 0.0),
                                                  axis=-1, keepdims=True)
    xe = xe_scr[...]
    h1 = jnp.dot(xe, w1_ref[0, 0], preferred_element_type=F32)
    h3 = jnp.dot(xe, w3_ref[0, 0], preferred_element_type=F32)
    act = (h1 * jax.nn.sigmoid(h1) * h3).astype(BF16)
    y = (jnp.dot(act, w2_ref[0, 0], preferred_element_type=F32) * g_scr[...]).astype(BF16)
    for i in range(nb):
        y_ref[i, 0] = y[i * cap:(i + 1) * cap]


def _moe_ffn(xm, slot, aff_t, row0, n, nb, w1, w3, w2, li):
    bsz, t, d = xm.shape
    cap = EC_CAPACITY * n // N_EXPERTS
    rows, off = (n, 0) if row0 % n == 0 else (t, row0)
    blk = row0 // n if off == 0 else 0
    slot4 = slot.reshape(bsz, N_EXPERTS, 1, n)
    aff4 = aff_t.reshape(bsz, N_EXPERTS, 1, t)
    return pl.pallas_call(
        functools.partial(_moe_ffn_body, nb=nb, cap=cap, n=n, off=off),
        grid=(N_EXPERTS, bsz // nb),
        in_specs=[pl.BlockSpec((nb, rows, d), lambda e, b: (b, blk, 0)),
                  pl.BlockSpec((nb, 1, 1, n), lambda e, b: (b, e, 0, 0)),
                  pl.BlockSpec((nb, 1, 1, rows), lambda e, b: (b, e, 0, blk)),
                  pl.BlockSpec((1, 1, d, D_EXPERT), lambda e, b: (li, e, 0, 0)),
                  pl.BlockSpec((1, 1, d, D_EXPERT), lambda e, b: (li, e, 0, 0)),
                  pl.BlockSpec((1, 1, D_EXPERT, d), lambda e, b: (li, e, 0, 0))],
        out_specs=pl.BlockSpec((nb, 1, cap, d), lambda e, b: (b, e, 0, 0)),
        out_shape=jax.ShapeDtypeStruct((bsz, N_EXPERTS, cap, d), BF16),
        scratch_shapes=[pltpu.VMEM((nb * cap, d), BF16), pltpu.VMEM((nb * cap, 1), F32)],
        compiler_params=_cparams("arbitrary", "arbitrary"),
    )(xm, slot4, aff4, w1, w3, w2)


def _moe_combine_body(y_ref, st_ref, hm_ref, m_ref, fw_ref, o_ref, *, cap, final):
    st = st_ref[0]
    lane = lax.broadcasted_iota(jnp.int32, (st.shape[0], cap), 1)
    hots = [jnp.where(lane == st[:, e:e + 1], 1.0, 0.0).astype(BF16) for e in range(N_EXPERTS)]
    if cap % LANES == 0:
        acc = jnp.dot(jnp.concatenate(hots, axis=1), y_ref[0].reshape(N_EXPERTS * cap, y_ref.shape[-1]),
                      preferred_element_type=F32)
    else:
        acc = sum(jnp.dot(hots[e], y_ref[0, e], preferred_element_type=F32) for e in range(N_EXPERTS))
    h = hm_ref[0] + m_ref[0, 0][5:6] * acc
    o_ref[0] = _rms(h) * fw_ref[...] if final else h


def _moe_combine(y, slot_t, hm, modsel, row0, n, is_latent, final_w=None):
    bsz, _, cap, d = y.shape
    nt = n // ROW_TILE
    blk0 = row0 // ROW_TILE
    final = final_w is not None
    fw = (final_w if final else jnp.ones((d,), F32)).reshape(1, d)
    return pl.pallas_call(
        functools.partial(_moe_combine_body, cap=cap, final=final),
        grid=(bsz, nt),
        in_specs=[pl.BlockSpec((1, N_EXPERTS, cap, d), lambda b, j: (b, 0, 0, 0)),
                  pl.BlockSpec((1, ROW_TILE, LANES), lambda b, j: (b, j, 0)),
                  pl.BlockSpec((1, ROW_TILE, d), lambda b, j: (b, blk0 + j, 0)),
                  pl.BlockSpec((1, 1, 6, d), lambda b, j: (b, is_latent, 0, 0)),
                  pl.BlockSpec((1, d), lambda b, j: (0, 0))],
        out_specs=pl.BlockSpec((1, ROW_TILE, d), lambda b, j: (b, j if final else blk0 + j, 0)),
        out_shape=jax.ShapeDtypeStruct((bsz, n, d) if final else hm.shape, F32),
        input_output_aliases={} if final else {2: 0},
        compiler_params=_cparams("arbitrary", "arbitrary"),
    )(y, slot_t, hm, modsel, fw)


def _swap_pairs(x):
    w = x.shape[-1]
    lane = lax.broadcasted_iota(jnp.int32, x.shape, x.ndim - 1)
    return jnp.where(lane % 2 == 0, pltpu.roll(x, w - 1, x.ndim - 1), pltpu.roll(x, 1, x.ndim - 1))


def _attn_body(q_ref, k_ref, v_ref, cq_ref, sq_ref, ck_ref, sk_ref, lq1_ref, lk1_ref, lq2_ref, lk2_ref, nw_ref,
               o_ref, k_scr, vt_scr, *, rope, lam_init):
    @pl.when(pl.program_id(1) == 0)
    def _():
        k = k_ref[0]
        if rope:
            k = k * ck_ref[...] + _swap_pairs(k) * sk_ref[...]
        k_scr[...] = k.astype(BF16)
        for r in range(0, k_scr.shape[0], ROW_TILE):
            vt_scr[:, r:r + ROW_TILE] = v_ref[0, r:r + ROW_TILE, :].T.astype(BF16)

    q = q_ref[0]
    if rope:
        q = q * cq_ref[...] + _swap_pairs(q) * sq_ref[...]
    q_t = (q * (DA_DQK ** -0.5 * LOG2_E)).T
    lam = (jnp.exp(jnp.sum(lq1_ref[...] * lk1_ref[...], axis=-1, keepdims=True))
           - jnp.exp(jnp.sum(lq2_ref[...] * lk2_ref[...], axis=-1, keepdims=True)) + lam_init)
    row = lax.broadcasted_iota(jnp.int32, (GROUP_W, 1), 0)
    kk = k_scr[...]
    outs = []
    for h in range(DA_HEADS):
        qm = jnp.concatenate([jnp.where(row // DA_DQK == 2 * h + mi, q_t, 0.0) for mi in range(2)], axis=1)
        s = jnp.dot(kk, qm.astype(BF16), preferred_element_type=F32)
        e = jnp.exp2(s - jnp.max(s, axis=0, keepdims=True))
        pv = (jnp.dot(vt_scr[h * DA_DV:(h + 1) * DA_DV, :], e.astype(BF16), preferred_element_type=F32)
              / jnp.sum(e, axis=0, keepdims=True))
        nq = q_t.shape[1]
        oh = pv[:, :nq] - lam * pv[:, nq:]
        outs.append(oh * lax.rsqrt(jnp.mean(oh * oh, axis=0, keepdims=True) + EPS))
    o_ref[0] = jnp.concatenate(outs, axis=0).T * nw_ref[...] * (1.0 - lam_init)


def _diff_attention(z, q_row0, nq, nk, rope, cos_t, sin_t, lq1, lk1, lq2, lk2, norm_w, lam_init):
    bsz = z.shape[0]
    qb = ROW_TILE
    q0 = q_row0 // qb
    col = DA_OFF // GROUP_W
    vec = lambda a: a.reshape(1, -1)
    small = pl.BlockSpec((1, DA_DQK), lambda b, j: (0, 0))
    return pl.pallas_call(
        functools.partial(_attn_body, rope=rope, lam_init=lam_init),
        grid=(bsz, nq // qb),
        in_specs=[pl.BlockSpec((1, qb, GROUP_W), lambda b, j: (b, q0 + j, col)),
                  pl.BlockSpec((1, nk, GROUP_W), lambda b, j: (b, 0, col + 1)),
                  pl.BlockSpec((1, nk, GROUP_W), lambda b, j: (b, 0, col + 2)),
                  pl.BlockSpec((qb, GROUP_W), lambda b, j: (q0 + j, 0)),
                  pl.BlockSpec((qb, GROUP_W), lambda b, j: (q0 + j, 0)),
                  pl.BlockSpec((nk, GROUP_W), lambda b, j: (0, 0)),
                  pl.BlockSpec((nk, GROUP_W), lambda b, j: (0, 0)),
                  small, small, small, small,
                  pl.BlockSpec((1, GROUP_W), lambda b, j: (0, 0))],
        out_specs=pl.BlockSpec((1, qb, GROUP_W), lambda b, j: (b, j, 0)),
        out_shape=jax.ShapeDtypeStruct((bsz, nq, GROUP_W), F32),
        scratch_shapes=[pltpu.VMEM((nk, GROUP_W), BF16), pltpu.VMEM((GROUP_W, nk), BF16)],
        compiler_params=_cparams("arbitrary", "arbitrary"),
    )(z, z, z, cos_t, sin_t, cos_t, sin_t, vec(lq1), vec(lk1), vec(lq2), vec(lk2), vec(norm_w))


def _rope_tables(t):
    n = t - CTX_LEN
    axis_dim = DA_DQK // 2
    inv = ROPE_BASE ** (-jnp.arange(0, axis_dim, 2, dtype=F32) / axis_dim)
    tok = jnp.arange(n, dtype=jnp.int32)
    row = (tok // GRID_W).astype(F32)
    colp = (tok % GRID_W).astype(F32)
    ang = jnp.concatenate([row[:, None] * inv, colp[:, None] * inv], axis=-1)
    cos = jnp.repeat(jnp.cos(ang), 2, axis=-1)
    sin = jnp.repeat(jnp.sin(ang), 2, axis=-1) * jnp.tile(jnp.array([-1.0, 1.0], F32), DA_DQK // 2)
    reps = GROUP_W // DA_DQK
    cos = jnp.concatenate([jnp.ones((CTX_LEN, GROUP_W), F32), jnp.tile(cos, (1, reps))], axis=0)
    sin = jnp.concatenate([jnp.zeros((CTX_LEN, GROUP_W), F32), jnp.tile(sin, (1, reps))], axis=0)
    return cos, sin


def _cmul(ar, ai, br, bi):
    return ar * br - ai * bi, ar * bi + ai * br


def _s5_scan_part(bufs, coef, base, seg_len, carry):
    xrf, xif, xrb, xib = bufs
    af_r, af_i, ab_r, ab_i = coef
    assert seg_len & (seg_len - 1) == 0

    def sweep(store, init):
        def step(i, st):
            fr, fi, br, bi = st
            rf = pl.ds(pl.multiple_of(base + S5_SEGS * i, S5_SEGS), S5_SEGS)
            rb = pl.ds(pl.multiple_of(base + S5_SEGS * (seg_len - 1 - i), S5_SEGS), S5_SEGS)
            nfr = af_r * fr - af_i * fi + xrf[rf, :]
            nfi = af_r * fi + af_i * fr + xif[rf, :]
            nbr = ab_r * br - ab_i * bi + xrb[rb, :]
            nbi = ab_r * bi + ab_i * br + xib[rb, :]
            if store:
                xrf[rf, :] = nfr
                xif[rf, :] = nfi
                xrb[rb, :] = nbr
                xib[rb, :] = nbi
            return nfr, nfi, nbr, nbi
        return lax.fori_loop(0, seg_len, step, init, unroll=4)

    zero = jnp.zeros(af_r.shape, F32)
    ef_r, ef_i, eb_r, eb_i = sweep(False, (zero,) * 4)
    pf = (af_r[0:1], af_i[0:1])
    pb = (ab_r[0:1], ab_i[0:1])
    for _ in range(seg_len.bit_length() - 1):
        pf = _cmul(*pf, *pf)
        pb = _cmul(*pb, *pb)
    cf_r, cf_i, cb_r, cb_i = carry
    ins_f, ins_b = [], []
    for j in range(S5_SEGS):
        ins_f.append((cf_r, cf_i))
        gr, gi = _cmul(*pf, cf_r, cf_i)
        cf_r, cf_i = ef_r[j:j + 1] + gr, ef_i[j:j + 1] + gi
    for j in range(S5_SEGS - 1, -1, -1):
        ins_b.append((cb_r, cb_i))
        gr, gi = _cmul(*pb, cb_r, cb_i)
        cb_r, cb_i = eb_r[j:j + 1] + gr, eb_i[j:j + 1] + gi
    ins_b.reverse()
    stack = lambda rows: jnp.concatenate(rows, axis=0)
    sweep(True, (stack([a for a, _ in ins_f]), stack([b for _, b in ins_f]),
                 stack([a for a, _ in ins_b]), stack([b for _, b in ins_b])))
    return cf_r, cf_i, cb_r, cb_i


def _seg_interleave(x):
    b, r, c = x.shape
    return x.reshape(b, S5_SEGS, r // S5_SEGS, c).transpose(0, 2, 1, 3).reshape(b, r, c)


def _seg_deinterleave(x):
    b, r, c = x.shape
    return x.reshape(b, r // S5_SEGS, S5_SEGS, c).transpose(0, 2, 1, 3).reshape(b, r, c)


def _s5_body(u_ref, bre_ref, bim_ref, cre_ref, cim_ref, disc_ref, dsk_ref, gw_ref, gb_ref, o_ref,
             xrf, xif, xrb, xib, y_scr, *, t_len):
    u = u_ref[0]
    ub = u.astype(BF16)
    y_scr[...] = u * dsk_ref[...]
    n_lat = t_len - CTX_LEN
    for half in range(2):
        ls = slice(half * S5_HALF, (half + 1) * S5_HALF)
        bu_r = jnp.dot(ub, bre_ref[:, ls], preferred_element_type=F32)
        bu_i = jnp.dot(ub, bim_ref[:, ls], preferred_element_type=F32)
        disc = disc_ref[:, ls]
        xrf[...] = disc[2:3] * bu_r - disc[3:4] * bu_i
        xif[...] = disc[2:3] * bu_i + disc[3:4] * bu_r
        xrb[...] = disc[6:7] * bu_r - disc[7:8] * bu_i
        xib[...] = disc[6:7] * bu_i + disc[7:8] * bu_r
        coef = tuple(jnp.broadcast_to(disc[r:r + 1], (S5_SEGS, S5_HALF)) for r in (0, 1, 4, 5))
        zero = jnp.zeros((1, S5_HALF), F32)
        carry = _s5_scan_part((xrf, xif, xrb, xib), coef, 0, CTX_LEN // S5_SEGS, (zero,) * 4)
        _s5_scan_part((xrf, xif, xrb, xib), coef, CTX_LEN, n_lat // S5_SEGS, carry)
        cre = cre_ref[ls, :]
        cim = cim_ref[ls, :]
        y_scr[...] += (jnp.dot((xrf[...] + xrb[...]).astype(BF16), cre, preferred_element_type=F32)
                       - jnp.dot((xif[...] + xib[...]).astype(BF16), cim, preferred_element_type=F32))
    y = jax.nn.gelu(y_scr[...])
    gate = jax.nn.sigmoid(jnp.dot(y.astype(BF16), gw_ref[...], preferred_element_type=F32) + gb_ref[...])
    o_ref[0] = y * gate


def _s5_mixer(z, lam_re, lam_im, log_dt, b_re, b_im, c_re, c_im, d_skip, glu_w, glu_b):
    bsz, t, _ = z.shape
    eye = jnp.eye(S5_GROUPS, dtype=F32)
    bbd = lambda w: jnp.einsum('gpc,gh->gchp', w.astype(F32), eye).reshape(GROUP_W, S5_LANES).astype(BF16)
    cbd = lambda w: jnp.einsum('gcp,gh->gphc', w.astype(F32), eye).reshape(S5_LANES, GROUP_W).astype(BF16)
    rows = []
    for d in range(2):
        lr, li = lam_re[d].astype(F32), lam_im[d].astype(F32)
        dt = jnp.exp(log_dt[d].astype(F32))[:, None]
        mag = jnp.exp(lr * dt)
        ab_re, ab_im = mag * jnp.cos(li * dt), mag * jnp.sin(li * dt)
        den = lr * lr + li * li
        co_re = ((ab_re - 1.0) * lr + ab_im * li) / den
        co_im = (ab_im * lr - (ab_re - 1.0) * li) / den
        rows += [ab_re, ab_im, co_re, co_im]
    disc = jnp.stack([r.reshape(S5_LANES) for r in rows], axis=0)
    full = lambda shape: pl.BlockSpec(shape, lambda b: (0,) * len(shape))
    u = z[:, :, S5_OFF:S5_OFF + GROUP_W]
    u = jnp.concatenate([_seg_interleave(u[:, :CTX_LEN]), _seg_interleave(u[:, CTX_LEN:])], axis=1)
    out = pl.pallas_call(
        functools.partial(_s5_body, t_len=t),
        grid=(bsz,),
        in_specs=[pl.BlockSpec((1, t, GROUP_W), lambda b: (b, 0, 0)),
                  full((GROUP_W, S5_LANES)), full((GROUP_W, S5_LANES)),
                  full((S5_LANES, GROUP_W)), full((S5_LANES, GROUP_W)),
                  full((8, S5_LANES)), full((1, GROUP_W)), full((GROUP_W, GROUP_W)), full((1, GROUP_W))],
        out_specs=pl.BlockSpec((1, t, GROUP_W), lambda b: (b, 0, 0)),
        out_shape=jax.ShapeDtypeStruct((bsz, t, GROUP_W), F32),
        scratch_shapes=[pltpu.VMEM((t, S5_HALF), F32)] * 4 + [pltpu.VMEM((t, GROUP_W), F32)],
        compiler_params=_cparams("arbitrary"),
    )(u, bbd(b_re), bbd(b_im), cbd(c_re), cbd(c_im), disc, d_skip.astype(F32).reshape(1, GROUP_W),
      glu_w.astype(BF16), glu_b.astype(F32).reshape(1, GROUP_W))
    return jnp.concatenate([_seg_deinterleave(out[:, :CTX_LEN]), _seg_deinterleave(out[:, CTX_LEN:])], axis=1)


def _split3(x):
    hi = x.astype(BF16)
    r = x - hi.astype(F32)
    mid = r.astype(BF16)
    return hi, mid, (r - mid.astype(F32)).astype(BF16)


def _dot3(x, sel):
    return sum(jnp.dot(p, sel, preferred_element_type=F32) for p in _split3(x))


def _mdot3(sel, x):
    return sum(jnp.dot(sel, p, preferred_element_type=F32) for p in _split3(x))


def _dot2(x, sel):
    hi = x.astype(BF16)
    lo = (x - hi.astype(F32)).astype(BF16)
    return jnp.dot(hi, sel, preferred_element_type=F32) + jnp.dot(lo, sel, preferred_element_type=F32)


def _head_blocks(n, seg):
    ri = lax.broadcasted_iota(jnp.int32, (n, n), 0)
    ci = lax.broadcasted_iota(jnp.int32, (n, n), 1)
    return ri // seg == ci // seg, ri, ci


def _chunk_cumsum_cols(x, chunk, reverse):
    same, ri, ci = _head_blocks(ROW_TILE, chunk)
    tri = jnp.where(same & ((ci >= ri) if reverse else (ci <= ri)), 1.0, 0.0).astype(BF16)
    return [_mdot3(tri, x[r:r + ROW_TILE]) for r in range(0, x.shape[0], ROW_TILE)]


def _repeat_row(row, n):
    return pl.ds(row, n, stride=0)


def _bwd_chunk(i, n_ctx, n_all):
    return jnp.where(i < n_ctx, n_ctx - 1 - i, n_all + n_ctx - 1 - i)


def _head_rms(x, ebd, seg):
    return x * lax.rsqrt(_dot2(x * x, ebd) * (1.0 / seg) + EPS)


def _hgrn2_body(z_ref, lb_ref, nw_ref, out_ref, bsc, ksc, vsc, osc, st, *, t_len):
    n_half = GROUP_W // LANES
    halves = lambda ref, idx, rows: jnp.concatenate([ref[idx + (h, rows)] for h in range(n_half)], axis=1)
    for h in range(n_half):
        vsc[h] = z_ref[0, :, GROUP_W + h * LANES:GROUP_W + (h + 1) * LANES]
    for d in range(2):
        lb = lb_ref[d:d + 1, :]
        f = lb + (1.0 - lb) * jax.nn.sigmoid(z_ref[0, :, (2 + d) * GROUP_W:(3 + d) * GROUP_W])
        k_all = 1.0 - f
        for h in range(n_half):
            ksc[d, h] = k_all[:, h * LANES:(h + 1) * LANES]
        for j, blk in enumerate(_chunk_cumsum_cols(jnp.log(f), HG_CHUNK, d == 1)):
            for h in range(n_half):
                bsc[d, h, j * ROW_TILE:(j + 1) * ROW_TILE, :] = blk[:, h * LANES:(h + 1) * LANES] * LOG2_E
    st[...] = jnp.zeros(st.shape, F32)
    bd, _, _ = _head_blocks(GROUP_W, HG_DK)
    ebd = jnp.where(bd, 1.0, 0.0).astype(BF16)
    tt = lax.broadcasted_iota(jnp.int32, (HG_CHUNK, GROUP_W), 0)
    n_all = t_len // HG_CHUNK
    n_ctx = CTX_LEN // HG_CHUNK

    def chunk(d, c):
        r0 = pl.multiple_of(c * HG_CHUNK, HG_CHUNK)
        rows = pl.ds(r0, HG_CHUNK)
        q = z_ref[0, rows, 0:GROUP_W] * (HG_DK ** -0.5)
        v = halves(vsc, (), rows)
        k = halves(ksc, (d,), rows)
        b = halves(bsc, (d,), rows)
        b_end = b[0:1] if d else b[HG_CHUNK - 1:HG_CHUNK]
        n_blk = HG_CHUNK // SUBLANES
        keep = lambda s, j: (j <= s // SUBLANES) if d else (j >= s // SUBLANES)
        pieces = []
        for s in range(HG_CHUNK):
            row_s = _repeat_row(r0 + s, HG_CHUNK)
            decay = jnp.exp2(b - halves(bsc, (d,), row_s))
            masked = jnp.where((tt <= s) if d else (tt >= s), decay, 0.0)
            qk = q * halves(ksc, (d,), row_s)
            for j in range(n_blk):
                if keep(s, j):
                    slab = (masked if j == s // SUBLANES else decay) * qk
                    pieces.append(slab[j * SUBLANES:(j + 1) * SUBLANES])
        a = jnp.concatenate(pieces, axis=0).astype(BF16)
        sc = jnp.dot(a, ebd, preferred_element_type=F32)
        o_intra = jnp.zeros((HG_CHUNK, GROUP_W), F32)
        i = 0
        for s in range(HG_CHUNK):
            blocks = []
            for j in range(n_blk):
                blocks.append(sc[i * SUBLANES:(i + 1) * SUBLANES] if keep(s, j) else jnp.zeros((SUBLANES, GROUP_W), F32))
                i += keep(s, j)
            o_intra = o_intra + jnp.concatenate(blocks, axis=0) * halves(vsc, (), _repeat_row(r0 + s, HG_CHUNK))
        s_prev = st[d]
        o_inter = lax.dot_general((q * jnp.exp2(b)).astype(BF16), s_prev.astype(BF16), (((1,), (1,)), ((), ())),
                                  preferred_element_type=F32)
        osc[d, rows, :] = o_intra + o_inter
        kd = (k * jnp.exp2(b_end - b)).astype(BF16)
        upd = lax.dot_general(v.astype(BF16), kd, (((0,), (0,)), ((), ())), preferred_element_type=F32)
        st[d] = s_prev * jnp.exp2(b_end) + jnp.where(bd, upd, 0.0)

    def step(i, carry):
        chunk(0, i)
        chunk(1, _bwd_chunk(i, n_ctx, n_all))
        return carry

    lax.fori_loop(0, n_all, step, 0, unroll=4)
    o = osc[0] + osc[1]
    g = z_ref[0, :, 4 * GROUP_W:5 * GROUP_W]
    out_ref[0] = _head_rms(o, ebd, HG_DV) * nw_ref[...] * (g * jax.nn.sigmoid(g))


def _hgrn2_mixer(z, lb, norm_w):
    bsz, t, _ = z.shape
    return pl.pallas_call(
        functools.partial(_hgrn2_body, t_len=t),
        grid=(bsz,),
        in_specs=[pl.BlockSpec((1, t, 5 * GROUP_W), lambda b: (b, 0, 0)),
                  pl.BlockSpec((2, GROUP_W), lambda b: (0, 0)),
                  pl.BlockSpec((1, GROUP_W), lambda b: (0, 0))],
        out_specs=pl.BlockSpec((1, t, GROUP_W), lambda b: (b, 0, 0)),
        out_shape=jax.ShapeDtypeStruct((bsz, t, GROUP_W), F32),
        scratch_shapes=[pltpu.VMEM((2, GROUP_W // LANES, t, LANES), F32), pltpu.VMEM((2, GROUP_W // LANES, t, LANES), F32),
                        pltpu.VMEM((GROUP_W // LANES, t, LANES), F32),
                        pltpu.VMEM((2, t, GROUP_W), F32), pltpu.VMEM((2, GROUP_W, GROUP_W), F32)],
        compiler_params=_cparams("arbitrary"),
    )(z, lb.astype(F32), norm_w.astype(F32).reshape(1, GROUP_W))


def _lane_expand(cols, seg):
    lane = lax.broadcasted_iota(jnp.int32, (1, len(cols) * seg), 1)
    out = cols[-1]
    for h in range(len(cols) - 2, -1, -1):
        out = jnp.where(lane < (h + 1) * seg, cols[h], out)
    return out


def _mlstm_body(q_ref, k_ref, v_ref, o_ref, zg_ref, gr_ref, nw_ref, out_ref,
                bce, lice, brow, hsc, cst, nst, *, t_len):
    n_all = t_len // ML_CHUNK
    n_ctx = CTX_LEN // ML_CHUNK
    g = zg_ref[0]
    lf = jax.nn.log_sigmoid(g)
    src = lax.broadcasted_iota(jnp.int32, (LANES, GROUP_W), 0)
    dst_head = lax.broadcasted_iota(jnp.int32, (LANES, GROUP_W), 1) // ML_DK
    bd, _, _ = _head_blocks(GROUP_W, ML_DK)
    ebd = jnp.where(bd, 1.0, 0.0).astype(BF16)
    for d in range(2):
        lice[d] = _dot3(g, jnp.where(src == ML_HEADS * d + dst_head, 1.0, 0.0).astype(BF16))
        lf_exp = _dot3(lf, jnp.where(src == ML_HEADS * (2 + d) + dst_head, 1.0, 0.0).astype(BF16))
        for j, blk in enumerate(_chunk_cumsum_cols(lf_exp, ML_CHUNK, d == 1)):
            bce[d, j * ROW_TILE:(j + 1) * ROW_TILE, :] = blk
        _, ri, ci = _head_blocks(GROUP_W, ML_CHUNK)
        tri = jnp.where(bd & ((ri >= ci) if d else (ri <= ci)), 1.0, 0.0).astype(BF16)
        brow[d] = _dot3(jax.nn.log_sigmoid(gr_ref[0, 2 + d]), tri)
    cst[...] = jnp.zeros(cst.shape, F32)
    nst[...] = jnp.zeros(nst.shape, F32)
    tt = lax.broadcasted_iota(jnp.int32, (ML_CHUNK, GROUP_W), 0)
    ss = lax.broadcasted_iota(jnp.int32, (ML_CHUNK, GROUP_W), 1) % ML_CHUNK
    lane_h = lax.broadcasted_iota(jnp.int32, (1, LANES), 1) < ML_CHUNK

    def chunk(d, c, m_prev):
        rows = pl.ds(pl.multiple_of(c * ML_CHUNK, ML_CHUNK), ML_CHUNK)
        q = q_ref[0, rows, :]
        k = k_ref[0, rows, :] * (ML_DK ** -0.5)
        v = v_ref[0, rows, :]
        bc = bce[d, rows, :]
        lic = lice[d, rows, :]
        br = brow[d, pl.ds(c, 1), :]
        lir = gr_ref[0, d, pl.ds(c, 1), :]
        b_end = bc[0:1] if d else bc[ML_CHUNK - 1:ML_CHUNK]
        valid = (ss >= tt) if d else (ss <= tt)
        d_log = jnp.where(valid, bc - br + lir, -jnp.inf)
        cols = []
        for hp in range(GROUP_W // LANES):
            xs = d_log[:, hp * LANES:(hp + 1) * LANES]
            cols.append(jnp.max(jnp.where(lane_h, xs, -jnp.inf), axis=-1, keepdims=True))
            cols.append(jnp.max(jnp.where(lane_h, -jnp.inf, xs), axis=-1, keepdims=True))
        w_inter = bc + m_prev
        m_t = jnp.maximum(w_inter, _lane_expand(cols, ML_CHUNK))
        e_inter = jnp.exp(w_inter - m_t)
        qb = q.astype(BF16)
        kbd = jnp.where(bd, jnp.concatenate([k] * ML_HEADS, axis=0).T, 0.0).astype(BF16)
        vbd = jnp.where(bd, jnp.concatenate([v] * ML_HEADS, axis=0), 0.0).astype(BF16)
        wts = jnp.exp(d_log - m_t) * jnp.dot(qb, kbd, preferred_element_type=F32)
        w_b = wts.astype(BF16)
        c_prev, n_prev = cst[d], nst[d]
        num = (jnp.dot(w_b, vbd, preferred_element_type=F32)
               + e_inter * jnp.dot(qb, c_prev.astype(BF16), preferred_element_type=F32))
        den = (jnp.dot(w_b, ebd, preferred_element_type=F32)
               + e_inter * jnp.dot(qb, n_prev.astype(BF16), preferred_element_type=F32))
        hsc[d, rows, :] = num / jnp.maximum(jnp.abs(den), jnp.exp(-m_t))
        w_end = b_end - bc + lic
        m_loc = jnp.max(w_end, axis=0, keepdims=True)
        e_end = jnp.exp(w_end - m_loc)
        m_new = jnp.maximum(b_end + m_prev, m_loc)
        a_old = jnp.exp(b_end + m_prev - m_new)
        a_loc = jnp.exp(m_loc - m_new)
        ev2 = jnp.concatenate([e_end * v, e_end], axis=1).astype(BF16)
        upd = lax.dot_general(k.astype(BF16), ev2, (((0,), (0,)), ((), ())), preferred_element_type=F32)
        cst[d] = a_old * c_prev + a_loc * jnp.where(bd, upd[:, :GROUP_W], 0.0)
        nst[d] = a_old * n_prev + a_loc * jnp.where(bd, upd[:, GROUP_W:], 0.0)
        return m_new

    def step(i, carry):
        return chunk(0, i, carry[0]), chunk(1, _bwd_chunk(i, n_ctx, n_all), carry[1])

    zero = jnp.zeros((1, GROUP_W), F32)
    lax.fori_loop(0, n_all, step, (zero, zero), unroll=4)
    out_ref[0] = _head_rms(hsc[0] + hsc[1], ebd, ML_DV) * nw_ref[...] * jax.nn.sigmoid(o_ref[0])


def _mlstm_mixer(z, gates_t, norm_w):
    bsz, t, _ = z.shape
    nc = t // ML_CHUNK
    ncp = -(-nc // 8) * 8
    gr = gates_t.reshape(bsz, 4, ML_HEADS, nc, ML_CHUNK).transpose(0, 1, 3, 2, 4).reshape(bsz, 4, nc, GROUP_W)
    gr = jnp.pad(gr, ((0, 0), (0, 0), (0, ncp - nc), (0, 0)))
    col = ML_OFF // GROUP_W
    zspec = lambda j: pl.BlockSpec((1, t, GROUP_W), lambda b: (b, 0, col + j))
    return pl.pallas_call(
        functools.partial(_mlstm_body, t_len=t),
        grid=(bsz,),
        in_specs=[zspec(0), zspec(1), zspec(2), zspec(3),
                  pl.BlockSpec((1, t, LANES), lambda b: (b, 0, ML_GATE_OFF // LANES)),
                  pl.BlockSpec((1, 4, ncp, GROUP_W), lambda b: (b, 0, 0, 0)),
                  pl.BlockSpec((1, GROUP_W), lambda b: (0, 0))],
        out_specs=pl.BlockSpec((1, t, GROUP_W), lambda b: (b, 0, 0)),
        out_shape=jax.ShapeDtypeStruct((bsz, t, GROUP_W), F32),
        scratch_shapes=[pltpu.VMEM((2, t, GROUP_W), F32), pltpu.VMEM((2, t, GROUP_W), F32),
                        pltpu.VMEM((2, ncp, GROUP_W), F32), pltpu.VMEM((2, t, GROUP_W), F32),
                        pltpu.VMEM((2, GROUP_W, GROUP_W), F32), pltpu.VMEM((2, GROUP_W, GROUP_W), F32)],
        compiler_params=_cparams("arbitrary"),
    )(z, z, z, z, z, gr, norm_w.astype(F32).reshape(1, GROUP_W))


def kernel(x, c, ctx, c_ctx, mod_w, mod_b, norm1_w, norm2_w, w_in, b_in, hg_lb_logits, hg_norm_w,
           s5_lam_re, s5_lam_im, s5_log_dt, s5_b_re, s5_b_im, s5_c_re, s5_c_im, s5_d, s5_glu_w, s5_glu_b,
           da_lq1, da_lk1, da_lq2, da_lk2, da_norm_w, ml_norm_w, w_out, router_w,
           exp_w1, exp_w3, exp_w2, final_norm_w):
    bsz, seq, d = x.shape
    assert ctx.shape[1] == CTX_LEN == ROW_TILE and seq % ROW_TILE == 0 and d == D_MODEL
    t = CTX_LEN + seq
    lb_all = jnp.cumsum(jax.nn.softmax(hg_lb_logits.astype(F32), axis=0), axis=0)
    lb_all = lb_all - lb_all[0]

    n_rows = -(-(bsz + 1) // 8) * 8
    cv = jnp.zeros((n_rows, d), F32).at[:bsz].set(c).at[bsz].set(c_ctx)
    mod_all = _modulation(cv, mod_w, mod_b)
    cos_t, sin_t = _rope_tables(t)
    h = jnp.concatenate([ctx, x], axis=1)
    w1, w3, w2 = exp_w1.astype(BF16), exp_w3.astype(BF16), exp_w2.astype(BF16)

    for li in range(DEPTH):
        ctx_out = li < DEPTH - 1
        m6 = mod_all[li].reshape(n_rows, 6, d)
        modsel = jnp.stack([jnp.broadcast_to(m6[bsz], (bsz, 6, d)), m6[:bsz]], axis=1)
        w_pad = jnp.pad(w_in[li], ((0, 0), (0, IN_COLS_PAD - IN_COLS))).astype(BF16)
        b_pad = jnp.pad(b_in[li], (0, IN_COLS_PAD - IN_COLS)).reshape(1, IN_COLS_PAD)
        wg_t = w_in[li][:, ML_GATE_OFF:].T.astype(BF16)
        bg_col = b_in[li][ML_GATE_OFF:].reshape(N_GATES, 1)
        z, gates_t = _in_projection(h, modsel, norm1_w[li], w_pad, b_pad, wg_t, bg_col)

        mix_a = _hgrn2_mixer(z, lb_all[li], hg_norm_w[li])
        mix_b = _s5_mixer(z, s5_lam_re[li], s5_lam_im[li], s5_log_dt[li], s5_b_re[li], s5_b_im[li],
                          s5_c_re[li], s5_c_im[li], s5_d[li], s5_glu_w[li], s5_glu_b[li])
        lam_init = 0.8 - 0.6 * math.exp(-0.3 * li)
        att_args = (da_lq1[li], da_lk1[li], da_lq2[li], da_lk2[li], da_norm_w[li], lam_init)
        c_l = _diff_attention(z, CTX_LEN, seq, t, True, cos_t, sin_t, *att_args)
        if ctx_out:
            c_c = _diff_attention(z, 0, CTX_LEN, CTX_LEN, False, cos_t, sin_t, *att_args)
        else:
            c_c = jnp.zeros((bsz, CTX_LEN, GROUP_W), F32)
        mix_c = jnp.concatenate([c_c, c_l], axis=1)
        mix_d = _mlstm_mixer(z, gates_t, ml_norm_w[li])

        hm, xm, aff_t = _out_projection((mix_a, mix_b, mix_c, mix_d), h, modsel, norm2_w[li],
                                        w_out[li].astype(BF16), router_w[li].T.astype(BF16))
        slot, slot_t = _route(aff_t, CTX_LEN, seq)
        y = _moe_ffn(xm, slot, aff_t, CTX_LEN, seq, 1, w1, w3, w2, li)
        if not ctx_out:
            return _moe_combine(y, slot_t, hm, modsel, CTX_LEN, seq, 1, final_w=final_norm_w)
        h = _moe_combine(y, slot_t, hm, modsel, CTX_LEN, seq, 1)
        slot, slot_t = _route(aff_t, 0, CTX_LEN)
        y = _moe_ffn(xm, slot, aff_t, 0, CTX_LEN, bsz, w1, w3, w2, li)
        h = _moe_combine(y, slot_t, h, modsel, 0, CTX_LEN, 0)
```
